```python
import math
import jax, jax.numpy as jnp
from jax import lax
import numpy as np

D_MODEL = 1024
BATCH = 16
SEQ = 256
DEPTH = 2
DEC_BATCH = 4
DEC_SEQ = 4096
PAST_LEN = 256

GRID_W = 64
N_A_LAYERS = (DEPTH + 1) // 2
N_B_LAYERS = DEPTH // 2
ML_HEADS = 4
ML_DK = D_MODEL // ML_HEADS
ML_DV = D_MODEL // ML_HEADS
ML_QK = ML_HEADS * ML_DK
ML_V = ML_HEADS * ML_DV
ML_CHUNK = 128
ML_CONV = 3
RET_HEADS = 4
RET_DK = D_MODEL // RET_HEADS
RET_DV = 2 * D_MODEL // RET_HEADS
RET_QK = RET_HEADS * RET_DK
RET_V = RET_HEADS * RET_DV
RET_CHUNK = 128
ROPE_BASE = 10000.0
D_FF = ((8 * D_MODEL // 3 + 127) // 128) * 128
N_EXPERTS = 8
TOP_K = 2
D_FF_EXPERT = D_FF
MOE_BLOCK = 256
N_MOD = 6
EPS = 1e-6

kernel_name = 'bidir_mlstm_retention_moe_flow_trunk'


def rms_norm(x, g):
    x32 = x.astype(jnp.float32)
    y = x32 * lax.rsqrt(jnp.mean(x32 * x32, -1, keepdims=True) + EPS)
    return y.astype(x.dtype) * g


def head_rms(x, g):
    H, d = x.shape[1], x.shape[3]
    y = x * lax.rsqrt(jnp.mean(x * x, -1, keepdims=True) + EPS)
    return y * g.astype(jnp.float32).reshape(1, H, 1, d)


def split_heads(x, h):
    B, N, _ = x.shape
    return x.reshape(B, N, h, -1).transpose(0, 2, 1, 3)


def merge_heads(x):
    B, H, N, d = x.shape
    return x.transpose(0, 2, 1, 3).reshape(B, N, H * d)


def to_chunks(a, L):
    B, H, N = a.shape[:3]
    return jnp.moveaxis(a.reshape(B, H, N // L, L, *a.shape[3:]), 2, 0)


def from_chunks(a):
    nc, B, H, L = a.shape[:4]
    return jnp.moveaxis(a, 0, 2).reshape(B, H, nc * L, *a.shape[4:])


def centred_dwconv(x, w, b):
    K, N = w.shape[0], x.shape[1]
    pad = K // 2
    xp = jnp.pad(x, ((0, 0), (pad, pad), (0, 0)))
    return sum(xp[:, t:t + N] * w[t] for t in range(K)) + b


def axial_rope_tables(n):
    rows = n // GRID_W
    row = jnp.repeat(jnp.arange(rows), GRID_W).astype(jnp.float32)
    col = jnp.tile(jnp.arange(GRID_W), rows).astype(jnp.float32)
    R = RET_DK // 4
    inv = 1.0 / (ROPE_BASE ** (jnp.arange(R, dtype=jnp.float32) / R))
    ang = jnp.stack([row[:, None] * inv, col[:, None] * inv], axis=1)
    return jnp.cos(ang), jnp.sin(ang)


def apply_rope2d(x, cos, sin):
    B, H, N, d = x.shape
    R = d // 4
    xr = x.reshape(B, H, N, 2, 2, R)
    x1, x2 = xr[..., 0, :], xr[..., 1, :]
    return jnp.stack([x1 * cos - x2 * sin, x2 * cos + x1 * sin], axis=-2).reshape(B, H, N, d)


def mlstm_scan(q, k, v, li, lf, state):
    L = ML_CHUNK
    mask = jnp.tril(jnp.ones((L, L), bool))

    def step(carry, xs):
        C, n, m = carry
        qc, kc, vc, ic, fc = xs
        b = jnp.cumsum(fc, -1)
        a = b + m[..., None]
        dlog = jnp.where(mask, b[..., :, None] - b[..., None, :] + ic[..., None, :], -jnp.inf)
        m_t = jnp.maximum(a, dlog.max(-1))
        w_inter = jnp.exp(a - m_t)
        s = jnp.einsum('bhid,bhjd->bhij', qc, kc) * jnp.exp(dlog - m_t[..., None])
        num = w_inter[..., None] * jnp.einsum('bhid,bhde->bhie', qc, C) + jnp.einsum('bhij,bhje->bhie', s, vc)
        den = w_inter * jnp.einsum('bhid,bhd->bhi', qc, n) + s.sum(-1)
        h = num / jnp.maximum(jnp.abs(den), jnp.exp(-m_t))[..., None]
        b_end = b[..., -1]
        g = b_end[..., None] - b + ic
        m_new = jnp.maximum(b_end + m, g.max(-1))
        w_c = jnp.exp(g - m_new[..., None])
        decay = jnp.exp(b_end + m - m_new)
        C_new = decay[..., None, None] * C + jnp.einsum('bhj,bhjd,bhje->bhde', w_c, kc, vc)
        n_new = decay[..., None] * n + jnp.einsum('bhj,bhjd->bhd', w_c, kc)
        return (C_new, n_new, m_new), h

    xs = tuple(to_chunks(a, L) for a in (q, k, v, li, lf))
    final, h = lax.scan(step, state, xs)
    return from_chunks(h), final


def mlstm_mixer(h, state, w_in, gate_b, conv_w, conv_b, head_g, w_out):
    B, N, _ = h.shape
    proj = h @ w_in
    qk = centred_dwconv(proj[..., :2 * ML_QK], conv_w, conv_b)
    v = proj[..., 2 * ML_QK:2 * ML_QK + ML_V]
    o = proj[..., 2 * ML_QK + ML_V:2 * ML_QK + 2 * ML_V]
    gates = (proj[..., 2 * ML_QK + 2 * ML_V:] + gate_b).astype(jnp.float32)
    gates = gates.reshape(B, N, 2, 2, ML_HEADS).transpose(2, 3, 0, 4, 1)
    q = split_heads(qk[..., :ML_QK], ML_HEADS).astype(jnp.float32) * ML_DK ** -0.5
    k = split_heads(qk[..., ML_QK:], ML_HEADS).astype(jnp.float32)
    vh = split_heads(v, ML_HEADS).astype(jnp.float32)
    C0, n0, m0 = (s.astype(jnp.float32) for s in state)
    flip = lambda a: jnp.flip(a, axis=2)
    h_f, (Cf, nf, mf) = mlstm_scan(q, k, vh, gates[0, 0], jax.nn.log_sigmoid(gates[0, 1]),
                                   (C0[:, 0], n0[:, 0], m0[:, 0]))
    h_b, (Cb, nb, mb) = mlstm_scan(flip(q), flip(k), flip(vh), flip(gates[1, 0]),
                                   flip(jax.nn.log_sigmoid(gates[1, 1])), (C0[:, 1], n0[:, 1], m0[:, 1]))
    hm = head_rms(h_f + flip(h_b), head_g)
    y = (jax.nn.sigmoid(o) * merge_heads(hm).astype(h.dtype)) @ w_out
    return y, (jnp.stack([Cf, Cb], 1), jnp.stack([nf, nb], 1), jnp.stack([mf, mb], 1))


def retention_scan(q, k, v, log_gamma, S0):
    L = RET_CHUNK
    pos = jnp.arange(L, dtype=jnp.float32)
    rel = pos[:, None] - pos[None, :]
    dmat = jnp.where(rel >= 0, jnp.exp(log_gamma[:, None, None] * jnp.maximum(rel, 0.0)), 0.0)
    q_dec = jnp.exp(log_gamma[:, None] * (pos + 1.0))[..., None]
    k_dec = jnp.exp(log_gamma[:, None] * (L - 1.0 - pos))[..., None]
    c_dec = jnp.exp(log_gamma * L)[:, None, None]

    def step(S, xs):
        qc, kc, vc = xs
        s = jnp.einsum('bhid,bhjd->bhij', qc, kc) * dmat
        out = jnp.einsum('bhij,bhje->bhie', s, vc) + jnp.einsum('bhid,bhde->bhie', qc * q_dec, S)
        S_new = c_dec * S + jnp.einsum('bhjd,bhje->bhde', kc * k_dec, vc)
        return S_new, out

    S, out = lax.scan(step, S0, tuple(to_chunks(a, L) for a in (q, k, v)))
    return from_chunks(out), S


def retention_mixer(h, S0, w_in, decay_logit, head_g, w_out, rope):
    proj = h @ w_in
    q = split_heads(proj[..., :RET_QK], RET_HEADS).astype(jnp.float32)
    k = split_heads(proj[..., RET_QK:2 * RET_QK], RET_HEADS).astype(jnp.float32)
    v = split_heads(proj[..., 2 * RET_QK:2 * RET_QK + RET_V], RET_HEADS).astype(jnp.float32)
    g = proj[..., 2 * RET_QK + RET_V:]
    if rope is not None:
        q = apply_rope2d(q, rope[0], rope[1])
        k = apply_rope2d(k, rope[0], rope[1])
    k = k * RET_DK ** -0.5
    lg = jax.nn.log_sigmoid(decay_logit.astype(jnp.float32))
    S0 = S0.astype(jnp.float32)
    flip = lambda a: jnp.flip(a, axis=2)
    o_f, Sf = retention_scan(q, k, v, lg[0], S0[:, 0])
    o_b, Sb = retention_scan(flip(q), flip(k), flip(v), lg[1], S0[:, 1])
    o = head_rms(o_f + flip(o_b), head_g)
    y = (jax.nn.silu(g) * merge_heads(o).astype(h.dtype)) @ w_out
    return y, jnp.stack([Sf, Sb], 1)


def swiglu(x, wg, wu, wd):
    return (jax.nn.silu(x @ wg) * (x @ wu)) @ wd


def moe_swiglu(x, router_w, w_gate, w_up, w_down):
    B, N, D = x.shape
    xt = x.reshape(-1, D)
    T = xt.shape[0]
    logits = (xt @ router_w).astype(jnp.float32)
    top_val, top_idx = lax.top_k(logits, TOP_K)
    gates = jax.nn.softmax(top_val, -1).astype(x.dtype)
    flat_e = top_idx.reshape(-1).astype(jnp.int32)
    flat_tok = jnp.repeat(jnp.arange(T, dtype=jnp.int32), TOP_K)
    flat_g = gates.reshape(-1)
    order = jnp.argsort(flat_e)
    se, stok, sg = flat_e[order], flat_tok[order], flat_g[order]
    counts = jnp.bincount(flat_e, length=N_EXPERTS).astype(jnp.int32)
    padded = ((counts + MOE_BLOCK - 1) // MOE_BLOCK) * MOE_BLOCK
    ends = jnp.cumsum(padded)
    pad_start = ends - padded
    start = jnp.cumsum(counts) - counts
    dest = pad_start[se] + jnp.arange(T * TOP_K, dtype=jnp.int32) - start[se]
    n_blocks = -(-(T * TOP_K) // MOE_BLOCK) + N_EXPERTS
    P = n_blocks * MOE_BLOCK
    slot_tok = jnp.full((P,), T, jnp.int32).at[dest].set(stok)
    x_pad = jnp.concatenate([xt, jnp.zeros((1, D), xt.dtype)], 0)
    xb = x_pad[slot_tok].reshape(n_blocks, MOE_BLOCK, D)
    block_e = jnp.minimum(jnp.searchsorted(ends, jnp.arange(n_blocks, dtype=jnp.int32) * MOE_BLOCK, side='right'),
                          N_EXPERTS - 1)

    def expert_block(args):
        xb_i, e = args
        return swiglu(xb_i, w_gate[e], w_up[e], w_down[e])

    yb = lax.map(expert_block, (xb, block_e)).reshape(P, D)
    y_assign = yb[dest] * sg[:, None]
    out = jnp.zeros((T, D), x.dtype).at[stok].add(y_assign)
    return out.reshape(B, N, D)


def modulation(cond, w, b):
    m = (cond @ w + b).reshape(cond.shape[0], 1, N_MOD, D_MODEL)
    return tuple(m[:, :, i] for i in range(N_MOD))


def adaln_in(x, g, shift, scale):
    return rms_norm(x, g) * (1 + scale) + shift


def gated_residual(x, y, g, gate):
    return x + gate * rms_norm(y, g)


def setup_inputs(seed: int = 0) -> dict:
    key = jax.random.key(seed)
    ks = iter(jax.random.split(key, 40))
    nrm = lambda shape, scale: jax.random.normal(next(ks), shape, jnp.float32) * scale
    D = D_MODEL
    ml_in_cols = 2 * ML_QK + 2 * ML_V + 4 * ML_HEADS
    i_b = nrm((N_A_LAYERS, 2, 1, ML_HEADS), 0.1)
    f_b = jnp.linspace(3.0, 6.0, ML_HEADS)[None, None, None, :] + nrm((N_A_LAYERS, 2, 1, ML_HEADS), 0.1)
    mlstm_gate_b = jnp.concatenate([i_b, f_b], axis=2).reshape(N_A_LAYERS, 4 * ML_HEADS)
    base_logit = jnp.log(2.0 ** (5.0 + jnp.arange(RET_HEADS, dtype=jnp.float32)) - 1.0)
    ret_decay_logit = base_logit[None, None, :] + nrm((N_B_LAYERS, 2, RET_HEADS), 0.1)
    return {
        'x_prompt': nrm((BATCH, SEQ, D), 1.0),
        'x_sample': nrm((DEC_BATCH, DEC_SEQ, D), 1.0),
        'state_mlstm_C': nrm((DEC_BATCH, N_A_LAYERS, 2, ML_HEADS, ML_DK, ML_DV), 0.5),
        'state_mlstm_n': nrm((DEC_BATCH, N_A_LAYERS, 2, ML_HEADS, ML_DK), 0.5),
        'state_mlstm_m': nrm((DEC_BATCH, N_A_LAYERS, 2, ML_HEADS), 1.0),
        'state_ret_S': nrm((DEC_BATCH, N_B_LAYERS, 2, RET_HEADS, RET_DK, RET_DV), 1.0),
        'c': nrm((DEC_BATCH, D), 1.0),
        'c_ctx': nrm((D,), 1.0),
        'mod_w': nrm((DEPTH, D, N_MOD * D), D ** -0.5),
        'mod_b': nrm((DEPTH, N_MOD * D), 0.02),
        'norm_g': 1.0 + nrm((DEPTH, 4, D), 0.02),
        'mlstm_w_in': nrm((N_A_LAYERS, D, ml_in_cols), D ** -0.5),
        'mlstm_gate_b': mlstm_gate_b,
        'mlstm_conv_w': nrm((N_A_LAYERS, ML_CONV, 2 * ML_QK), ML_CONV ** -0.5),
        'mlstm_conv_b': nrm((N_A_LAYERS, 2 * ML_QK), 0.02),
        'mlstm_head_g': 1.0 + nrm((N_A_LAYERS, ML_V), 0.02),
        'mlstm_w_out': nrm((N_A_LAYERS, ML_V, D), ML_V ** -0.5),
        'ret_w_in': nrm((N_B_LAYERS, D, 2 * RET_QK + 2 * RET_V), D ** -0.5),
        'ret_decay_logit': ret_decay_logit,
        'ret_head_g': 1.0 + nrm((N_B_LAYERS, RET_V), 0.02),
        'ret_w_out': nrm((N_B_LAYERS, RET_V, D), RET_V ** -0.5),
        'ffn_w_gate': nrm((N_A_LAYERS, D, D_FF), D ** -0.5),
        'ffn_w_up': nrm((N_A_LAYERS, D, D_FF), D ** -0.5),
        'ffn_w_down': nrm((N_A_LAYERS, D_FF, D), D_FF ** -0.5),
        'moe_router': nrm((N_B_LAYERS, D, N_EXPERTS), D ** -0.5),
        'moe_w_gate': nrm((N_B_LAYERS, N_EXPERTS, D, D_FF_EXPERT), D ** -0.5),
        'moe_w_up': nrm((N_B_LAYERS, N_EXPERTS, D, D_FF_EXPERT), D ** -0.5),
        'moe_w_down': nrm((N_B_LAYERS, N_EXPERTS, D_FF_EXPERT, D), D_FF_EXPERT ** -0.5),
    }


def reference(x_prompt, x_sample, state_mlstm_C, state_mlstm_n, state_mlstm_m, state_ret_S, c, c_ctx,
              mod_w, mod_b, norm_g, mlstm_w_in, mlstm_gate_b, mlstm_conv_w, mlstm_conv_b, mlstm_head_g,
              mlstm_w_out, ret_w_in, ret_decay_logit, ret_head_g, ret_w_out, ffn_w_gate, ffn_w_up,
              ffn_w_down, moe_router, moe_w_gate, moe_w_up, moe_w_down):
    dt = x_prompt.dtype
    bp = x_prompt.shape[0]
    rope_lat = axial_rope_tables(x_sample.shape[1])
    cond_ctx = jax.nn.silu(c_ctx)[None]
    cond_lat = jax.nn.silu(c)
    xp, xs = x_prompt, x_sample
    new_C, new_n, new_m, new_S = [], [], [], []
    for layer in range(DEPTH):
        j = layer // 2
        g = norm_g[layer]
        sh_p, sc_p, ga_p, fsh_p, fsc_p, fga_p = modulation(cond_ctx, mod_w[layer], mod_b[layer])
        sh_s, sc_s, ga_s, fsh_s, fsc_s, fga_s = modulation(cond_lat, mod_w[layer], mod_b[layer])
        hp = adaln_in(xp, g[0], sh_p, sc_p)
        hs = adaln_in(xs, g[0], sh_s, sc_s)
        if layer % 2 == 0:
            w = (mlstm_w_in[j], mlstm_gate_b[j], mlstm_conv_w[j], mlstm_conv_b[j], mlstm_head_g[j], mlstm_w_out[j])
            zero_state = (jnp.zeros((bp, 2, ML_HEADS, ML_DK, ML_DV), jnp.float32),
                          jnp.zeros((bp, 2, ML_HEADS, ML_DK), jnp.float32),
                          jnp.zeros((bp, 2, ML_HEADS), jnp.float32))
            yp, (Cp, nP, mP) = mlstm_mixer(hp, zero_state, *w)
            ys, _ = mlstm_mixer(hs, (state_mlstm_C[:, j], state_mlstm_n[:, j], state_mlstm_m[:, j]), *w)
            new_C.append(Cp.astype(dt))
            new_n.append(nP.astype(dt))
            new_m.append(mP.astype(dt))
        else:
            w = (ret_w_in[j], ret_decay_logit[j], ret_head_g[j], ret_w_out[j])
            yp, Sp = retention_mixer(hp, jnp.zeros((bp, 2, RET_HEADS, RET_DK, RET_DV), jnp.float32), *w, rope=None)
            ys, _ = retention_mixer(hs, state_ret_S[:, j], *w, rope=rope_lat)
            new_S.append(Sp.astype(dt))
        xp = gated_residual(xp, yp, g[1], ga_p)
        xs = gated_residual(xs, ys, g[1], ga_s)
        hp = adaln_in(xp, g[2], fsh_p, fsc_p)
        hs = adaln_in(xs, g[2], fsh_s, fsc_s)
        if layer % 2 == 0:
            fp = swiglu(hp, ffn_w_gate[j], ffn_w_up[j], ffn_w_down[j])
            fs = swiglu(hs, ffn_w_gate[j], ffn_w_up[j], ffn_w_down[j])
        else:
            fp = moe_swiglu(hp, moe_router[j], moe_w_gate[j], moe_w_up[j], moe_w_down[j])
            fs = moe_swiglu(hs, moe_router[j], moe_w_gate[j], moe_w_up[j], moe_w_down[j])
        xp = gated_residual(xp, fp, g[3], fga_p)
        xs = gated_residual(xs, fs, g[3], fga_s)
    new_mlstm_C = jnp.stack(new_C, axis=1)
    new_mlstm_n = jnp.stack(new_n, axis=1)
    new_mlstm_m = jnp.stack(new_m, axis=1)
    new_ret_S = jnp.stack(new_S, axis=1)
    return (xp, xs, new_mlstm_C, new_mlstm_n, new_mlstm_m, new_ret_S)
```

```python
import functools
import math

import jax
import jax.numpy as jnp
from jax import lax
from jax.experimental import pallas as pl
from jax.experimental.pallas import tpu as pltpu

F32 = jnp.float32
BF16 = jnp.bfloat16

EPS = 1e-6
N_MOD = 6
HEADS = 4
CHUNK = 128
GRID_W = 64
ROPE_BASE = 10000.0
N_EXPERTS = 8
TOP_K = 2
MOE_BLOCK = 256
LANES = 128
SUBLANES = 8
ROW_TILE = 256
MOD_ROWS = 8
MIB = 1024 * 1024


def _cparams(sem, vmem_mib):
    return pltpu.CompilerParams(dimension_semantics=sem, vmem_limit_bytes=vmem_mib * MIB)


def _dot(a, b):
    return jnp.dot(a, b, preferred_element_type=F32)


def _dot_nt(a, b):
    return lax.dot_general(a, b, (((1,), (1,)), ((), ())), preferred_element_type=F32)


def _dot_tn(a, b):
    return lax.dot_general(a, b, (((0,), (0,)), ((), ())), preferred_element_type=F32)


def _split_bf16(x):
    hi = x.astype(BF16)
    lo = (x - hi.astype(F32)).astype(BF16)
    return hi, lo


def _rms(x, g):
    return x * lax.rsqrt(jnp.mean(x * x, -1, keepdims=True) + EPS) * g


def _adaln(x, g, shift, scale):
    return _rms(x, g) * (1.0 + scale) + shift


def _silu(x):
    return x * jax.nn.sigmoid(x)


def _logsig(x):
    return jnp.minimum(x, 0.0) - jnp.log1p(jnp.exp(-jnp.abs(x)))


def _store_row_tiles(ref, x):
    rows = x.shape[0]
    for s in range(SUBLANES):
        ref[pl.ds(s, rows, stride=SUBLANES), :] = x[:, s * LANES:(s + 1) * LANES]


def _load_row_tiles(ref, rows):
    return jnp.concatenate([ref[pl.ds(s, rows, stride=SUBLANES), :] for s in range(SUBLANES)], axis=1)


def _const_spec(shape):
    nd = len(shape)
    return pl.BlockSpec(shape, lambda *_: (0,) * nd, pipeline_mode=pl.Buffered(1))


def _mod_spec(d, layer, comp, rows_per_batch, first_row):
    if rows_per_batch is None:
        return pl.BlockSpec((None, None, None, 1, d), lambda i: (layer, first_row, comp, 0, 0))
    return pl.BlockSpec((None, None, None, 1, d), lambda i: (layer, first_row + i // rows_per_batch, comp, 0, 0))


def _mod_kernel(c_ref, w_ref, b_ref, o_ref):
    s = _silu(c_ref[...]).astype(BF16)
    o_ref[...] = _dot(s, w_ref[...].astype(BF16)) + b_ref[...]


def _modulation(cond, mod_w, mod_b):
    depth, d, n = mod_w.shape
    tn = n // 4
    return pl.pallas_call(
        _mod_kernel,
        grid=(depth, n // tn),
        in_specs=[pl.BlockSpec((MOD_ROWS, d), lambda l, j: (0, 0)),
                  pl.BlockSpec((None, d, tn), lambda l, j: (l, 0, j)),
                  pl.BlockSpec((None, 1, tn), lambda l, j: (l, 0, j))],
        out_specs=pl.BlockSpec((None, MOD_ROWS, tn), lambda l, j: (l, 0, j)),
        out_shape=jax.ShapeDtypeStruct((depth, MOD_ROWS, n), F32),
        compiler_params=_cparams(("arbitrary", "arbitrary"), 40),
        name="modulation",
    )(cond, mod_w, mod_b.reshape(depth, 1, n))


def _proj_mlstm_kernel(x_ref, g_ref, sh_ref, sc_ref, wqk_ref, wv_ref, wo_ref, wgh_ref, wgl_ref,
                       wgth_ref, wgtl_ref, bcol_ref, brow_ref,
                       qk_ref, v_ref, o_ref, gcol_ref, grow_ref):
    h = _adaln(x_ref[...], g_ref[...], sh_ref[...], sc_ref[...])
    hb, hl = _split_bf16(h)
    qk_ref[...] = _dot(hb, wqk_ref[...])
    v_ref[...] = _dot(hb, wv_ref[...]).astype(BF16)
    o_ref[...] = _dot(hb, wo_ref[...])
    gc = _dot(hb, wgh_ref[...]) + _dot(hl, wgh_ref[...]) + _dot(hb, wgl_ref[...]) + bcol_ref[...]
    for hh in range(HEADS):
        gcol_ref[hh] = gc if hh == 0 else pltpu.roll(gc, LANES - hh, axis=1)
    grow_ref[...] = (_dot_nt(wgth_ref[...], hb) + _dot_nt(wgth_ref[...], hl) + _dot_nt(wgtl_ref[...], hb)
                     + brow_ref[...])


def _proj_mlstm(x, mods6, layer, first_row, rows_per_batch, g, wqk, wv, wo, wgh, wgl, wgth, wgtl, bcol, brow):
    t, d = x.shape
    tm = ROW_TILE
    ms = lambda comp: _mod_spec(d, layer, comp, rows_per_batch, first_row)
    ng = wgth.shape[0]
    return pl.pallas_call(
        _proj_mlstm_kernel,
        grid=(t // tm,),
        in_specs=[pl.BlockSpec((tm, d), lambda i: (i, 0)),
                  _const_spec((1, d)), ms(0), ms(1),
                  _const_spec(wqk.shape), _const_spec(wv.shape), _const_spec(wo.shape),
                  _const_spec(wgh.shape), _const_spec(wgl.shape), _const_spec(wgth.shape), _const_spec(wgtl.shape),
                  _const_spec(bcol.shape), _const_spec(brow.shape)],
        out_specs=[pl.BlockSpec((tm, wqk.shape[1]), lambda i: (i, 0)),
                   pl.BlockSpec((tm, wv.shape[1]), lambda i: (i, 0)),
                   pl.BlockSpec((tm, wo.shape[1]), lambda i: (i, 0)),
                   pl.BlockSpec((HEADS, tm, LANES), lambda i: (0, i, 0)),
                   pl.BlockSpec((ng, tm), lambda i: (0, i))],
        out_shape=[jax.ShapeDtypeStruct((t, wqk.shape[1]), F32),
                   jax.ShapeDtypeStruct((t, wv.shape[1]), BF16),
                   jax.ShapeDtypeStruct((t, wo.shape[1]), F32),
                   jax.ShapeDtypeStruct((HEADS, t, LANES), F32),
                   jax.ShapeDtypeStruct((ng, t), F32)],
        compiler_params=_cparams(("arbitrary",), 48),
        name="proj_mlstm",
    )(x, g, mods6, mods6, wqk, wv, wo, wgh, wgl, wgth, wgtl, bcol, brow)


def _make_mlstm_scan_kernel(n_tok, dk, dv, has_state, emit_state):
    L = CHUNK
    nc = n_tok // L
    assert nc % 2 == 0

    def kernel(*refs):
        it = iter(refs)
        m0_ref = next(it)
        q_ref, k_ref, v_ref, gcol_ref = next(it), next(it), next(it), next(it)
        gi0, gf0, gi1, gf1 = next(it), next(it), next(it), next(it)
        gi_refs, gf_refs = (gi0, gi1), (gf0, gf1)
        cwq_ref, cwk_ref, cbq_ref, cbk_ref = next(it), next(it), next(it), next(it)
        if has_state:
            c0_ref, n0_ref = next(it), next(it)
        out_ref = next(it)
        if emit_state:
            cout_ref, nout_ref, mout_ref = next(it), next(it), next(it)
        qs, ks, cst = next(it), next(it), next(it)

        b = pl.program_id(0)
        hh = pl.program_id(1)

        rowi = lax.broadcasted_iota(jnp.int32, (L, dk), 0)

        def conv_chunk(c, _):
            r0 = pl.multiple_of(c * L, L)
            for src, cw_ref, cb_ref, dst, scale in ((q_ref, cwq_ref, cbq_ref, qs, dk ** -0.5),
                                                    (k_ref, cwk_ref, cbk_ref, ks, 1.0)):
                xc = src[pl.ds(r0, L), :]
                prev_row = src[pl.ds(jnp.maximum(r0 - 1, 0), 1), :] * jnp.where(c > 0, 1.0, 0.0)
                next_row = src[pl.ds(jnp.minimum(r0 + L, n_tok - 1), 1), :] * jnp.where(c < nc - 1, 1.0, 0.0)
                prev = jnp.where(rowi == 0, prev_row, pltpu.roll(xc, 1, axis=0))
                nxt = jnp.where(rowi == L - 1, next_row, pltpu.roll(xc, L - 1, axis=0))
                y = prev * cw_ref[0:1, :] + xc * cw_ref[1:2, :] + nxt * cw_ref[2:3, :] + cb_ref[...]
                dst[pl.ds(r0, L), :] = (y * scale).astype(BF16)
            return 0

        lax.fori_loop(0, nc, conv_chunk, 0)

        if has_state:
            cst[...] = c0_ref[...]
            n_init = tuple(n0_ref[d, pl.ds(hh, 1), :] for d in range(2))
        else:
            cst[...] = jnp.zeros_like(cst)
            n_init = tuple(jnp.zeros((1, dk), F32) for _ in range(2))
        m_init = tuple(jnp.full((1, 1), m0_ref[(b * 2 + d) * HEADS + hh], F32) for d in range(2))

        ri = lax.broadcasted_iota(jnp.int32, (L, L), 0)
        ci = lax.broadcasted_iota(jnp.int32, (L, L), 1)
        lower = ri >= ci
        upper = ri <= ci
        masks = ((lower, upper), (upper, lower))

        def chunk(d, c, n, m):
            mk, mkt = masks[d]
            r0 = pl.multiple_of(c * L, L)
            q = qs[pl.ds(r0, L), :]
            k = ks[pl.ds(r0, L), :]
            v = v_ref[pl.ds(r0, L), :]
            gc = gcol_ref[pl.ds(r0, L), :]
            i_col = gc[:, 8 * d:8 * d + 1]
            f_col = _logsig(gc[:, 8 * d + 4:8 * d + 5])
            i_row = gi_refs[d][:, pl.ds(r0, L)]
            f_row = _logsig(gf_refs[d][:, pl.ds(r0, L)])
            b_col = jnp.sum(jnp.where(mk, f_row, 0.0), axis=1, keepdims=True)
            b_row = jnp.sum(jnp.where(mkt, f_col, 0.0), axis=0, keepdims=True)
            dlog = jnp.where(mk, b_col - b_row + i_row, -jnp.inf)
            a = b_col + m
            m_t = jnp.maximum(a, jnp.max(dlog, axis=1, keepdims=True))
            w_inter = jnp.exp(a - m_t)
            s = _dot_nt(q, k) * jnp.exp(dlog - m_t)
            c_old = cst[d]
            num = w_inter * _dot(q, c_old.astype(BF16)) + _dot(s.astype(BF16), v)
            den = w_inter * jnp.sum(q.astype(F32) * n, axis=1, keepdims=True) + jnp.sum(s, axis=1, keepdims=True)
            h = num * (1.0 / jnp.maximum(jnp.abs(den), jnp.exp(-m_t)))
            b_end = jnp.sum(f_row, axis=1, keepdims=True)
            g = b_end - b_col + i_col
            m_new = jnp.maximum(b_end + m, jnp.max(g, axis=0, keepdims=True))
            w_c = jnp.exp(g - m_new)
            decay = jnp.exp(b_end + m - m_new)
            kw = k.astype(F32) * w_c
            cst[d] = decay * c_old + _dot_tn(kw.astype(BF16), v)
            n_new = decay * n + jnp.sum(kw, axis=0, keepdims=True)
            return h, n_new, m_new

        def make_body(accumulate):
            def body(i, carry):
                nf, mf, nb, mb = carry
                hf, nf, mf = chunk(0, i, nf, mf)
                rf = pl.multiple_of(i * L, L)
                cb = nc - 1 - i
                hb, nb, mb = chunk(1, cb, nb, mb)
                rb = pl.multiple_of(cb * L, L)
                if accumulate:
                    out_ref[pl.ds(rf, L), :] += hf
                    out_ref[pl.ds(rb, L), :] += hb
                else:
                    out_ref[pl.ds(rf, L), :] = hf
                    out_ref[pl.ds(rb, L), :] = hb
                return nf, mf, nb, mb
            return body

        carry = (n_init[0], m_init[0], n_init[1], m_init[1])
        carry = lax.fori_loop(0, nc // 2, make_body(False), carry)
        carry = lax.fori_loop(nc // 2, nc, make_body(True), carry)

        if emit_state:
            cout_ref[...] = cst[...]
            for d in range(2):
                nout_ref[d, pl.ds(hh, 1), :] = carry[2 * d]
                mout_ref[d, pl.ds(hh, 1), :] = jnp.broadcast_to(carry[2 * d + 1], (1, LANES))

    return kernel


def _mlstm_scan(qk, v, gcol, grow3, conv_w, conv_b, m0_flat, state, n_seq, n_tok, emit_state):
    dk = qk.shape[1] // (2 * HEADS)
    dv = v.shape[1] // HEADS
    has_state = state is not None
    kern = _make_mlstm_scan_kernel(n_tok, dk, dv, has_state, emit_state)
    grow_spec = lambda r: pl.BlockSpec((None, 1, n_tok), lambda b, h, m: (r * HEADS + h, 0, b))
    in_specs = [pl.BlockSpec((n_tok, dk), lambda b, h, m: (b, h)),
                pl.BlockSpec((n_tok, dk), lambda b, h, m: (b, HEADS + h)),
                pl.BlockSpec((n_tok, dv), lambda b, h, m: (b, h)),
                pl.BlockSpec((None, n_tok, LANES), lambda b, h, m: (h, b, 0)),
                grow_spec(0), grow_spec(1), grow_spec(2), grow_spec(3),
                pl.BlockSpec((3, dk), lambda b, h, m: (0, h)),
                pl.BlockSpec((3, dk), lambda b, h, m: (0, HEADS + h)),
                pl.BlockSpec((1, dk), lambda b, h, m: (0, h)),
                pl.BlockSpec((1, dk), lambda b, h, m: (0, HEADS + h))]
    args = [qk, qk, v, gcol, grow3, grow3, grow3, grow3, conv_w, conv_w, conv_b, conv_b]
    if has_state:
        in_specs += [pl.BlockSpec((None, 2, None, dk, dv), lambda b, h, m: (b, 0, h, 0, 0)),
                     pl.BlockSpec((None, 2, HEADS, dk), lambda b, h, m: (b, 0, 0, 0))]
        args += list(state)
    out_specs = [pl.BlockSpec((n_tok, dv), lambda b, h, m: (b, h))]
    out_shape = [jax.ShapeDtypeStruct((n_seq * n_tok, HEADS * dv), F32)]
    if emit_state:
        out_specs += [pl.BlockSpec((None, 2, None, dk, dv), lambda b, h, m: (b, 0, h, 0, 0)),
                      pl.BlockSpec((None, 2, HEADS, dk), lambda b, h, m: (b, 0, 0, 0)),
                      pl.BlockSpec((None, 2, HEADS, LANES), lambda b, h, m: (b, 0, 0, 0))]
        out_shape += [jax.ShapeDtypeStruct((n_seq, 2, HEADS, dk, dv), F32),
                      jax.ShapeDtypeStruct((n_seq, 2, HEADS, dk), F32),
                      jax.ShapeDtypeStruct((n_seq, 2, HEADS, LANES), F32)]
    grid_spec = pltpu.PrefetchScalarGridSpec(
        num_scalar_prefetch=1, grid=(n_seq, HEADS), in_specs=in_specs, out_specs=out_specs,
        scratch_shapes=[pltpu.VMEM((n_tok, dk), BF16), pltpu.VMEM((n_tok, dk), BF16),
                        pltpu.VMEM((2, dk, dv), F32)])
    return pl.pallas_call(
        kern, grid_spec=grid_spec, out_shape=out_shape,
        compiler_params=_cparams(("arbitrary", "arbitrary"), 56),
        name="mlstm_scan",
    )(m0_flat, *args)


def _proj_ret_kernel(x_ref, g_ref, sh_ref, sc_ref, wqk_ref, wv_ref, wg_ref, qk_ref, v_ref, gate_ref):
    hb = _adaln(x_ref[...], g_ref[...], sh_ref[...], sc_ref[...]).astype(BF16)
    qk_ref[...] = _dot(hb, wqk_ref[...])
    v_ref[...] = _dot(hb, wv_ref[...]).astype(BF16)
    gate_ref[...] = _dot(hb, wg_ref[...])


def _proj_ret(x, mods6, layer, first_row, rows_per_batch, g, wqk, wv, wg):
    t, d = x.shape
    tm = ROW_TILE
    ms = lambda comp: _mod_spec(d, layer, comp, rows_per_batch, first_row)
    return pl.pallas_call(
        _proj_ret_kernel,
        grid=(t // tm,),
        in_specs=[pl.BlockSpec((tm, d), lambda i: (i, 0)), _const_spec((1, d)), ms(0), ms(1),
                  _const_spec(wqk.shape), _const_spec(wv.shape), _const_spec(wg.shape)],
        out_specs=[pl.BlockSpec((tm, wqk.shape[1]), lambda i: (i, 0)),
                   pl.BlockSpec((tm, wv.shape[1]), lambda i: (i, 0)),
                   pl.BlockSpec((tm, wg.shape[1]), lambda i: (i, 0))],
        out_shape=[jax.ShapeDtypeStruct((t, wqk.shape[1]), F32),
                   jax.ShapeDtypeStruct((t, wv.shape[1]), BF16),
                   jax.ShapeDtypeStruct((t, wg.shape[1]), F32)],
        compiler_params=_cparams(("arbitrary",), 48),
        name="proj_ret",
    )(x, g, mods6, mods6, wqk, wv, wg)


def _make_ret_scan_kernel(n_tok, dk, dv, has_state, emit_state, rope):
    L = CHUNK
    nc = n_tok // L
    assert nc % 2 == 0
    rows_per_chunk = L // GRID_W

    def kernel(*refs):
        it = iter(refs)
        dl_ref = next(it)
        q_ref, k_ref, v_ref = next(it), next(it), next(it)
        if rope:
            rcos_ref, rsin_ref, ccos_ref, csin_ref = next(it), next(it), next(it), next(it)
        if has_state:
            s0_ref = next(it)
        out_ref = next(it)
        if emit_state:
            sout_ref = next(it)
        sst = next(it)

        hh = pl.program_id(1)
        ri = lax.broadcasted_iota(jnp.int32, (L, L), 0)
        ci = lax.broadcasted_iota(jnp.int32, (L, L), 1)
        pos = lax.broadcasted_iota(jnp.int32, (L, 1), 0).astype(F32)
        rel = (ri - ci).astype(F32)

        dmat, q_dec, k_dec, c_dec = [], [], [], []
        for d in range(2):
            lg = _logsig(jnp.full((1, 1), dl_ref[d * HEADS + hh], F32))
            if d == 0:
                dmat.append(jnp.where(ri >= ci, jnp.exp(lg * jnp.maximum(rel, 0.0)), 0.0))
                q_dec.append(jnp.exp(lg * (pos + 1.0)))
                k_dec.append(jnp.exp(lg * (L - 1.0 - pos)))
            else:
                dmat.append(jnp.where(ri <= ci, jnp.exp(lg * jnp.maximum(-rel, 0.0)), 0.0))
                q_dec.append(jnp.exp(lg * (L - pos)))
                k_dec.append(jnp.exp(lg * pos))
            c_dec.append(jnp.exp(lg * float(L)))

        if has_state:
            sst[...] = s0_ref[...]
        else:
            sst[...] = jnp.zeros_like(sst)

        if rope:
            tok = lax.broadcasted_iota(jnp.int32, (L, LANES), 0)

        def rotate(x, c):
            if not rope:
                return x
            rc, rs = ccos_ref[...], csin_ref[...]
            row_c = rcos_ref[c * rows_per_chunk]
            row_s = rsin_ref[c * rows_per_chunk]
            for j in range(1, rows_per_chunk):
                sel = tok >= j * GRID_W
                row_c = jnp.where(sel, rcos_ref[c * rows_per_chunk + j], row_c)
                row_s = jnp.where(sel, rsin_ref[c * rows_per_chunk + j], row_s)
            xa, xb = x[:, :LANES], x[:, LANES:]
            ya = xa * row_c + pltpu.roll(xa, LANES // 2, axis=1) * row_s
            yb = xb * rc + pltpu.roll(xb, LANES // 2, axis=1) * rs
            return jnp.concatenate([ya, yb], axis=1)

        def chunk(d, c):
            r0 = pl.multiple_of(c * L, L)
            q = rotate(q_ref[pl.ds(r0, L), :], c)
            k = rotate(k_ref[pl.ds(r0, L), :], c) * dk ** -0.5
            v = v_ref[pl.ds(r0, L), :]
            qb = q.astype(BF16)
            kb = k.astype(BF16)
            s = _dot_nt(qb, kb) * dmat[d]
            s_old = sst[d]
            out = _dot(s.astype(BF16), v) + _dot((q * q_dec[d]).astype(BF16), s_old.astype(BF16))
            sst[d] = c_dec[d] * s_old + _dot_tn((k * k_dec[d]).astype(BF16), v)
            return out

        def body1(i, _):
            rf = pl.multiple_of(i * L, L)
            out_ref[pl.ds(rf, L), :] = chunk(0, i)
            cb = nc - 1 - i
            rb = pl.multiple_of(cb * L, L)
            out_ref[pl.ds(rb, L), :] = chunk(1, cb)
            return 0

        def body2(i, _):
            rf = pl.multiple_of(i * L, L)
            out_ref[pl.ds(rf, L), :] += chunk(0, i)
            cb = nc - 1 - i
            rb = pl.multiple_of(cb * L, L)
            out_ref[pl.ds(rb, L), :] += chunk(1, cb)
            return 0

        lax.fori_loop(0, nc // 2, body1, 0)
        lax.fori_loop(nc // 2, nc, body2, 0)
        if emit_state:
            sout_ref[...] = sst[...]

    return kernel


def _ret_scan(qk, v, decay_flat, rope_tabs, state, n_seq, n_tok, emit_state):
    dk = qk.shape[1] // (2 * HEADS)
    dv = v.shape[1] // HEADS
    has_state = state is not None
    rope = rope_tabs is not None
    kern = _make_ret_scan_kernel(n_tok, dk, dv, has_state, emit_state, rope)
    in_specs = [pl.BlockSpec((n_tok, dk), lambda b, h, m: (b, h)),
                pl.BlockSpec((n_tok, dk), lambda b, h, m: (b, HEADS + h)),
                pl.BlockSpec((n_tok, dv), lambda b, h, m: (b, h))]
    args = [qk, qk, v]
    if rope:
        for tab in rope_tabs:
            in_specs.append(pl.BlockSpec(tab.shape, lambda b, h, m, nd=tab.ndim: (0,) * nd))
            args.append(tab)
    if has_state:
        in_specs.append(pl.BlockSpec((None, 2, None, dk, dv), lambda b, h, m: (b, 0, h, 0, 0)))
        args.append(state)
    out_specs = [pl.BlockSpec((n_tok, dv), lambda b, h, m: (b, h))]
    out_shape = [jax.ShapeDtypeStruct((n_seq * n_tok, HEADS * dv), F32)]
    if emit_state:
        out_specs.append(pl.BlockSpec((None, 2, None, dk, dv), lambda b, h, m: (b, 0, h, 0, 0)))
        out_shape.append(jax.ShapeDtypeStruct((n_seq, 2, HEADS, dk, dv), F32))
    grid_spec = pltpu.PrefetchScalarGridSpec(
        num_scalar_prefetch=1, grid=(n_seq, HEADS), in_specs=in_specs, out_specs=out_specs,
        scratch_shapes=[pltpu.VMEM((2, dk, dv), F32)])
    return pl.pallas_call(
        kern, grid_spec=grid_spec, out_shape=out_shape,
        compiler_params=_cparams(("arbitrary", "arbitrary"), 56),
        name="ret_scan",
    )(decay_flat, *args)


def _rope_tables(n_tok, dk):
    r = dk // 4
    inv = 1.0 / (ROPE_BASE ** (jnp.arange(r, dtype=F32) / r))
    sign = jnp.concatenate([-jnp.ones((r,), F32), jnp.ones((r,), F32)])
    rows = jnp.arange(n_tok // GRID_W, dtype=F32)[:, None] * inv
    cols = (jnp.arange(CHUNK) % GRID_W).astype(F32)[:, None] * inv
    two = lambda a: jnp.concatenate([a, a], axis=-1)
    return (two(jnp.cos(rows))[:, None, :], (two(jnp.sin(rows)) * sign)[:, None, :],
            two(jnp.cos(cols)), two(jnp.sin(cols)) * sign)


def _make_mixer_out_kernel(dv, sigmoid_gate, with_router):
    def kernel(*refs):
        it = iter(refs)
        hs_ref, gate_ref, hg_ref, w_ref, x_ref, g1_ref, ga_ref = (next(it) for _ in range(7))
        if with_router:
            g2_ref, fsh_ref, fsc_ref, wrh_ref, wrl_ref = (next(it) for _ in range(5))
        xo_ref = next(it)
        if with_router:
            h2_ref, lg_ref = next(it), next(it)
        z_ref = next(it)
        for hh in range(HEADS):
            sl = slice(hh * dv, (hh + 1) * dv)
            seg = hs_ref[:, sl]
            y = seg * lax.rsqrt(jnp.mean(seg * seg, -1, keepdims=True) + EPS) * hg_ref[:, sl]
            gt = gate_ref[:, sl]
            act = jax.nn.sigmoid(gt) if sigmoid_gate else _silu(gt)
            z_ref[:, sl] = (act * y).astype(BF16)
        y = _dot(z_ref[...], w_ref[...])
        xn = x_ref[...] + ga_ref[...] * _rms(y, g1_ref[...])
        xo_ref[...] = xn
        if with_router:
            h2 = _adaln(xn, g2_ref[...], fsh_ref[...], fsc_ref[...])
            _store_row_tiles(h2_ref, h2)
            hb, hl = _split_bf16(h2)
            lg_ref[...] = _dot(hb, wrh_ref[...]) + _dot(hl, wrh_ref[...]) + _dot(hb, wrl_ref[...])
    return kernel


def _mixer_out(hs, gate, gate_blk, head_g, w_out, x, mods6, layer, first_row, rows_per_batch, g1, sigmoid_gate,
               router=None):
    t, d = x.shape
    vdim = hs.shape[1]
    tm = ROW_TILE
    ms = lambda comp: _mod_spec(d, layer, comp, rows_per_batch, first_row)
    with_router = router is not None
    in_specs = [pl.BlockSpec((tm, vdim), lambda i: (i, 0)),
                pl.BlockSpec((tm, vdim), lambda i: (i, gate_blk)),
                _const_spec((1, vdim)), _const_spec(w_out.shape),
                pl.BlockSpec((tm, d), lambda i: (i, 0)), _const_spec((1, d)), ms(2)]
    args = [hs, gate, head_g, w_out, x, g1, mods6]
    out_specs = [pl.BlockSpec((tm, d), lambda i: (i, 0))]
    out_shape = [jax.ShapeDtypeStruct((t, d), F32)]
    if with_router:
        g2, wrh, wrl = router
        in_specs += [_const_spec((1, d)), ms(3), ms(4), _const_spec(wrh.shape), _const_spec(wrl.shape)]
        args += [g2, mods6, mods6, wrh, wrl]
        assert d == SUBLANES * LANES
        out_specs += [pl.BlockSpec((tm * SUBLANES, LANES), lambda i: (i, 0)),
                      pl.BlockSpec((tm, LANES), lambda i: (i, 0))]
        out_shape += [jax.ShapeDtypeStruct((t * SUBLANES, LANES), F32), jax.ShapeDtypeStruct((t, LANES), F32)]
    return pl.pallas_call(
        _make_mixer_out_kernel(vdim // HEADS, sigmoid_gate, with_router),
        grid=(t // tm,), in_specs=in_specs, out_specs=out_specs, out_shape=out_shape,
        scratch_shapes=[pltpu.VMEM((tm, vdim), BF16)],
        compiler_params=_cparams(("arbitrary",), 40),
        name="mixer_out",
    )(*args)


def _ffn_kernel(x_ref, g2_ref, fsh_ref, fsc_ref, wg_ref, wu_ref, wd_ref, g3_ref, fga_ref, o_ref):
    x = x_ref[...]
    hb = _adaln(x, g2_ref[...], fsh_ref[...], fsc_ref[...]).astype(BF16)
    a = (_silu(_dot(hb, wg_ref[...])) * _dot(hb, wu_ref[...])).astype(BF16)
    y = _dot(a, wd_ref[...])
    o_ref[...] = x + fga_ref[...] * _rms(y, g3_ref[...])


def _ffn(x, mods6, layer, first_row, rows_per_batch, g2, g3, wg, wu, wd):
    t, d = x.shape
    tm = ROW_TILE
    ms = lambda comp: _mod_spec(d, layer, comp, rows_per_batch, first_row)
    return pl.pallas_call(
        _ffn_kernel,
        grid=(t // tm,),
        in_specs=[pl.BlockSpec((tm, d), lambda i: (i, 0)), _const_spec((1, d)), ms(3), ms(4),
                  _const_spec(wg.shape), _const_spec(wu.shape), _const_spec(wd.shape),
                  _const_spec((1, d)), ms(5)],
        out_specs=pl.BlockSpec((tm, d), lambda i: (i, 0)),
        out_shape=jax.ShapeDtypeStruct((t, d), F32),
        compiler_params=_cparams(("arbitrary",), 48),
        name="ffn_dense",
    )(x, g2, mods6, mods6, wg, wu, wd, g3, mods6)


def _router_kernel(lg_ref, e_ref, r_ref, gt_ref, cnt_ref, carry):
    tb = lg_ref.shape[0]

    @pl.when(pl.program_id(0) == 0)
    def _():
        carry[...] = jnp.zeros_like(carry)

    lane = lax.broadcasted_iota(jnp.int32, (tb, LANES), 1)
    lg = jnp.where(lane < N_EXPERTS, lg_ref[...], -jnp.inf)
    v1 = jnp.max(lg, axis=1, keepdims=True)
    i1 = jnp.min(jnp.where(lg == v1, lane, LANES), axis=1, keepdims=True)
    lg2 = jnp.where(lane == i1, -jnp.inf, lg)
    v2 = jnp.max(lg2, axis=1, keepdims=True)
    i2 = jnp.min(jnp.where(lg2 == v2, lane, LANES), axis=1, keepdims=True)
    ex = jnp.exp(v2 - v1)
    den = 1.0 + ex
    g1 = 1.0 / den
    g2 = ex / den
    oh1 = lane == i1
    oh2 = lane == i2
    onehot = jnp.where(oh1 | oh2, 1.0, 0.0)
    ri = lax.broadcasted_iota(jnp.int32, (tb, tb), 0)
    ci = lax.broadcasted_iota(jnp.int32, (tb, tb), 1)
    before = jnp.where(ri > ci, 1.0, 0.0).astype(BF16)
    rank = _dot(before, onehot.astype(BF16)) + carry[...]
    r1 = jnp.sum(jnp.where(oh1, rank, 0.0), axis=1, keepdims=True)
    r2 = jnp.sum(jnp.where(oh2, rank, 0.0), axis=1, keepdims=True)
    carry[...] = carry[...] + jnp.sum(onehot, axis=0, keepdims=True)
    e_ref[...] = jnp.where(lane == 0, i1, jnp.where(lane == 1, i2, 0))
    r_ref[...] = jnp.where(lane == 0, r1, jnp.where(lane == 1, r2, 0.0)).astype(jnp.int32)
    gt_ref[...] = jnp.where(lane == 0, g1, jnp.where(lane == 1, g2, 0.0))
    cnt_ref[...] = carry[...].astype(jnp.int32)


def _router(logits):
    t = logits.shape[0]
    tb = ROW_TILE
    blk = pl.BlockSpec((tb, LANES), lambda i: (i, 0))
    return pl.pallas_call(
        _router_kernel,
        grid=(t // tb,),
        in_specs=[blk],
        out_specs=[blk, blk, blk, pl.BlockSpec((1, LANES), lambda i: (0, 0))],
        out_shape=[jax.ShapeDtypeStruct((t, LANES), jnp.int32), jax.ShapeDtypeStruct((t, LANES), jnp.int32),
                   jax.ShapeDtypeStruct((t, LANES), F32), jax.ShapeDtypeStruct((1, LANES), jnp.int32)],
        scratch_shapes=[pltpu.VMEM((1, LANES), F32)],
        compiler_params=_cparams(("arbitrary",), 16),
        name="moe_router",
    )(logits)


def _make_dispatch_kernel(tb, n_blocks):
    def kernel(ps_ref, nb_ref, e_ref, r_ref, h_ref, xb_ref, zbuf, sem):
        i = pl.program_id(0)

        def row_copy(t, dest):
            src = pl.multiple_of(t * SUBLANES, SUBLANES)
            dst = pl.multiple_of(dest * SUBLANES, SUBLANES)
            return pltpu.make_async_copy(h_ref.at[pl.ds(src, SUBLANES), :], xb_ref.at[pl.ds(dst, SUBLANES), :],
                                         sem.at[0])

        @pl.when(i == 0)
        def _():
            zbuf[...] = jnp.zeros_like(zbuf)

            def zero_block(row0):
                dst = pl.multiple_of(row0 * SUBLANES, MOE_BLOCK * SUBLANES)
                cp = pltpu.make_async_copy(zbuf, xb_ref.at[pl.ds(dst, MOE_BLOCK * SUBLANES), :], sem.at[1])
                cp.start()
                cp.wait()

            for e in range(N_EXPERTS):
                @pl.when(nb_ref[e] > 0)
                def _():
                    zero_block(ps_ref[e] + (nb_ref[e] - 1) * MOE_BLOCK)

            used = ps_ref[N_EXPERTS - 1] // MOE_BLOCK + nb_ref[N_EXPERTS - 1]

            def tail(j, _):
                zero_block(j * MOE_BLOCK)
                return 0

            lax.fori_loop(used, n_blocks, tail, 0)

        base = i * tb

        def issue(t, _):
            for kk in range(TOP_K):
                a = (base + t) * TOP_K + kk
                row_copy(t, ps_ref[e_ref[a]] + r_ref[a]).start()
            return 0

        lax.fori_loop(0, tb, issue, 0)

        def drain(t, _):
            for kk in range(TOP_K):
                row_copy(0, 0).wait()
            return 0

        lax.fori_loop(0, tb, drain, 0)

    return kernel


def _dispatch(h2, pad_start, nblk, e_flat, r_flat, n_blocks):
    t = h2.shape[0] // SUBLANES
    tb = ROW_TILE
    grid_spec = pltpu.PrefetchScalarGridSpec(
        num_scalar_prefetch=4, grid=(t // tb,),
        in_specs=[pl.BlockSpec((tb * SUBLANES, LANES), lambda i, *_: (i, 0))],
        out_specs=pl.BlockSpec(memory_space=pl.ANY),
        scratch_shapes=[pltpu.VMEM((MOE_BLOCK * SUBLANES, LANES), F32), pltpu.SemaphoreType.DMA((2,))])
    return pl.pallas_call(
        _make_dispatch_kernel(tb, n_blocks), grid_spec=grid_spec,
        out_shape=jax.ShapeDtypeStruct((n_blocks * MOE_BLOCK * SUBLANES, LANES), F32),
        compiler_params=_cparams(("arbitrary",), 16),
        name="moe_dispatch",
    )(pad_start, nblk, e_flat, r_flat, h2)


def _expert_kernel(be_ref, nu_ref, xb_ref, wg_ref, wu_ref, wd_ref, yb_ref):
    i = pl.program_id(0)

    @pl.when(i < nu_ref[0])
    def _():
        xb = _load_row_tiles(xb_ref, MOE_BLOCK).astype(BF16)
        a = (_silu(_dot(xb, wg_ref[...])) * _dot(xb, wu_ref[...])).astype(BF16)
        _store_row_tiles(yb_ref, _dot(a, wd_ref[...]))

    @pl.when(i >= nu_ref[0])
    def _():
        yb_ref[...] = jnp.zeros_like(yb_ref)


def _experts(xb, block_e, n_used, wg, wu, wd):
    _, d, f = wg.shape
    n_blocks = xb.shape[0] // (MOE_BLOCK * SUBLANES)
    blk = pl.BlockSpec((MOE_BLOCK * SUBLANES, LANES), lambda i, be, nu: (i, 0))
    grid_spec = pltpu.PrefetchScalarGridSpec(
        num_scalar_prefetch=2, grid=(n_blocks,),
        in_specs=[blk,
                  pl.BlockSpec((None, d, f), lambda i, be, nu: (be[i], 0, 0)),
                  pl.BlockSpec((None, d, f), lambda i, be, nu: (be[i], 0, 0)),
                  pl.BlockSpec((None, f, d), lambda i, be, nu: (be[i], 0, 0))],
        out_specs=blk)
    return pl.pallas_call(
        _expert_kernel, grid_spec=grid_spec,
        out_shape=jax.ShapeDtypeStruct(xb.shape, F32),
        compiler_params=_cparams(("arbitrary",), 56),
        name="moe_experts",
    )(block_e, n_used, xb, wg, wu, wd)


def _make_combine_kernel(tb):
    def kernel(ps_ref, e_ref, r_ref, yb_ref, gt_ref, x_ref, g3_ref, fga_ref, o_ref, buf, sem):
        base = pl.program_id(0) * tb

        def row_copy(t, kk, slot):
            src = pl.multiple_of(slot * SUBLANES, SUBLANES)
            dst = pl.multiple_of(t * SUBLANES, SUBLANES)
            return pltpu.make_async_copy(yb_ref.at[pl.ds(src, SUBLANES), :], buf.at[kk, pl.ds(dst, SUBLANES), :],
                                         sem.at[0])

        def issue(t, _):
            for kk in range(TOP_K):
                a = (base + t) * TOP_K + kk
                row_copy(t, kk, ps_ref[e_ref[a]] + r_ref[a]).start()
            return 0

        lax.fori_loop(0, tb, issue, 0)

        def drain(t, _):
            for kk in range(TOP_K):
                row_copy(0, kk, 0).wait()
            return 0

        lax.fori_loop(0, tb, drain, 0)
        gt = gt_ref[...]
        f = gt[:, 0:1] * _load_row_tiles(buf.at[0], tb) + gt[:, 1:2] * _load_row_tiles(buf.at[1], tb)
        o_ref[...] = x_ref[...] + fga_ref[...] * _rms(f, g3_ref[...])

    return kernel


def _combine(yb, pad_start, e_flat, r_flat, gates, x, mods6, layer, first_row, rows_per_batch, g3):
    t, d = x.shape
    tb = ROW_TILE
    if rows_per_batch is None:
        fga_map = lambda i, *_: (layer, first_row, 5, 0, 0)
    else:
        fga_map = lambda i, *_: (layer, first_row + i // rows_per_batch, 5, 0, 0)
    grid_spec = pltpu.PrefetchScalarGridSpec(
        num_scalar_prefetch=3, grid=(t // tb,),
        in_specs=[pl.BlockSpec(memory_space=pl.ANY),
                  pl.BlockSpec((tb, LANES), lambda i, *_: (i, 0)),
                  pl.BlockSpec((tb, d), lambda i, *_: (i, 0)),
                  pl.BlockSpec((1, d), lambda i, *_: (0, 0)),
                  pl.BlockSpec((None, None, None, 1, d), fga_map)],
        out_specs=pl.BlockSpec((tb, d), lambda i, *_: (i, 0)),
        scratch_shapes=[pltpu.VMEM((TOP_K, tb * SUBLANES, LANES), F32), pltpu.SemaphoreType.DMA((1,))])
    return pl.pallas_call(
        _make_combine_kernel(tb), grid_spec=grid_spec,
        out_shape=jax.ShapeDtypeStruct((t, d), F32),
        compiler_params=_cparams(("arbitrary",), 16),
        name="moe_combine",
    )(pad_start, e_flat, r_flat, yb, gates, x, g3, mods6)


def _moe(h2, logits, x, mods6, layer, first_row, rows_per_batch, g3, wg, wu, wd):
    t = x.shape[0]
    n_blocks = -(-(t * TOP_K) // MOE_BLOCK) + N_EXPERTS
    e_pad, r_pad, gates, cnt = _router(logits)
    counts = cnt[0, :N_EXPERTS]
    nblk = (counts + MOE_BLOCK - 1) // MOE_BLOCK
    blk_end = jnp.cumsum(nblk)
    pad_start = ((blk_end - nblk) * MOE_BLOCK).astype(jnp.int32)
    n_used = blk_end[-1:].astype(jnp.int32)
    blk = jnp.minimum(jnp.arange(n_blocks, dtype=jnp.int32), n_used[0] - 1)
    block_e = jnp.minimum(jnp.sum(blk[:, None] >= blk_end[None, :], axis=1), N_EXPERTS - 1).astype(jnp.int32)
    e_flat = e_pad[:, :TOP_K].reshape(-1)
    r_flat = r_pad[:, :TOP_K].reshape(-1)
    xb = _dispatch(h2, pad_start, nblk.astype(jnp.int32), e_flat, r_flat, n_blocks)
    yb = _experts(xb, block_e, n_used, wg, wu, wd)
    return _combine(yb, pad_start, e_flat, r_flat, gates, x, mods6, layer, first_row, rows_per_batch, g3)


def _pad_cols(w, n):
    return jnp.pad(w, ((0, 0), (0, n - w.shape[1])))


def kernel(x_prompt, x_sample, state_mlstm_C, state_mlstm_n, state_mlstm_m, state_ret_S, c, c_ctx, mod_w, mod_b, norm_g, mlstm_w_in, mlstm_gate_b, mlstm_conv_w, mlstm_conv_b, mlstm_head_g, mlstm_w_out, ret_w_in, ret_decay_logit, ret_head_g, ret_w_out, ffn_w_gate, ffn_w_up, ffn_w_down, moe_router, moe_w_gate, moe_w_up, moe_w_down):
    bp, n_p, d = x_prompt.shape
    bs, n_s, _ = x_sample.shape
    depth = mod_w.shape[0]
    assert depth == 2 and CHUNK % GRID_W == 0 and n_s % GRID_W == 0

    cond = jnp.zeros((MOD_ROWS, d), F32).at[0].set(c_ctx).at[1:1 + bs].set(c)
    mods6 = _modulation(cond, mod_w, mod_b).reshape(depth, MOD_ROWS, N_MOD, 1, d)

    groups = [dict(x=x_prompt.reshape(bp * n_p, d), first=0, rpb=None, nseq=bp, ntok=n_p, prompt=True),
              dict(x=x_sample.reshape(bs * n_s, d), first=1, rpb=n_s // ROW_TILE, nseq=bs, ntok=n_s, prompt=False)]

    j = 0
    ml_qk = (mlstm_w_in.shape[2] - 4 * HEADS) // 2
    ml_v = ml_qk // 2
    w_in = mlstm_w_in[j]
    wqk = w_in[:, :ml_qk].astype(BF16)
    wv = w_in[:, ml_qk:ml_qk + ml_v].astype(BF16)
    wo = w_in[:, ml_qk + ml_v:ml_qk + 2 * ml_v].astype(BF16)
    w_gate = w_in[:, ml_qk + 2 * ml_v:]
    wgh, wgl = _split_bf16(_pad_cols(w_gate, LANES))
    wgth, wgtl = _split_bf16(w_gate.T)
    bcol = _pad_cols(mlstm_gate_b[j][None, :], LANES)
    brow = mlstm_gate_b[j][:, None]
    g = norm_g[0]
    w_out0 = mlstm_w_out[j].astype(BF16)
    fwg, fwu, fwd = ffn_w_gate[j].astype(BF16), ffn_w_up[j].astype(BF16), ffn_w_down[j].astype(BF16)
    new_c = new_n = new_m = None
    for grp in groups:
        args = (mods6, 0, grp["first"], grp["rpb"])
        qk, v, o, gcol, grow = _proj_mlstm(grp["x"], *args, g[0:1], wqk, wv, wo, wgh, wgl, wgth, wgtl, bcol, brow)
        grow3 = grow.reshape(grow.shape[0], 1, grow.shape[1])
        if grp["prompt"]:
            m0 = jnp.zeros((grp["nseq"] * 2 * HEADS,), F32)
            hs, new_c, new_n, new_m = _mlstm_scan(qk, v, gcol, grow3, mlstm_conv_w[j], mlstm_conv_b[j][None, :], m0,
                                                  None, grp["nseq"], grp["ntok"], True)
        else:
            (hs,) = _mlstm_scan(qk, v, gcol, grow3, mlstm_conv_w[j], mlstm_conv_b[j][None, :],
                                state_mlstm_m[:, j].reshape(-1),
                                (state_mlstm_C[:, j], state_mlstm_n[:, j]), grp["nseq"], grp["ntok"], False)
        (x1,) = _mixer_out(hs, o, 0, mlstm_head_g[j][None, :], w_out0, grp["x"], *args, g[1:2], True)
        grp["x"] = _ffn(x1, *args, g[2:3], g[3:4], fwg, fwu, fwd)

    ret_qk = ret_w_in.shape[2] // 3
    w_in = ret_w_in[j]
    rwqk = w_in[:, :ret_qk].astype(BF16)
    rwv = w_in[:, ret_qk:2 * ret_qk].astype(BF16)
    rwg = w_in[:, 2 * ret_qk:].astype(BF16)
    g = norm_g[1]
    w_out1 = ret_w_out[j].astype(BF16)
    wrh, wrl = _split_bf16(_pad_cols(moe_router[j], LANES))
    ewg, ewu, ewd = moe_w_gate[j].astype(BF16), moe_w_up[j].astype(BF16), moe_w_down[j].astype(BF16)
    decay_flat = ret_decay_logit[j].reshape(-1)
    new_s = None
    for grp in groups:
        args = (mods6, 1, grp["first"], grp["rpb"])
        qk, v, gate = _proj_ret(grp["x"], *args, g[0:1], rwqk, rwv, rwg)
        if grp["prompt"]:
            hs, new_s = _ret_scan(qk, v, decay_flat, None, None, grp["nseq"], grp["ntok"], True)
        else:
            (hs,) = _ret_scan(qk, v, decay_flat, _rope_tables(grp["ntok"], ret_qk // (2 * HEADS)), state_ret_S[:, j],
                              grp["nseq"], grp["ntok"], False)
        x1, h2, logits = _mixer_out(hs, gate, 0, ret_head_g[j][None, :], w_out1, grp["x"], *args, g[1:2], False,
                                    router=(g[2:3], wrh, wrl))
        grp["x"] = _moe(h2, logits, x1, *args, g[3:4], ewg, ewu, ewd)

    y_prompt = groups[0]["x"].reshape(bp, n_p, d)
    y_sample = groups[1]["x"].reshape(bs, n_s, d)
    return (y_prompt, y_sample, new_c[:, None], new_n[:, None], new_m[:, None, :, :, 0], new_s[:, None])
```

```python
import functools
import math

import jax
import jax.numpy as jnp
from jax import lax
from jax.experimental import pallas as pl
from jax.experimental.pallas import tpu as pltpu

F32 = jnp.float32
BF16 = jnp.bfloat16

EPS = 1e-6
N_MOD = 6
HEADS = 4
CHUNK = 128
GRID_W = 64
ROPE_BASE = 10000.0
N_EXPERTS = 8
TOP_K = 2
MOE_BLOCK = 256
LANES = 128
SUBLANES = 8
ROW_TILE = 256
MOD_ROWS = 8
STAGE_ROWS = -(-(TOP_K * ROW_TILE + N_EXPERTS * (SUBLANES - 1)) // LANES) * LANES
MIB = 1024 * 1024


def _cparams(sem, vmem_mib):
    return pltpu.CompilerParams(dimension_semantics=sem, vmem_limit_bytes=vmem_mib * MIB)


def _dot(a, b):
    return jnp.dot(a, b, preferred_element_type=F32)


def _dot_nt(a, b):
    return lax.dot_general(a, b, (((1,), (1,)), ((), ())), preferred_element_type=F32)


def _dot_tn(a, b):
    return lax.dot_general(a, b, (((0,), (0,)), ((), ())), preferred_element_type=F32)


def _split_bf16(x):
    hi = x.astype(BF16)
    lo = (x - hi.astype(F32)).astype(BF16)
    return hi, lo


def _rms(x, g):
    return x * lax.rsqrt(jnp.mean(x * x, -1, keepdims=True) + EPS) * g


def _adaln(x, g, shift, scale):
    return _rms(x, g) * (1.0 + scale) + shift


def _silu(x):
    return x * jax.nn.sigmoid(x)


def _logsig(x):
    return jnp.minimum(x, 0.0) - jnp.log1p(jnp.exp(-jnp.abs(x)))


def _const_spec(shape):
    nd = len(shape)
    return pl.BlockSpec(shape, lambda *_: (0,) * nd, pipeline_mode=pl.Buffered(1))


def _mod_spec(d, layer, comp, rows_per_batch, first_row):
    if rows_per_batch is None:
        return pl.BlockSpec((None, None, None, 1, d), lambda i: (layer, first_row, comp, 0, 0))
    return pl.BlockSpec((None, None, None, 1, d), lambda i: (layer, first_row + i // rows_per_batch, comp, 0, 0))


def _mod_kernel(c_ref, w_ref, b_ref, o_ref):
    s = _silu(c_ref[...]).astype(BF16)
    o_ref[...] = _dot(s, w_ref[...].astype(BF16)) + b_ref[...]


def _modulation(cond, mod_w, mod_b):
    depth, d, n = mod_w.shape
    tn = n // 4
    return pl.pallas_call(
        _mod_kernel,
        grid=(depth, n // tn),
        in_specs=[pl.BlockSpec((MOD_ROWS, d), lambda l, j: (0, 0)),
                  pl.BlockSpec((None, d, tn), lambda l, j: (l, 0, j)),
                  pl.BlockSpec((None, 1, tn), lambda l, j: (l, 0, j))],
        out_specs=pl.BlockSpec((None, MOD_ROWS, tn), lambda l, j: (l, 0, j)),
        out_shape=jax.ShapeDtypeStruct((depth, MOD_ROWS, n), F32),
        compiler_params=_cparams(("arbitrary", "arbitrary"), 40),
        name="modulation",
    )(cond, mod_w, mod_b.reshape(depth, 1, n))


def _proj_mlstm_kernel(x_ref, g_ref, sh_ref, sc_ref, wqk_ref, wv_ref, wo_ref, wgh_ref, wgl_ref,
                       wgth_ref, wgtl_ref, bcol_ref, brow_ref,
                       qk_ref, v_ref, o_ref, gcol_ref, grow_ref):
    h = _adaln(x_ref[...], g_ref[...], sh_ref[...], sc_ref[...])
    hb, hl = _split_bf16(h)
    qk_ref[...] = _dot(hb, wqk_ref[...])
    v_ref[...] = _dot(hb, wv_ref[...]).astype(BF16)
    o_ref[...] = _dot(hb, wo_ref[...])
    gc = _dot(hb, wgh_ref[...]) + _dot(hl, wgh_ref[...]) + _dot(hb, wgl_ref[...]) + bcol_ref[...]
    for hh in range(HEADS):
        gcol_ref[hh] = gc if hh == 0 else pltpu.roll(gc, LANES - hh, axis=1)
    grow_ref[...] = (_dot_nt(wgth_ref[...], hb) + _dot_nt(wgth_ref[...], hl) + _dot_nt(wgtl_ref[...], hb)
                     + brow_ref[...])


def _proj_mlstm(x, mods6, layer, first_row, rows_per_batch, g, wqk, wv, wo, wgh, wgl, wgth, wgtl, bcol, brow):
    t, d = x.shape
    tm = ROW_TILE
    ms = lambda comp: _mod_spec(d, layer, comp, rows_per_batch, first_row)
    ng = wgth.shape[0]
    return pl.pallas_call(
        _proj_mlstm_kernel,
        grid=(t // tm,),
        in_specs=[pl.BlockSpec((tm, d), lambda i: (i, 0)),
                  _const_spec((1, d)), ms(0), ms(1),
                  _const_spec(wqk.shape), _const_spec(wv.shape), _const_spec(wo.shape),
                  _const_spec(wgh.shape), _const_spec(wgl.shape), _const_spec(wgth.shape), _const_spec(wgtl.shape),
                  _const_spec(bcol.shape), _const_spec(brow.shape)],
        out_specs=[pl.BlockSpec((tm, wqk.shape[1]), lambda i: (i, 0)),
                   pl.BlockSpec((tm, wv.shape[1]), lambda i: (i, 0)),
                   pl.BlockSpec((tm, wo.shape[1]), lambda i: (i, 0)),
                   pl.BlockSpec((HEADS, tm, LANES), lambda i: (0, i, 0)),
                   pl.BlockSpec((ng, tm), lambda i: (0, i))],
        out_shape=[jax.ShapeDtypeStruct((t, wqk.shape[1]), F32),
                   jax.ShapeDtypeStruct((t, wv.shape[1]), BF16),
                   jax.ShapeDtypeStruct((t, wo.shape[1]), F32),
                   jax.ShapeDtypeStruct((HEADS, t, LANES), F32),
                   jax.ShapeDtypeStruct((ng, t), F32)],
        compiler_params=_cparams(("arbitrary",), 48),
        name="proj_mlstm",
    )(x, g, mods6, mods6, wqk, wv, wo, wgh, wgl, wgth, wgtl, bcol, brow)


def _make_mlstm_scan_kernel(n_tok, dk, dv, has_state, emit_state):
    L = CHUNK
    nc = n_tok // L
    assert nc % 2 == 0

    def kernel(*refs):
        it = iter(refs)
        m0_ref = next(it)
        q_ref, k_ref, v_ref, gcol_ref = next(it), next(it), next(it), next(it)
        gi0, gf0, gi1, gf1 = next(it), next(it), next(it), next(it)
        gi_refs, gf_refs = (gi0, gi1), (gf0, gf1)
        cwq_ref, cwk_ref, cbq_ref, cbk_ref = next(it), next(it), next(it), next(it)
        if has_state:
            c0_ref, n0_ref = next(it), next(it)
        out_ref = next(it)
        if emit_state:
            cout_ref, nout_ref, mout_ref = next(it), next(it), next(it)
        qs, ks, cst = next(it), next(it), next(it)

        b = pl.program_id(0)
        hh = pl.program_id(1)

        rowi = lax.broadcasted_iota(jnp.int32, (L, dk), 0)

        def conv_chunk(c, _):
            r0 = pl.multiple_of(c * L, L)
            for src, cw_ref, cb_ref, dst, scale in ((q_ref, cwq_ref, cbq_ref, qs, dk ** -0.5),
                                                    (k_ref, cwk_ref, cbk_ref, ks, 1.0)):
                xc = src[pl.ds(r0, L), :]
                prev_row = src[pl.ds(jnp.maximum(r0 - 1, 0), 1), :] * jnp.where(c > 0, 1.0, 0.0)
                next_row = src[pl.ds(jnp.minimum(r0 + L, n_tok - 1), 1), :] * jnp.where(c < nc - 1, 1.0, 0.0)
                prev = jnp.where(rowi == 0, prev_row, pltpu.roll(xc, 1, axis=0))
                nxt = jnp.where(rowi == L - 1, next_row, pltpu.roll(xc, L - 1, axis=0))
                y = prev * cw_ref[0:1, :] + xc * cw_ref[1:2, :] + nxt * cw_ref[2:3, :] + cb_ref[...]
                dst[pl.ds(r0, L), :] = (y * scale).astype(BF16)
            return 0

        lax.fori_loop(0, nc, conv_chunk, 0)

        if has_state:
            cst[...] = c0_ref[...]
            n_init = tuple(n0_ref[d, pl.ds(hh, 1), :] for d in range(2))
        else:
            cst[...] = jnp.zeros_like(cst)
            n_init = tuple(jnp.zeros((1, dk), F32) for _ in range(2))
        m_init = tuple(jnp.full((1, 1), m0_ref[(b * 2 + d) * HEADS + hh], F32) for d in range(2))

        ri = lax.broadcasted_iota(jnp.int32, (L, L), 0)
        ci = lax.broadcasted_iota(jnp.int32, (L, L), 1)
        lower = ri >= ci
        upper = ri <= ci
        masks = ((lower, upper), (upper, lower))

        def chunk(d, c, n, m):
            mk, mkt = masks[d]
            r0 = pl.multiple_of(c * L, L)
            q = qs[pl.ds(r0, L), :]
            k = ks[pl.ds(r0, L), :]
            v = v_ref[pl.ds(r0, L), :]
            gc = gcol_ref[pl.ds(r0, L), :]
            i_col = gc[:, 8 * d:8 * d + 1]
            f_col = _logsig(gc[:, 8 * d + 4:8 * d + 5])
            i_row = gi_refs[d][:, pl.ds(r0, L)]
            f_row = _logsig(gf_refs[d][:, pl.ds(r0, L)])
            b_col = jnp.sum(jnp.where(mk, f_row, 0.0), axis=1, keepdims=True)
            b_row = jnp.sum(jnp.where(mkt, f_col, 0.0), axis=0, keepdims=True)
            dlog = jnp.where(mk, b_col - b_row + i_row, -jnp.inf)
            a = b_col + m
            m_t = jnp.maximum(a, jnp.max(dlog, axis=1, keepdims=True))
            w_inter = jnp.exp(a - m_t)
            s = _dot_nt(q, k) * jnp.exp(dlog - m_t)
            c_old = cst[d]
            num = w_inter * _dot(q, c_old.astype(BF16)) + _dot(s.astype(BF16), v)
            den = w_inter * jnp.sum(q.astype(F32) * n, axis=1, keepdims=True) + jnp.sum(s, axis=1, keepdims=True)
            h = num * (1.0 / jnp.maximum(jnp.abs(den), jnp.exp(-m_t)))
            b_end = jnp.sum(f_row, axis=1, keepdims=True)
            g = b_end - b_col + i_col
            m_new = jnp.maximum(b_end + m, jnp.max(g, axis=0, keepdims=True))
            w_c = jnp.exp(g - m_new)
            decay = jnp.exp(b_end + m - m_new)
            kw = k.astype(F32) * w_c
            cst[d] = decay * c_old + _dot_tn(kw.astype(BF16), v)
            n_new = decay * n + jnp.sum(kw, axis=0, keepdims=True)
            return h, n_new, m_new

        def make_body(accumulate):
            def body(i, carry):
                nf, mf, nb, mb = carry
                hf, nf, mf = chunk(0, i, nf, mf)
                rf = pl.multiple_of(i * L, L)
                cb = nc - 1 - i
                hb, nb, mb = chunk(1, cb, nb, mb)
                rb = pl.multiple_of(cb * L, L)
                if accumulate:
                    out_ref[pl.ds(rf, L), :] += hf
                    out_ref[pl.ds(rb, L), :] += hb
                else:
                    out_ref[pl.ds(rf, L), :] = hf
                    out_ref[pl.ds(rb, L), :] = hb
                return nf, mf, nb, mb
            return body

        carry = (n_init[0], m_init[0], n_init[1], m_init[1])
        carry = lax.fori_loop(0, nc // 2, make_body(False), carry)
        carry = lax.fori_loop(nc // 2, nc, make_body(True), carry)

        if emit_state:
            cout_ref[...] = cst[...]
            for d in range(2):
                nout_ref[d, pl.ds(hh, 1), :] = carry[2 * d]
                mout_ref[d, pl.ds(hh, 1), :] = jnp.broadcast_to(carry[2 * d + 1], (1, LANES))

    return kernel


def _mlstm_scan(qk, v, gcol, grow3, conv_w, conv_b, m0_flat, state, n_seq, n_tok, emit_state):
    dk = qk.shape[1] // (2 * HEADS)
    dv = v.shape[1] // HEADS
    has_state = state is not None
    kern = _make_mlstm_scan_kernel(n_tok, dk, dv, has_state, emit_state)
    grow_spec = lambda r: pl.BlockSpec((None, 1, n_tok), lambda b, h, m: (r * HEADS + h, 0, b))
    in_specs = [pl.BlockSpec((n_tok, dk), lambda b, h, m: (b, h)),
                pl.BlockSpec((n_tok, dk), lambda b, h, m: (b, HEADS + h)),
                pl.BlockSpec((n_tok, dv), lambda b, h, m: (b, h)),
                pl.BlockSpec((None, n_tok, LANES), lambda b, h, m: (h, b, 0)),
                grow_spec(0), grow_spec(1), grow_spec(2), grow_spec(3),
                pl.BlockSpec((3, dk), lambda b, h, m: (0, h)),
                pl.BlockSpec((3, dk), lambda b, h, m: (0, HEADS + h)),
                pl.BlockSpec((1, dk), lambda b, h, m: (0, h)),
                pl.BlockSpec((1, dk), lambda b, h, m: (0, HEADS + h))]
    args = [qk, qk, v, gcol, grow3, grow3, grow3, grow3, conv_w, conv_w, conv_b, conv_b]
    if has_state:
        in_specs += [pl.BlockSpec((None, 2, None, dk, dv), lambda b, h, m: (b, 0, h, 0, 0)),
                     pl.BlockSpec((None, 2, HEADS, dk), lambda b, h, m: (b, 0, 0, 0))]
        args += list(state)
    out_specs = [pl.BlockSpec((n_tok, dv), lambda b, h, m: (b, h))]
    out_shape = [jax.ShapeDtypeStruct((n_seq * n_tok, HEADS * dv), F32)]
    if emit_state:
        out_specs += [pl.BlockSpec((None, 2, None, dk, dv), lambda b, h, m: (b, 0, h, 0, 0)),
                      pl.BlockSpec((None, 2, HEADS, dk), lambda b, h, m: (b, 0, 0, 0)),
                      pl.BlockSpec((None, 2, HEADS, LANES), lambda b, h, m: (b, 0, 0, 0))]
        out_shape += [jax.ShapeDtypeStruct((n_seq, 2, HEADS, dk, dv), F32),
                      jax.ShapeDtypeStruct((n_seq, 2, HEADS, dk), F32),
                      jax.ShapeDtypeStruct((n_seq, 2, HEADS, LANES), F32)]
    grid_spec = pltpu.PrefetchScalarGridSpec(
        num_scalar_prefetch=1, grid=(n_seq, HEADS), in_specs=in_specs, out_specs=out_specs,
        scratch_shapes=[pltpu.VMEM((n_tok, dk), BF16), pltpu.VMEM((n_tok, dk), BF16),
                        pltpu.VMEM((2, dk, dv), F32)])
    return pl.pallas_call(
        kern, grid_spec=grid_spec, out_shape=out_shape,
        compiler_params=_cparams(("arbitrary", "arbitrary"), 56),
        name="mlstm_scan",
    )(m0_flat, *args)


def _proj_ret_kernel(x_ref, g_ref, sh_ref, sc_ref, wqk_ref, wv_ref, wg_ref, qk_ref, v_ref, gate_ref):
    hb = _adaln(x_ref[...], g_ref[...], sh_ref[...], sc_ref[...]).astype(BF16)
    qk_ref[...] = _dot(hb, wqk_ref[...])
    v_ref[...] = _dot(hb, wv_ref[...]).astype(BF16)
    gate_ref[...] = _dot(hb, wg_ref[...])


def _proj_ret(x, mods6, layer, first_row, rows_per_batch, g, wqk, wv, wg):
    t, d = x.shape
    tm = ROW_TILE
    ms = lambda comp: _mod_spec(d, layer, comp, rows_per_batch, first_row)
    return pl.pallas_call(
        _proj_ret_kernel,
        grid=(t // tm,),
        in_specs=[pl.BlockSpec((tm, d), lambda i: (i, 0)), _const_spec((1, d)), ms(0), ms(1),
                  _const_spec(wqk.shape), _const_spec(wv.shape), _const_spec(wg.shape)],
        out_specs=[pl.BlockSpec((tm, wqk.shape[1]), lambda i: (i, 0)),
                   pl.BlockSpec((tm, wv.shape[1]), lambda i: (i, 0)),
                   pl.BlockSpec((tm, wg.shape[1]), lambda i: (i, 0))],
        out_shape=[jax.ShapeDtypeStruct((t, wqk.shape[1]), F32),
                   jax.ShapeDtypeStruct((t, wv.shape[1]), BF16),
                   jax.ShapeDtypeStruct((t, wg.shape[1]), F32)],
        compiler_params=_cparams(("arbitrary",), 48),
        name="proj_ret",
    )(x, g, mods6, mods6, wqk, wv, wg)


def _make_ret_scan_kernel(n_tok, dk, dv, has_state, emit_state, rope):
    L = CHUNK
    nc = n_tok // L
    assert nc % 2 == 0
    rows_per_chunk = L // GRID_W

    def kernel(*refs):
        it = iter(refs)
        dl_ref = next(it)
        q_ref, k_ref, v_ref = next(it), next(it), next(it)
        if rope:
            rcos_ref, rsin_ref, ccos_ref, csin_ref = next(it), next(it), next(it), next(it)
        if has_state:
            s0_ref = next(it)
        out_ref = next(it)
        if emit_state:
            sout_ref = next(it)
        sst = next(it)

        hh = pl.program_id(1)
        ri = lax.broadcasted_iota(jnp.int32, (L, L), 0)
        ci = lax.broadcasted_iota(jnp.int32, (L, L), 1)
        pos = lax.broadcasted_iota(jnp.int32, (L, 1), 0).astype(F32)
        rel = (ri - ci).astype(F32)

        dmat, q_dec, k_dec, c_dec = [], [], [], []
        for d in range(2):
            lg = _logsig(jnp.full((1, 1), dl_ref[d * HEADS + hh], F32))
            if d == 0:
                dmat.append(jnp.where(ri >= ci, jnp.exp(lg * jnp.maximum(rel, 0.0)), 0.0))
                q_dec.append(jnp.exp(lg * (pos + 1.0)))
                k_dec.append(jnp.exp(lg * (L - 1.0 - pos)))
            else:
                dmat.append(jnp.where(ri <= ci, jnp.exp(lg * jnp.maximum(-rel, 0.0)), 0.0))
                q_dec.append(jnp.exp(lg * (L - pos)))
                k_dec.append(jnp.exp(lg * pos))
            c_dec.append(jnp.exp(lg * float(L)))

        if has_state:
            sst[...] = s0_ref[...]
        else:
            sst[...] = jnp.zeros_like(sst)

        if rope:
            tok = lax.broadcasted_iota(jnp.int32, (L, LANES), 0)

        def rotate(x, c):
            if not rope:
                return x
            rc, rs = ccos_ref[...], csin_ref[...]
            row_c = rcos_ref[c * rows_per_chunk]
            row_s = rsin_ref[c * rows_per_chunk]
            for j in range(1, rows_per_chunk):
                sel = tok >= j * GRID_W
                row_c = jnp.where(sel, rcos_ref[c * rows_per_chunk + j], row_c)
                row_s = jnp.where(sel, rsin_ref[c * rows_per_chunk + j], row_s)
            xa, xb = x[:, :LANES], x[:, LANES:]
            ya = xa * row_c + pltpu.roll(xa, LANES // 2, axis=1) * row_s
            yb = xb * rc + pltpu.roll(xb, LANES // 2, axis=1) * rs
            return jnp.concatenate([ya, yb], axis=1)

        def chunk(d, c):
            r0 = pl.multiple_of(c * L, L)
            q = rotate(q_ref[pl.ds(r0, L), :], c)
            k = rotate(k_ref[pl.ds(r0, L), :], c) * dk ** -0.5
            v = v_ref[pl.ds(r0, L), :]
            qb = q.astype(BF16)
            kb = k.astype(BF16)
            s = _dot_nt(qb, kb) * dmat[d]
            s_old = sst[d]
            out = _dot(s.astype(BF16), v) + _dot((q * q_dec[d]).astype(BF16), s_old.astype(BF16))
            sst[d] = c_dec[d] * s_old + _dot_tn((k * k_dec[d]).astype(BF16), v)
            return out

        def body1(i, _):
            rf = pl.multiple_of(i * L, L)
            out_ref[pl.ds(rf, L), :] = chunk(0, i)
            cb = nc - 1 - i
            rb = pl.multiple_of(cb * L, L)
            out_ref[pl.ds(rb, L), :] = chunk(1, cb)
            return 0

        def body2(i, _):
            rf = pl.multiple_of(i * L, L)
            out_ref[pl.ds(rf, L), :] += chunk(0, i)
            cb = nc - 1 - i
            rb = pl.multiple_of(cb * L, L)
            out_ref[pl.ds(rb, L), :] += chunk(1, cb)
            return 0

        lax.fori_loop(0, nc // 2, body1, 0)
        lax.fori_loop(nc // 2, nc, body2, 0)
        if emit_state:
            sout_ref[...] = sst[...]

    return kernel


def _ret_scan(qk, v, decay_flat, rope_tabs, state, n_seq, n_tok, emit_state):
    dk = qk.shape[1] // (2 * HEADS)
    dv = v.shape[1] // HEADS
    has_state = state is not None
    rope = rope_tabs is not None
    kern = _make_ret_scan_kernel(n_tok, dk, dv, has_state, emit_state, rope)
    in_specs = [pl.BlockSpec((n_tok, dk), lambda b, h, m: (b, h)),
                pl.BlockSpec((n_tok, dk), lambda b, h, m: (b, HEADS + h)),
                pl.BlockSpec((n_tok, dv), lambda b, h, m: (b, h))]
    args = [qk, qk, v]
    if rope:
        for tab in rope_tabs:
            in_specs.append(pl.BlockSpec(tab.shape, lambda b, h, m, nd=tab.ndim: (0,) * nd))
            args.append(tab)
    if has_state:
        in_specs.append(pl.BlockSpec((None, 2, None, dk, dv), lambda b, h, m: (b, 0, h, 0, 0)))
        args.append(state)
    out_specs = [pl.BlockSpec((n_tok, dv), lambda b, h, m: (b, h))]
    out_shape = [jax.ShapeDtypeStruct((n_seq * n_tok, HEADS * dv), F32)]
    if emit_state:
        out_specs.append(pl.BlockSpec((None, 2, None, dk, dv), lambda b, h, m: (b, 0, h, 0, 0)))
        out_shape.append(jax.ShapeDtypeStruct((n_seq, 2, HEADS, dk, dv), F32))
    grid_spec = pltpu.PrefetchScalarGridSpec(
        num_scalar_prefetch=1, grid=(n_seq, HEADS), in_specs=in_specs, out_specs=out_specs,
        scratch_shapes=[pltpu.VMEM((2, dk, dv), F32)])
    return pl.pallas_call(
        kern, grid_spec=grid_spec, out_shape=out_shape,
        compiler_params=_cparams(("arbitrary", "arbitrary"), 56),
        name="ret_scan",
    )(decay_flat, *args)


def _rope_tables(n_tok, dk):
    r = dk // 4
    inv = 1.0 / (ROPE_BASE ** (jnp.arange(r, dtype=F32) / r))
    sign = jnp.concatenate([-jnp.ones((r,), F32), jnp.ones((r,), F32)])
    rows = jnp.arange(n_tok // GRID_W, dtype=F32)[:, None] * inv
    cols = (jnp.arange(CHUNK) % GRID_W).astype(F32)[:, None] * inv
    two = lambda a: jnp.concatenate([a, a], axis=-1)
    return (two(jnp.cos(rows))[:, None, :], (two(jnp.sin(rows)) * sign)[:, None, :],
            two(jnp.cos(cols)), two(jnp.sin(cols)) * sign)


def _make_mixer_out_kernel(dv, sigmoid_gate, with_router):
    def kernel(*refs):
        it = iter(refs)
        hs_ref, gate_ref, hg_ref, w_ref, x_ref, g1_ref, ga_ref = (next(it) for _ in range(7))
        if with_router:
            g2_ref, fsh_ref, fsc_ref, wrh_ref, wrl_ref = (next(it) for _ in range(5))
        xo_ref = next(it)
        if with_router:
            h2_ref, lg_ref = next(it), next(it)
        z_ref = next(it)
        for hh in range(HEADS):
            sl = slice(hh * dv, (hh + 1) * dv)
            seg = hs_ref[:, sl]
            y = seg * lax.rsqrt(jnp.mean(seg * seg, -1, keepdims=True) + EPS) * hg_ref[:, sl]
            gt = gate_ref[:, sl]
            act = jax.nn.sigmoid(gt) if sigmoid_gate else _silu(gt)
            z_ref[:, sl] = (act * y).astype(BF16)
        y = _dot(z_ref[...], w_ref[...])
        xn = x_ref[...] + ga_ref[...] * _rms(y, g1_ref[...])
        xo_ref[...] = xn
        if with_router:
            h2 = _adaln(xn, g2_ref[...], fsh_ref[...], fsc_ref[...])
            h2_ref[...] = h2
            hb, hl = _split_bf16(h2)
            lg_ref[...] = _dot(hb, wrh_ref[...]) + _dot(hl, wrh_ref[...]) + _dot(hb, wrl_ref[...])
    return kernel


def _mixer_out(hs, gate, gate_blk, head_g, w_out, x, mods6, layer, first_row, rows_per_batch, g1, sigmoid_gate,
               router=None):
    t, d = x.shape
    vdim = hs.shape[1]
    tm = ROW_TILE
    ms = lambda comp: _mod_spec(d, layer, comp, rows_per_batch, first_row)
    with_router = router is not None
    in_specs = [pl.BlockSpec((tm, vdim), lambda i: (i, 0)),
                pl.BlockSpec((tm, vdim), lambda i: (i, gate_blk)),
                _const_spec((1, vdim)), _const_spec(w_out.shape),
                pl.BlockSpec((tm, d), lambda i: (i, 0)), _const_spec((1, d)), ms(2)]
    args = [hs, gate, head_g, w_out, x, g1, mods6]
    out_specs = [pl.BlockSpec((tm, d), lambda i: (i, 0))]
    out_shape = [jax.ShapeDtypeStruct((t, d), F32)]
    if with_router:
        g2, wrh, wrl = router
        in_specs += [_const_spec((1, d)), ms(3), ms(4), _const_spec(wrh.shape), _const_spec(wrl.shape)]
        args += [g2, mods6, mods6, wrh, wrl]
        out_specs += [pl.BlockSpec((tm, d), lambda i: (i, 0)), pl.BlockSpec((tm, LANES), lambda i: (i, 0))]
        out_shape += [jax.ShapeDtypeStruct((t, d), F32), jax.ShapeDtypeStruct((t, LANES), F32)]
    return pl.pallas_call(
        _make_mixer_out_kernel(vdim // HEADS, sigmoid_gate, with_router),
        grid=(t // tm,), in_specs=in_specs, out_specs=out_specs, out_shape=out_shape,
        scratch_shapes=[pltpu.VMEM((tm, vdim), BF16)],
        compiler_params=_cparams(("arbitrary",), 40),
        name="mixer_out",
    )(*args)


def _ffn_kernel(x_ref, g2_ref, fsh_ref, fsc_ref, wg_ref, wu_ref, wd_ref, g3_ref, fga_ref, o_ref):
    x = x_ref[...]
    hb = _adaln(x, g2_ref[...], fsh_ref[...], fsc_ref[...]).astype(BF16)
    a = (_silu(_dot(hb, wg_ref[...])) * _dot(hb, wu_ref[...])).astype(BF16)
    y = _dot(a, wd_ref[...])
    o_ref[...] = x + fga_ref[...] * _rms(y, g3_ref[...])


def _ffn(x, mods6, layer, first_row, rows_per_batch, g2, g3, wg, wu, wd):
    t, d = x.shape
    tm = ROW_TILE
    ms = lambda comp: _mod_spec(d, layer, comp, rows_per_batch, first_row)
    return pl.pallas_call(
        _ffn_kernel,
        grid=(t // tm,),
        in_specs=[pl.BlockSpec((tm, d), lambda i: (i, 0)), _const_spec((1, d)), ms(3), ms(4),
                  _const_spec(wg.shape), _const_spec(wu.shape), _const_spec(wd.shape),
                  _const_spec((1, d)), ms(5)],
        out_specs=pl.BlockSpec((tm, d), lambda i: (i, 0)),
        out_shape=jax.ShapeDtypeStruct((t, d), F32),
        compiler_params=_cparams(("arbitrary",), 48),
        name="ffn_dense",
    )(x, g2, mods6, mods6, wg, wu, wd, g3, mods6)


def _router_kernel(lg_ref, lrow_ref, gt_ref, tab_ref, tot_ref, carry):
    tb = lg_ref.shape[0]

    @pl.when(pl.program_id(0) == 0)
    def _():
        carry[...] = jnp.zeros_like(carry)

    lane = lax.broadcasted_iota(jnp.int32, (tb, LANES), 1)
    lg = jnp.where(lane < N_EXPERTS, lg_ref[...], -jnp.inf)
    v1 = jnp.max(lg, axis=1, keepdims=True)
    i1 = jnp.min(jnp.where(lg == v1, lane, LANES), axis=1, keepdims=True)
    lg2 = jnp.where(lane == i1, -jnp.inf, lg)
    v2 = jnp.max(lg2, axis=1, keepdims=True)
    i2 = jnp.min(jnp.where(lg2 == v2, lane, LANES), axis=1, keepdims=True)
    ex = jnp.exp(v2 - v1)
    den = 1.0 + ex
    g1 = 1.0 / den
    g2 = ex / den
    oh1 = lane == i1
    oh2 = lane == i2
    onehot = jnp.where(oh1 | oh2, 1.0, 0.0)
    ri = lax.broadcasted_iota(jnp.int32, (tb, tb), 0)
    ci = lax.broadcasted_iota(jnp.int32, (tb, tb), 1)
    before = jnp.where(ri > ci, 1.0, 0.0).astype(BF16)
    rank = _dot(before, onehot.astype(BF16))
    tiles = jnp.ceil(jnp.sum(onehot, axis=0, keepdims=True) * (1.0 / SUBLANES))
    ei = lax.broadcasted_iota(jnp.int32, (LANES, LANES), 0)
    ej = lax.broadcasted_iota(jnp.int32, (LANES, LANES), 1)
    earlier = jnp.where(ei < ej, 1.0, 0.0).astype(BF16)
    off = _dot(jnp.broadcast_to(tiles, (SUBLANES, LANES)).astype(BF16), earlier)[0:1] * SUBLANES
    pos = rank + off
    r1 = jnp.sum(jnp.where(oh1, pos, 0.0), axis=1, keepdims=True)
    r2 = jnp.sum(jnp.where(oh2, pos, 0.0), axis=1, keepdims=True)
    lrow_ref[...] = jnp.where(lane == 0, r1, jnp.where(lane == 1, r2, 0.0)).astype(jnp.int32)
    gt_ref[...] = jnp.where(lane == 0, g1, jnp.where(lane == 1, g2, 0.0))
    sub = lax.broadcasted_iota(jnp.int32, (SUBLANES, LANES), 0)
    tab = jnp.where(sub == 0, tiles, jnp.where(sub == 1, off, jnp.where(sub == 2, carry[...], 0.0)))
    tab_ref[...] = tab.astype(jnp.int32)
    carry[...] = carry[...] + tiles * SUBLANES
    tot_ref[...] = carry[...].astype(jnp.int32)


def _router(logits):
    t = logits.shape[0]
    tb = ROW_TILE
    blk = pl.BlockSpec((tb, LANES), lambda i: (i, 0))
    return pl.pallas_call(
        _router_kernel,
        grid=(t // tb,),
        in_specs=[blk],
        out_specs=[blk, blk, pl.BlockSpec((SUBLANES, LANES), lambda i: (i, 0)),
                   pl.BlockSpec((1, LANES), lambda i: (0, 0))],
        out_shape=[jax.ShapeDtypeStruct((t, LANES), jnp.int32), jax.ShapeDtypeStruct((t, LANES), F32),
                   jax.ShapeDtypeStruct((t // tb * SUBLANES, LANES), jnp.int32),
                   jax.ShapeDtypeStruct((1, LANES), jnp.int32)],
        scratch_shapes=[pltpu.VMEM((1, LANES), F32)],
        compiler_params=_cparams(("arbitrary",), 16),
        name="moe_router",
    )(logits)


def _tile_copies(nt_ref, lo_ref, gd_ref, blk, make_copy, wait):
    for e in range(N_EXPERTS):
        idx = blk * N_EXPERTS + e
        lo, gd = lo_ref[idx], gd_ref[idx]

        def one(j, _):
            cp = make_copy(pl.multiple_of(lo + j * SUBLANES, SUBLANES), pl.multiple_of(gd + j * SUBLANES, SUBLANES))
            if wait:
                cp.wait()
            else:
                cp.start()
            return 0

        lax.fori_loop(0, nt_ref[idx], one, 0)


def _make_dispatch_kernel(tb, n_blocks):
    def kernel(ps_ref, nb_ref, nt_ref, lo_ref, gd_ref, lrow_ref, h_ref, xb_ref, xs, zbuf, sem):
        i = pl.program_id(0)

        @pl.when(i == 0)
        def _():
            zbuf[...] = jnp.zeros_like(zbuf)

            def zero_block(row0):
                dst = pl.multiple_of(row0, MOE_BLOCK)
                cp = pltpu.make_async_copy(zbuf, xb_ref.at[pl.ds(dst, MOE_BLOCK), :], sem.at[1])
                cp.start()
                cp.wait()

            for e in range(N_EXPERTS):
                @pl.when(nb_ref[e] > 0)
                def _():
                    zero_block(ps_ref[e] + (nb_ref[e] - 1) * MOE_BLOCK)

            used = ps_ref[N_EXPERTS - 1] // MOE_BLOCK + nb_ref[N_EXPERTS - 1]

            def tail(j, _):
                zero_block(j * MOE_BLOCK)
                return 0

            lax.fori_loop(used, n_blocks, tail, 0)

        lr = lrow_ref[...]
        r_iota = lax.broadcasted_iota(jnp.int32, (tb, STAGE_ROWS), 1)
        sel = jnp.where((r_iota == lr[:, 0:1]) | (r_iota == lr[:, 1:2]), 1.0, 0.0).astype(BF16)
        xs[...] = _dot_tn(sel, h_ref[...].astype(BF16))

        def make_copy(lo, gd):
            return pltpu.make_async_copy(xs.at[pl.ds(lo, SUBLANES), :], xb_ref.at[pl.ds(gd, SUBLANES), :], sem.at[0])

        _tile_copies(nt_ref, lo_ref, gd_ref, i, make_copy, wait=False)
        _tile_copies(nt_ref, lo_ref, gd_ref, i, make_copy, wait=True)

    return kernel


def _dispatch(h2, lrow, pad_start, nblk, ntile, loff, gdest, n_blocks):
    t, d = h2.shape
    tb = ROW_TILE
    grid_spec = pltpu.PrefetchScalarGridSpec(
        num_scalar_prefetch=5, grid=(t // tb,),
        in_specs=[pl.BlockSpec((tb, LANES), lambda i, *_: (i, 0)),
                  pl.BlockSpec((tb, d), lambda i, *_: (i, 0))],
        out_specs=pl.BlockSpec(memory_space=pl.ANY),
        scratch_shapes=[pltpu.VMEM((STAGE_ROWS, d), F32), pltpu.VMEM((MOE_BLOCK, d), F32),
                        pltpu.SemaphoreType.DMA((2,))])
    return pl.pallas_call(
        _make_dispatch_kernel(tb, n_blocks), grid_spec=grid_spec,
        out_shape=jax.ShapeDtypeStruct((n_blocks * MOE_BLOCK, d), F32),
        compiler_params=_cparams(("arbitrary",), 24),
        name="moe_dispatch",
    )(pad_start, nblk, ntile, loff, gdest, lrow, h2)


def _expert_kernel(be_ref, nu_ref, xb_ref, wg_ref, wu_ref, wd_ref, yb_ref):
    i = pl.program_id(0)

    @pl.when(i < nu_ref[0])
    def _():
        xb = xb_ref[...].astype(BF16)
        a = (_silu(_dot(xb, wg_ref[...])) * _dot(xb, wu_ref[...])).astype(BF16)
        yb_ref[...] = _dot(a, wd_ref[...])

    @pl.when(i >= nu_ref[0])
    def _():
        yb_ref[...] = jnp.zeros_like(yb_ref)


def _experts(xb, block_e, n_used, wg, wu, wd):
    _, d, f = wg.shape
    n_blocks = xb.shape[0] // MOE_BLOCK
    blk = pl.BlockSpec((MOE_BLOCK, d), lambda i, be, nu: (i, 0))
    grid_spec = pltpu.PrefetchScalarGridSpec(
        num_scalar_prefetch=2, grid=(n_blocks,),
        in_specs=[blk,
                  pl.BlockSpec((None, d, f), lambda i, be, nu: (be[i], 0, 0)),
                  pl.BlockSpec((None, d, f), lambda i, be, nu: (be[i], 0, 0)),
                  pl.BlockSpec((None, f, d), lambda i, be, nu: (be[i], 0, 0))],
        out_specs=blk)
    return pl.pallas_call(
        _expert_kernel, grid_spec=grid_spec,
        out_shape=jax.ShapeDtypeStruct(xb.shape, F32),
        compiler_params=_cparams(("arbitrary",), 56),
        name="moe_experts",
    )(block_e, n_used, xb, wg, wu, wd)


def _make_combine_kernel(tb):
    def kernel(nt_ref, lo_ref, gd_ref, yb_ref, lrow_ref, gt_ref, x_ref, g3_ref, fga_ref, o_ref, ys, sem):
        i = pl.program_id(0)

        @pl.when(i == 0)
        def _():
            ys[...] = jnp.zeros_like(ys)

        def make_copy(lo, gd):
            return pltpu.make_async_copy(yb_ref.at[pl.ds(gd, SUBLANES), :], ys.at[pl.ds(lo, SUBLANES), :], sem.at[0])

        _tile_copies(nt_ref, lo_ref, gd_ref, i, make_copy, wait=False)
        _tile_copies(nt_ref, lo_ref, gd_ref, i, make_copy, wait=True)

        lr = lrow_ref[...]
        gt = gt_ref[...]
        r_iota = lax.broadcasted_iota(jnp.int32, (tb, STAGE_ROWS), 1)
        q = jnp.where(r_iota == lr[:, 0:1], gt[:, 0:1], 0.0) + jnp.where(r_iota == lr[:, 1:2], gt[:, 1:2], 0.0)
        qh, ql = _split_bf16(q)
        yh, yl = _split_bf16(ys[...])
        f = _dot(qh, yh) + _dot(ql, yh) + _dot(qh, yl)
        o_ref[...] = x_ref[...] + fga_ref[...] * _rms(f, g3_ref[...])

    return kernel


def _combine(yb, lrow, ntile, loff, gdest, gates, x, mods6, layer, first_row, rows_per_batch, g3):
    t, d = x.shape
    tb = ROW_TILE
    if rows_per_batch is None:
        fga_map = lambda i, *_: (layer, first_row, 5, 0, 0)
    else:
        fga_map = lambda i, *_: (layer, first_row + i // rows_per_batch, 5, 0, 0)
    grid_spec = pltpu.PrefetchScalarGridSpec(
        num_scalar_prefetch=3, grid=(t // tb,),
        in_specs=[pl.BlockSpec(memory_space=pl.ANY),
                  pl.BlockSpec((tb, LANES), lambda i, *_: (i, 0)),
                  pl.BlockSpec((tb, LANES), lambda i, *_: (i, 0)),
                  pl.BlockSpec((tb, d), lambda i, *_: (i, 0)),
                  pl.BlockSpec((1, d), lambda i, *_: (0, 0)),
                  pl.BlockSpec((None, None, None, 1, d), fga_map)],
        out_specs=pl.BlockSpec((tb, d), lambda i, *_: (i, 0)),
        scratch_shapes=[pltpu.VMEM((STAGE_ROWS, d), F32), pltpu.SemaphoreType.DMA((1,))])
    return pl.pallas_call(
        _make_combine_kernel(tb), grid_spec=grid_spec,
        out_shape=jax.ShapeDtypeStruct((t, d), F32),
        compiler_params=_cparams(("arbitrary",), 32),
        name="moe_combine",
    )(ntile, loff, gdest, yb, lrow, gates, x, g3, mods6)


def _moe(h2, logits, x, mods6, layer, first_row, rows_per_batch, g3, wg, wu, wd):
    t = x.shape[0]
    n_tok_blocks = t // ROW_TILE
    max_rows = t * TOP_K + n_tok_blocks * N_EXPERTS * (SUBLANES - 1)
    n_blocks = -(-max_rows // MOE_BLOCK) + N_EXPERTS
    lrow, gates, tab, tot = _router(logits)
    tab = tab.reshape(n_tok_blocks, SUBLANES, LANES)[:, :, :N_EXPERTS]
    ntile, loff, prior = tab[:, 0], tab[:, 1], tab[:, 2]
    nblk = (tot[0, :N_EXPERTS] + MOE_BLOCK - 1) // MOE_BLOCK
    blk_end = jnp.cumsum(nblk)
    pad_start = ((blk_end - nblk) * MOE_BLOCK).astype(jnp.int32)
    gdest = (pad_start[None, :] + prior).astype(jnp.int32)
    n_used = blk_end[-1:].astype(jnp.int32)
    blk = jnp.minimum(jnp.arange(n_blocks, dtype=jnp.int32), n_used[0] - 1)
    block_e = jnp.minimum(jnp.sum(blk[:, None] >= blk_end[None, :], axis=1), N_EXPERTS - 1).astype(jnp.int32)
    ntile, loff, gdest = ntile.reshape(-1), loff.reshape(-1), gdest.reshape(-1)
    xb = _dispatch(h2, lrow, pad_start, nblk.astype(jnp.int32), ntile, loff, gdest, n_blocks)
    yb = _experts(xb, block_e, n_used, wg, wu, wd)
    return _combine(yb, lrow, ntile, loff, gdest, gates, x, mods6, layer, first_row, rows_per_batch, g3)


def _pad_cols(w, n):
    return jnp.pad(w, ((0, 0), (0, n - w.shape[1])))


def kernel(x_prompt, x_sample, state_mlstm_C, state_mlstm_n, state_mlstm_m, state_ret_S, c, c_ctx, mod_w, mod_b, norm_g, mlstm_w_in, mlstm_gate_b, mlstm_conv_w, mlstm_conv_b, mlstm_head_g, mlstm_w_out, ret_w_in, ret_decay_logit, ret_head_g, ret_w_out, ffn_w_gate, ffn_w_up, ffn_w_down, moe_router, moe_w_gate, moe_w_up, moe_w_down):
    bp, n_p, d = x_prompt.shape
    bs, n_s, _ = x_sample.shape
    depth = mod_w.shape[0]
    assert depth == 2 and CHUNK % GRID_W == 0 and n_s % GRID_W == 0

    cond = jnp.zeros((MOD_ROWS, d), F32).at[0].set(c_ctx).at[1:1 + bs].set(c)
    mods6 = _modulation(cond, mod_w, mod_b).reshape(depth, MOD_ROWS, N_MOD, 1, d)

    groups = [dict(x=x_prompt.reshape(bp * n_p, d), first=0, rpb=None, nseq=bp, ntok=n_p, prompt=True),
              dict(x=x_sample.reshape(bs * n_s, d), first=1, rpb=n_s // ROW_TILE, nseq=bs, ntok=n_s, prompt=False)]

    j = 0
    ml_qk = (mlstm_w_in.shape[2] - 4 * HEADS) // 2
    ml_v = ml_qk // 2
    w_in = mlstm_w_in[j]
    wqk = w_in[:, :ml_qk].astype(BF16)
    wv = w_in[:, ml_qk:ml_qk + ml_v].astype(BF16)
    wo = w_in[:, ml_qk + ml_v:ml_qk + 2 * ml_v].astype(BF16)
    w_gate = w_in[:, ml_qk + 2 * ml_v:]
    wgh, wgl = _split_bf16(_pad_cols(w_gate, LANES))
    wgth, wgtl = _split_bf16(w_gate.T)
    bcol = _pad_cols(mlstm_gate_b[j][None, :], LANES)
    brow = mlstm_gate_b[j][:, None]
    g = norm_g[0]
    w_out0 = mlstm_w_out[j].astype(BF16)
    fwg, fwu, fwd = ffn_w_gate[j].astype(BF16), ffn_w_up[j].astype(BF16), ffn_w_down[j].astype(BF16)
    new_c = new_n = new_m = None
    for grp in groups:
        args = (mods6, 0, grp["first"], grp["rpb"])
        qk, v, o, gcol, grow = _proj_mlstm(grp["x"], *args, g[0:1], wqk, wv, wo, wgh, wgl, wgth, wgtl, bcol, brow)
        grow3 = grow.reshape(grow.shape[0], 1, grow.shape[1])
        if grp["prompt"]:
            m0 = jnp.zeros((grp["nseq"] * 2 * HEADS,), F32)
            hs, new_c, new_n, new_m = _mlstm_scan(qk, v, gcol, grow3, mlstm_conv_w[j], mlstm_conv_b[j][None, :], m0,
                                                  None, grp["nseq"], grp["ntok"], True)
        else:
            (hs,) = _mlstm_scan(qk, v, gcol, grow3, mlstm_conv_w[j], mlstm_conv_b[j][None, :],
                                state_mlstm_m[:, j].reshape(-1),
                                (state_mlstm_C[:, j], state_mlstm_n[:, j]), grp["nseq"], grp["ntok"], False)
        (x1,) = _mixer_out(hs, o, 0, mlstm_head_g[j][None, :], w_out0, grp["x"], *args, g[1:2], True)
        grp["x"] = _ffn(x1, *args, g[2:3], g[3:4], fwg, fwu, fwd)

    ret_qk = ret_w_in.shape[2] // 3
    w_in = ret_w_in[j]
    rwqk = w_in[:, :ret_qk].astype(BF16)
    rwv = w_in[:, ret_qk:2 * ret_qk].astype(BF16)
    rwg = w_in[:, 2 * ret_qk:].astype(BF16)
    g = norm_g[1]
    w_out1 = ret_w_out[j].astype(BF16)
    wrh, wrl = _split_bf16(_pad_cols(moe_router[j], LANES))
    ewg, ewu, ewd = moe_w_gate[j].astype(BF16), moe_w_up[j].astype(BF16), moe_w_down[j].astype(BF16)
    decay_flat = ret_decay_logit[j].reshape(-1)
    new_s = None
    for grp in groups:
        args = (mods6, 1, grp["first"], grp["rpb"])
        qk, v, gate = _proj_ret(grp["x"], *args, g[0:1], rwqk, rwv, rwg)
        if grp["prompt"]:
            hs, new_s = _ret_scan(qk, v, decay_flat, None, None, grp["nseq"], grp["ntok"], True)
        else:
            (hs,) = _ret_scan(qk, v, decay_flat, _rope_tables(grp["ntok"], ret_qk // (2 * HEADS)), state_ret_S[:, j],
                              grp["nseq"], grp["ntok"], False)
        x1, h2, logits = _mixer_out(hs, gate, 0, ret_head_g[j][None, :], w_out1, grp["x"], *args, g[1:2], False,
                                    router=(g[2:3], wrh, wrl))
        grp["x"] = _moe(h2, logits, x1, *args, g[3:4], ewg, ewu, ewd)

    y_prompt = groups[0]["x"].reshape(bp, n_p, d)
    y_sample = groups[1]["x"].reshape(bs, n_s, d)
    return (y_prompt, y_sample, new_c[:, None], new_n[:, None], new_m[:, None, :, :, 0], new_s[:, None])
```

```python
import functools
import math

import jax
import jax.numpy as jnp
from jax import lax
from jax.experimental import pallas as pl
from jax.experimental.pallas import tpu as pltpu

F32 = jnp.float32
BF16 = jnp.bfloat16

EPS = 1e-6
N_MOD = 6
HEADS = 4
CHUNK = 128
GRID_W = 64
ROPE_BASE = 10000.0
N_EXPERTS = 8
TOP_K = 2
MOE_BLOCK = 256
LANES = 128
SUBLANES = 8
ROW_TILE = 256
MOD_ROWS = 8
STAGE_ROWS = -(-(TOP_K * ROW_TILE + N_EXPERTS * (SUBLANES - 1)) // LANES) * LANES
MIB = 1024 * 1024


def _cparams(sem, vmem_mib):
    return pltpu.CompilerParams(dimension_semantics=sem, vmem_limit_bytes=vmem_mib * MIB)


def _dot(a, b):
    return jnp.dot(a, b, preferred_element_type=F32)


def _dot_nt(a, b):
    return lax.dot_general(a, b, (((1,), (1,)), ((), ())), preferred_element_type=F32)


def _dot_tn(a, b):
    return lax.dot_general(a, b, (((0,), (0,)), ((), ())), preferred_element_type=F32)


def _split_bf16(x):
    hi = x.astype(BF16)
    lo = (x - hi.astype(F32)).astype(BF16)
    return hi, lo


def _rms(x, g):
    return x * lax.rsqrt(jnp.mean(x * x, -1, keepdims=True) + EPS) * g


def _adaln(x, g, shift, scale):
    return _rms(x, g) * (1.0 + scale) + shift


def _silu(x):
    return x * jax.nn.sigmoid(x)


def _logsig(x):
    return jnp.minimum(x, 0.0) - jnp.log1p(jnp.exp(-jnp.abs(x)))


def _const_spec(shape):
    nd = len(shape)
    return pl.BlockSpec(shape, lambda *_: (0,) * nd, pipeline_mode=pl.Buffered(1))


def _mod_spec(d, layer, comp, rows_per_batch, first_row):
    if rows_per_batch is None:
        return pl.BlockSpec((None, None, None, 1, d), lambda i: (layer, first_row, comp, 0, 0))
    return pl.BlockSpec((None, None, None, 1, d), lambda i: (layer, first_row + i // rows_per_batch, comp, 0, 0))


def _mod_kernel(c_ref, w_ref, b_ref, o_ref):
    s = _silu(c_ref[...]).astype(BF16)
    o_ref[...] = _dot(s, w_ref[...].astype(BF16)) + b_ref[...]


def _modulation(cond, mod_w, mod_b):
    depth, d, n = mod_w.shape
    tn = n // 4
    return pl.pallas_call(
        _mod_kernel,
        grid=(depth, n // tn),
        in_specs=[pl.BlockSpec((MOD_ROWS, d), lambda l, j: (0, 0)),
                  pl.BlockSpec((None, d, tn), lambda l, j: (l, 0, j)),
                  pl.BlockSpec((None, 1, tn), lambda l, j: (l, 0, j))],
        out_specs=pl.BlockSpec((None, MOD_ROWS, tn), lambda l, j: (l, 0, j)),
        out_shape=jax.ShapeDtypeStruct((depth, MOD_ROWS, n), F32),
        compiler_params=_cparams(("arbitrary", "arbitrary"), 40),
        name="modulation",
    )(cond, mod_w, mod_b.reshape(depth, 1, n))


def _proj_mlstm_kernel(tiles_per_seq, x_ref, xp_ref, xn_ref, g_ref, sh_ref, sc_ref, wqk_ref, wv_ref, wo_ref,
                       wgh_ref, wgl_ref, wgth_ref, wgtl_ref, bcol_ref, brow_ref, cw_ref, cb_ref,
                       q_ref, kt_ref, v_ref, o_ref, gcol_ref, grow_ref):
    tm = x_ref.shape[0]
    nq = q_ref.shape[1]
    h = _adaln(x_ref[...], g_ref[...], sh_ref[...], sc_ref[...])
    hb, hl = _split_bf16(h)
    p = _dot(hb, wqk_ref[...])
    x_halo = jnp.concatenate([xp_ref[...], xn_ref[...]], axis=0)
    p_halo = _dot(_adaln(x_halo, g_ref[...], sh_ref[...], sc_ref[...]).astype(BF16), wqk_ref[...])
    pos = pl.program_id(0) % tiles_per_seq
    p_prev = p_halo[SUBLANES - 1:SUBLANES, :] * jnp.where(pos > 0, 1.0, 0.0)
    p_next = p_halo[SUBLANES:SUBLANES + 1, :] * jnp.where(pos < tiles_per_seq - 1, 1.0, 0.0)
    rowi = lax.broadcasted_iota(jnp.int32, p.shape, 0)
    prev = jnp.where(rowi == 0, p_prev, pltpu.roll(p, 1, axis=0))
    nxt = jnp.where(rowi == tm - 1, p_next, pltpu.roll(p, tm - 1, axis=0))
    y = prev * cw_ref[0:1, :] + p * cw_ref[1:2, :] + nxt * cw_ref[2:3, :] + cb_ref[...]
    q_ref[...] = (y[:, :nq] * (nq // HEADS) ** -0.5).astype(BF16)
    kt_ref[...] = y[:, nq:].T.astype(BF16)
    v_ref[...] = _dot(hb, wv_ref[...]).astype(BF16)
    o_ref[...] = _dot(hb, wo_ref[...])
    gc = _dot(hb, wgh_ref[...]) + _dot(hl, wgh_ref[...]) + _dot(hb, wgl_ref[...]) + bcol_ref[...]
    for hh in range(HEADS):
        gcol_ref[hh] = gc if hh == 0 else pltpu.roll(gc, LANES - hh, axis=1)
    grow_ref[...] = (_dot_nt(wgth_ref[...], hb) + _dot_nt(wgth_ref[...], hl) + _dot_nt(wgtl_ref[...], hb)
                     + brow_ref[...])


def _proj_mlstm(x, n_tok, mods6, layer, first_row, rows_per_batch, g, wqk, wv, wo, wgh, wgl, wgth, wgtl, bcol, brow,
                conv_w, conv_b):
    t, d = x.shape
    tm = ROW_TILE
    ms = lambda comp: _mod_spec(d, layer, comp, rows_per_batch, first_row)
    ng = wgth.shape[0]
    nq = wqk.shape[1] // 2
    tps = tm // SUBLANES
    last = t // SUBLANES - 1
    return pl.pallas_call(
        functools.partial(_proj_mlstm_kernel, n_tok // tm),
        grid=(t // tm,),
        in_specs=[pl.BlockSpec((tm, d), lambda i: (i, 0)),
                  pl.BlockSpec((SUBLANES, d), lambda i: (jnp.maximum(i * tps - 1, 0), 0)),
                  pl.BlockSpec((SUBLANES, d), lambda i: (jnp.minimum((i + 1) * tps, last), 0)),
                  _const_spec((1, d)), ms(0), ms(1),
                  _const_spec(wqk.shape), _const_spec(wv.shape), _const_spec(wo.shape),
                  _const_spec(wgh.shape), _const_spec(wgl.shape), _const_spec(wgth.shape), _const_spec(wgtl.shape),
                  _const_spec(bcol.shape), _const_spec(brow.shape),
                  _const_spec(conv_w.shape), _const_spec(conv_b.shape)],
        out_specs=[pl.BlockSpec((tm, nq), lambda i: (i, 0)),
                   pl.BlockSpec((nq, tm), lambda i: (0, i)),
                   pl.BlockSpec((tm, wv.shape[1]), lambda i: (i, 0)),
                   pl.BlockSpec((tm, wo.shape[1]), lambda i: (i, 0)),
                   pl.BlockSpec((HEADS, tm, LANES), lambda i: (0, i, 0)),
                   pl.BlockSpec((ng, tm), lambda i: (0, i))],
        out_shape=[jax.ShapeDtypeStruct((t, nq), BF16),
                   jax.ShapeDtypeStruct((nq, t), BF16),
                   jax.ShapeDtypeStruct((t, wv.shape[1]), BF16),
                   jax.ShapeDtypeStruct((t, wo.shape[1]), F32),
                   jax.ShapeDtypeStruct((HEADS, t, LANES), F32),
                   jax.ShapeDtypeStruct((ng, t), F32)],
        compiler_params=_cparams(("arbitrary",), 52),
        name="proj_mlstm",
    )(x, x, x, g, mods6, mods6, wqk, wv, wo, wgh, wgl, wgth, wgtl, bcol, brow, conv_w, conv_b)


def _make_mlstm_scan_kernel(n_tok, dk, dv, has_state, emit_state):
    L = CHUNK
    nc = n_tok // L
    assert nc % 2 == 0 and L == LANES

    def kernel(*refs):
        it = iter(refs)
        m0_ref = next(it)
        q_ref, kt_ref, v_ref, gcol_ref = next(it), next(it), next(it), next(it)
        gi_refs = (next(it), next(it))
        if has_state:
            c0_ref, n0_ref = next(it), next(it)
        out_ref = next(it)
        if emit_state:
            cout_ref, nout_ref, mout_ref = next(it), next(it), next(it)
        brep, rmrep, rrow, stats, cst, cbf, nrep, nbf = (next(it) for _ in range(8))

        b = pl.program_id(0)
        hh = pl.program_id(1)

        ri = lax.broadcasted_iota(jnp.int32, (L, L), 0)
        ci = lax.broadcasted_iota(jnp.int32, (L, L), 1)
        lower = ri >= ci
        upper = ri <= ci
        masks = ((lower, upper), (upper, lower))

        tri = tuple(jnp.where(masks[d][0], 1.0, 0.0).astype(BF16) for d in range(2))
        rowid = lax.broadcasted_iota(jnp.int32, (L, LANES), 0)

        def running_max(x, reverse):
            s = 1
            while s < L:
                if reverse:
                    x = jnp.maximum(x, jnp.where(rowid < L - s, pltpu.roll(x, L - s, axis=0), -jnp.inf))
                else:
                    x = jnp.maximum(x, jnp.where(rowid >= s, pltpu.roll(x, s, axis=0), -jnp.inf))
                s *= 2
            return x

        def prep_chunk(c, _):
            r0 = pl.multiple_of(c * L, L)
            gc = gcol_ref[pl.ds(r0, L), :]
            for d in range(2):
                mk, mkt = masks[d]
                fr = jnp.broadcast_to(_logsig(gc[:, 8 * d + 4:8 * d + 5]), (L, LANES))
                ir = jnp.broadcast_to(gc[:, 8 * d:8 * d + 1], (L, LANES))
                f1 = fr.astype(BF16)
                e1 = fr - f1.astype(F32)
                f2 = e1.astype(BF16)
                f3 = (e1 - f2.astype(F32)).astype(BF16)
                b_rep = _dot(tri[d], f1) + _dot(tri[d], f2) + _dot(tri[d], f3)
                rm_rep = running_max(ir - b_rep, reverse=(d == 1))
                b_row = jnp.sum(jnp.where(mkt, fr, 0.0), axis=0, keepdims=True)
                brep[d, pl.ds(r0, L), :] = b_rep
                rmrep[d, pl.ds(r0, L), :] = rm_rep
                rrow[d, :, pl.ds(r0, L)] = gi_refs[d][:, pl.ds(r0, L)] - b_row
                end = 0 if d == 1 else L - 1
                stats[c, pl.ds(2 * d, 1), :] = b_rep[end:end + 1, :]
                stats[c, pl.ds(2 * d + 1, 1), :] = rm_rep[end:end + 1, :]
            return 0

        lax.fori_loop(0, nc, prep_chunk, 0, unroll=2)

        if has_state:
            cst[...] = c0_ref[...]
            for d in range(2):
                nrep[d] = jnp.broadcast_to(n0_ref[d], (dk, LANES))
        else:
            cst[...] = jnp.zeros_like(cst)
            nrep[...] = jnp.zeros_like(nrep)
        cbf[...] = cst[...].astype(BF16)
        nbf[...] = nrep[...].astype(BF16)
        m_init = tuple(jnp.full((1, LANES), m0_ref[(b * 2 + d) * HEADS + hh], F32) for d in range(2))

        def lanes(x, n):
            return jnp.concatenate([x] * (n // LANES), axis=1)

        def chunk(d, c, m):
            mk = masks[d][0]
            r0 = pl.multiple_of(c * L, L)
            q = q_ref[pl.ds(r0, L), :]
            kt = kt_ref[:, pl.ds(r0, L)]
            v = v_ref[pl.ds(r0, L), :]
            rr = rrow[d, :, pl.ds(r0, L)]
            st = stats[c]
            b_end, rm_end = st[2 * d:2 * d + 1, :], st[2 * d + 1:2 * d + 2, :]
            mm = jnp.maximum(m, rmrep[d, pl.ds(r0, L), :])
            qkn = _dot(q, jnp.concatenate([kt, nbf[d]], axis=1))
            s = qkn[:, :L] * jnp.where(mk, jnp.exp(rr - mm), 0.0)
            w_inter = jnp.exp(m - mm)
            num = lanes(w_inter, dv) * _dot(q, cbf[d]) + _dot(s.astype(BF16), v)
            den = w_inter * qkn[:, L:] + jnp.sum(s, axis=1, keepdims=True)
            inv = 1.0 / jnp.maximum(jnp.abs(den), jnp.exp(-(brep[d, pl.ds(r0, L), :] + mm)))
            h = num * lanes(inv, dv)
            m_end = jnp.maximum(m, rm_end)
            decay = jnp.exp(m - m_end)
            kw = kt.astype(F32) * jnp.exp(rr - m_end)
            c_new = lanes(decay, dv) * cst[d] + _dot(kw.astype(BF16), v)
            n_new = decay * nrep[d] + jnp.sum(kw, axis=1, keepdims=True)
            cst[d] = c_new
            cbf[d] = c_new.astype(BF16)
            nrep[d] = n_new
            nbf[d] = n_new.astype(BF16)
            return h, b_end + m_end

        def make_body(accumulate):
            def body(i, carry):
                mf, mb = carry
                hf, mf = chunk(0, i, mf)
                rf = pl.multiple_of(i * L, L)
                cb = nc - 1 - i
                hb, mb = chunk(1, cb, mb)
                rb = pl.multiple_of(cb * L, L)
                if accumulate:
                    out_ref[pl.ds(rf, L), :] += hf
                    out_ref[pl.ds(rb, L), :] += hb
                else:
                    out_ref[pl.ds(rf, L), :] = hf
                    out_ref[pl.ds(rb, L), :] = hb
                return mf, mb
            return body

        carry = lax.fori_loop(0, nc // 2, make_body(False), m_init)
        carry = lax.fori_loop(nc // 2, nc, make_body(True), carry)

        if emit_state:
            cout_ref[...] = cst[...]
            for d in range(2):
                nout_ref[d, pl.ds(hh, 1), :] = nrep[d].T[0:1, :]
                mout_ref[d, pl.ds(hh, 1), :] = carry[d]

    return kernel


def _mlstm_scan(q, kt, v, gcol, grow3, m0_flat, state, n_seq, n_tok, emit_state):
    dk = q.shape[1] // HEADS
    dv = v.shape[1] // HEADS
    has_state = state is not None
    kern = _make_mlstm_scan_kernel(n_tok, dk, dv, has_state, emit_state)
    grow_spec = lambda r: pl.BlockSpec((None, 1, n_tok), lambda b, h, m: (r * HEADS + h, 0, b))
    in_specs = [pl.BlockSpec((n_tok, dk), lambda b, h, m: (b, h)),
                pl.BlockSpec((dk, n_tok), lambda b, h, m: (h, b)),
                pl.BlockSpec((n_tok, dv), lambda b, h, m: (b, h)),
                pl.BlockSpec((None, n_tok, LANES), lambda b, h, m: (h, b, 0)),
                grow_spec(0), grow_spec(2)]
    args = [q, kt, v, gcol, grow3, grow3]
    if has_state:
        in_specs += [pl.BlockSpec((None, 2, None, dk, dv), lambda b, h, m: (b, 0, h, 0, 0)),
                     pl.BlockSpec((None, 2, None, dk, 1), lambda b, h, m: (b, 0, h, 0, 0))]
        args += [state[0], state[1][..., None]]
    out_specs = [pl.BlockSpec((n_tok, dv), lambda b, h, m: (b, h))]
    out_shape = [jax.ShapeDtypeStruct((n_seq * n_tok, HEADS * dv), F32)]
    if emit_state:
        out_specs += [pl.BlockSpec((None, 2, None, dk, dv), lambda b, h, m: (b, 0, h, 0, 0)),
                      pl.BlockSpec((None, 2, HEADS, dk), lambda b, h, m: (b, 0, 0, 0)),
                      pl.BlockSpec((None, 2, HEADS, LANES), lambda b, h, m: (b, 0, 0, 0))]
        out_shape += [jax.ShapeDtypeStruct((n_seq, 2, HEADS, dk, dv), F32),
                      jax.ShapeDtypeStruct((n_seq, 2, HEADS, dk), F32),
                      jax.ShapeDtypeStruct((n_seq, 2, HEADS, LANES), F32)]
    grid_spec = pltpu.PrefetchScalarGridSpec(
        num_scalar_prefetch=1, grid=(n_seq, HEADS), in_specs=in_specs, out_specs=out_specs,
        scratch_shapes=[pltpu.VMEM((2, n_tok, LANES), F32),
                        pltpu.VMEM((2, n_tok, LANES), F32),
                        pltpu.VMEM((2, 1, n_tok), F32),
                        pltpu.VMEM((n_tok // CHUNK, SUBLANES, LANES), F32),
                        pltpu.VMEM((2, dk, dv), F32), pltpu.VMEM((2, dk, dv), BF16),
                        pltpu.VMEM((2, dk, LANES), F32), pltpu.VMEM((2, dk, LANES), BF16)])
    return pl.pallas_call(
        kern, grid_spec=grid_spec, out_shape=out_shape,
        compiler_params=_cparams(("arbitrary", "arbitrary"), 56),
        name="mlstm_scan",
    )(m0_flat, *args)


def _proj_ret_kernel(x_ref, g_ref, sh_ref, sc_ref, wqk_ref, wv_ref, wg_ref, qk_ref, v_ref, gate_ref):
    hb = _adaln(x_ref[...], g_ref[...], sh_ref[...], sc_ref[...]).astype(BF16)
    qk_ref[...] = _dot(hb, wqk_ref[...])
    v_ref[...] = _dot(hb, wv_ref[...]).astype(BF16)
    gate_ref[...] = _dot(hb, wg_ref[...])


def _proj_ret(x, mods6, layer, first_row, rows_per_batch, g, wqk, wv, wg):
    t, d = x.shape
    tm = ROW_TILE
    ms = lambda comp: _mod_spec(d, layer, comp, rows_per_batch, first_row)
    return pl.pallas_call(
        _proj_ret_kernel,
        grid=(t // tm,),
        in_specs=[pl.BlockSpec((tm, d), lambda i: (i, 0)), _const_spec((1, d)), ms(0), ms(1),
                  _const_spec(wqk.shape), _const_spec(wv.shape), _const_spec(wg.shape)],
        out_specs=[pl.BlockSpec((tm, wqk.shape[1]), lambda i: (i, 0)),
                   pl.BlockSpec((tm, wv.shape[1]), lambda i: (i, 0)),
                   pl.BlockSpec((tm, wg.shape[1]), lambda i: (i, 0))],
        out_shape=[jax.ShapeDtypeStruct((t, wqk.shape[1]), F32),
                   jax.ShapeDtypeStruct((t, wv.shape[1]), BF16),
                   jax.ShapeDtypeStruct((t, wg.shape[1]), F32)],
        compiler_params=_cparams(("arbitrary",), 48),
        name="proj_ret",
    )(x, g, mods6, mods6, wqk, wv, wg)


def _make_ret_scan_kernel(n_tok, dk, dv, has_state, emit_state, rope):
    L = CHUNK
    nc = n_tok // L
    assert nc % 2 == 0
    rows_per_chunk = L // GRID_W

    def kernel(*refs):
        it = iter(refs)
        dl_ref = next(it)
        q_ref, k_ref, v_ref = next(it), next(it), next(it)
        if rope:
            rcos_ref, rsin_ref, ccos_ref, csin_ref = next(it), next(it), next(it), next(it)
        if has_state:
            s0_ref = next(it)
        out_ref = next(it)
        if emit_state:
            sout_ref = next(it)
        sst = next(it)

        hh = pl.program_id(1)
        ri = lax.broadcasted_iota(jnp.int32, (L, L), 0)
        ci = lax.broadcasted_iota(jnp.int32, (L, L), 1)
        pos = lax.broadcasted_iota(jnp.int32, (L, 1), 0).astype(F32)
        rel = (ri - ci).astype(F32)

        dmat, q_dec, k_dec, c_dec = [], [], [], []
        for d in range(2):
            lg = _logsig(jnp.full((1, 1), dl_ref[d * HEADS + hh], F32))
            if d == 0:
                dmat.append(jnp.where(ri >= ci, jnp.exp(lg * jnp.maximum(rel, 0.0)), 0.0))
                q_dec.append(jnp.exp(lg * (pos + 1.0)))
                k_dec.append(jnp.exp(lg * (L - 1.0 - pos)))
            else:
                dmat.append(jnp.where(ri <= ci, jnp.exp(lg * jnp.maximum(-rel, 0.0)), 0.0))
                q_dec.append(jnp.exp(lg * (L - pos)))
                k_dec.append(jnp.exp(lg * pos))
            c_dec.append(jnp.exp(lg * float(L)))

        if has_state:
            sst[...] = s0_ref[...]
        else:
            sst[...] = jnp.zeros_like(sst)

        if rope:
            tok = lax.broadcasted_iota(jnp.int32, (L, LANES), 0)

        def rotate(x, c):
            if not rope:
                return x
            rc, rs = ccos_ref[...], csin_ref[...]
            row_c = rcos_ref[c * rows_per_chunk]
            row_s = rsin_ref[c * rows_per_chunk]
            for j in range(1, rows_per_chunk):
                sel = tok >= j * GRID_W
                row_c = jnp.where(sel, rcos_ref[c * rows_per_chunk + j], row_c)
                row_s = jnp.where(sel, rsin_ref[c * rows_per_chunk + j], row_s)
            xa, xb = x[:, :LANES], x[:, LANES:]
            ya = xa * row_c + pltpu.roll(xa, LANES // 2, axis=1) * row_s
            yb = xb * rc + pltpu.roll(xb, LANES // 2, axis=1) * rs
            return jnp.concatenate([ya, yb], axis=1)

        def chunk(d, c):
            r0 = pl.multiple_of(c * L, L)
            q = rotate(q_ref[pl.ds(r0, L), :], c)
            k = rotate(k_ref[pl.ds(r0, L), :], c) * dk ** -0.5
            v = v_ref[pl.ds(r0, L), :]
            qb = q.astype(BF16)
            kb = k.astype(BF16)
            s = _dot_nt(qb, kb) * dmat[d]
            s_old = sst[d]
            out = _dot(s.astype(BF16), v) + _dot((q * q_dec[d]).astype(BF16), s_old.astype(BF16))
            sst[d] = c_dec[d] * s_old + _dot_tn((k * k_dec[d]).astype(BF16), v)
            return out

        def body1(i, _):
            rf = pl.multiple_of(i * L, L)
            out_ref[pl.ds(rf, L), :] = chunk(0, i)
            cb = nc - 1 - i
            rb = pl.multiple_of(cb * L, L)
            out_ref[pl.ds(rb, L), :] = chunk(1, cb)
            return 0

        def body2(i, _):
            rf = pl.multiple_of(i * L, L)
            out_ref[pl.ds(rf, L), :] += chunk(0, i)
            cb = nc - 1 - i
            rb = pl.multiple_of(cb * L, L)
            out_ref[pl.ds(rb, L), :] += chunk(1, cb)
            return 0

        lax.fori_loop(0, nc // 2, body1, 0)
        lax.fori_loop(nc // 2, nc, body2, 0)
        if emit_state:
            sout_ref[...] = sst[...]

    return kernel


def _ret_scan(qk, v, decay_flat, rope_tabs, state, n_seq, n_tok, emit_state):
    dk = qk.shape[1] // (2 * HEADS)
    dv = v.shape[1] // HEADS
    has_state = state is not None
    rope = rope_tabs is not None
    kern = _make_ret_scan_kernel(n_tok, dk, dv, has_state, emit_state, rope)
    in_specs = [pl.BlockSpec((n_tok, dk), lambda b, h, m: (b, h)),
                pl.BlockSpec((n_tok, dk), lambda b, h, m: (b, HEADS + h)),
                pl.BlockSpec((n_tok, dv), lambda b, h, m: (b, h))]
    args = [qk, qk, v]
    if rope:
        for tab in rope_tabs:
            in_specs.append(pl.BlockSpec(tab.shape, lambda b, h, m, nd=tab.ndim: (0,) * nd))
            args.append(tab)
    if has_state:
        in_specs.append(pl.BlockSpec((None, 2, None, dk, dv), lambda b, h, m: (b, 0, h, 0, 0)))
        args.append(state)
    out_specs = [pl.BlockSpec((n_tok, dv), lambda b, h, m: (b, h))]
    out_shape = [jax.ShapeDtypeStruct((n_seq * n_tok, HEADS * dv), F32)]
    if emit_state:
        out_specs.append(pl.BlockSpec((None, 2, None, dk, dv), lambda b, h, m: (b, 0, h, 0, 0)))
        out_shape.append(jax.ShapeDtypeStruct((n_seq, 2, HEADS, dk, dv), F32))
    grid_spec = pltpu.PrefetchScalarGridSpec(
        num_scalar_prefetch=1, grid=(n_seq, HEADS), in_specs=in_specs, out_specs=out_specs,
        scratch_shapes=[pltpu.VMEM((2, dk, dv), F32)])
    return pl.pallas_call(
        kern, grid_spec=grid_spec, out_shape=out_shape,
        compiler_params=_cparams(("arbitrary", "arbitrary"), 56),
        name="ret_scan",
    )(decay_flat, *args)


def _rope_tables(n_tok, dk):
    r = dk // 4
    inv = 1.0 / (ROPE_BASE ** (jnp.arange(r, dtype=F32) / r))
    sign = jnp.concatenate([-jnp.ones((r,), F32), jnp.ones((r,), F32)])
    rows = jnp.arange(n_tok // GRID_W, dtype=F32)[:, None] * inv
    cols = (jnp.arange(CHUNK) % GRID_W).astype(F32)[:, None] * inv
    two = lambda a: jnp.concatenate([a, a], axis=-1)
    return (two(jnp.cos(rows))[:, None, :], (two(jnp.sin(rows)) * sign)[:, None, :],
            two(jnp.cos(cols)), two(jnp.sin(cols)) * sign)


def _make_mixer_out_kernel(dv, sigmoid_gate, with_router):
    def kernel(*refs):
        it = iter(refs)
        hs_ref, gate_ref, hg_ref, w_ref, x_ref, g1_ref, ga_ref = (next(it) for _ in range(7))
        if with_router:
            g2_ref, fsh_ref, fsc_ref, wrh_ref, wrl_ref = (next(it) for _ in range(5))
        xo_ref = next(it)
        if with_router:
            h2_ref, lg_ref = next(it), next(it)
        z_ref = next(it)
        for hh in range(HEADS):
            sl = slice(hh * dv, (hh + 1) * dv)
            seg = hs_ref[:, sl]
            y = seg * lax.rsqrt(jnp.mean(seg * seg, -1, keepdims=True) + EPS) * hg_ref[:, sl]
            gt = gate_ref[:, sl]
            act = jax.nn.sigmoid(gt) if sigmoid_gate else _silu(gt)
            z_ref[:, sl] = (act * y).astype(BF16)
        y = _dot(z_ref[...], w_ref[...])
        xn = x_ref[...] + ga_ref[...] * _rms(y, g1_ref[...])
        xo_ref[...] = xn
        if with_router:
            h2 = _adaln(xn, g2_ref[...], fsh_ref[...], fsc_ref[...])
            h2_ref[...] = h2
            hb, hl = _split_bf16(h2)
            lg_ref[...] = _dot(hb, wrh_ref[...]) + _dot(hl, wrh_ref[...]) + _dot(hb, wrl_ref[...])
    return kernel


def _mixer_out(hs, gate, gate_blk, head_g, w_out, x, mods6, layer, first_row, rows_per_batch, g1, sigmoid_gate,
               router=None):
    t, d = x.shape
    vdim = hs.shape[1]
    tm = ROW_TILE
    ms = lambda comp: _mod_spec(d, layer, comp, rows_per_batch, first_row)
    with_router = router is not None
    in_specs = [pl.BlockSpec((tm, vdim), lambda i: (i, 0)),
                pl.BlockSpec((tm, vdim), lambda i: (i, gate_blk)),
                _const_spec((1, vdim)), _const_spec(w_out.shape),
                pl.BlockSpec((tm, d), lambda i: (i, 0)), _const_spec((1, d)), ms(2)]
    args = [hs, gate, head_g, w_out, x, g1, mods6]
    out_specs = [pl.BlockSpec((tm, d), lambda i: (i, 0))]
    out_shape = [jax.ShapeDtypeStruct((t, d), F32)]
    if with_router:
        g2, wrh, wrl = router
        in_specs += [_const_spec((1, d)), ms(3), ms(4), _const_spec(wrh.shape), _const_spec(wrl.shape)]
        args += [g2, mods6, mods6, wrh, wrl]
        out_specs += [pl.BlockSpec((tm, d), lambda i: (i, 0)), pl.BlockSpec((tm, LANES), lambda i: (i, 0))]
        out_shape += [jax.ShapeDtypeStruct((t, d), F32), jax.ShapeDtypeStruct((t, LANES), F32)]
    return pl.pallas_call(
        _make_mixer_out_kernel(vdim // HEADS, sigmoid_gate, with_router),
        grid=(t // tm,), in_specs=in_specs, out_specs=out_specs, out_shape=out_shape,
        scratch_shapes=[pltpu.VMEM((tm, vdim), BF16)],
        compiler_params=_cparams(("arbitrary",), 40),
        name="mixer_out",
    )(*args)


def _ffn_kernel(x_ref, g2_ref, fsh_ref, fsc_ref, wg_ref, wu_ref, wd_ref, g3_ref, fga_ref, o_ref):
    x = x_ref[...]
    hb = _adaln(x, g2_ref[...], fsh_ref[...], fsc_ref[...]).astype(BF16)
    a = (_silu(_dot(hb, wg_ref[...])) * _dot(hb, wu_ref[...])).astype(BF16)
    y = _dot(a, wd_ref[...])
    o_ref[...] = x + fga_ref[...] * _rms(y, g3_ref[...])


def _ffn(x, mods6, layer, first_row, rows_per_batch, g2, g3, wg, wu, wd):
    t, d = x.shape
    tm = ROW_TILE
    ms = lambda comp: _mod_spec(d, layer, comp, rows_per_batch, first_row)
    return pl.pallas_call(
        _ffn_kernel,
        grid=(t // tm,),
        in_specs=[pl.BlockSpec((tm, d), lambda i: (i, 0)), _const_spec((1, d)), ms(3), ms(4),
                  _const_spec(wg.shape), _const_spec(wu.shape), _const_spec(wd.shape),
                  _const_spec((1, d)), ms(5)],
        out_specs=pl.BlockSpec((tm, d), lambda i: (i, 0)),
        out_shape=jax.ShapeDtypeStruct((t, d), F32),
        compiler_params=_cparams(("arbitrary",), 48),
        name="ffn_dense",
    )(x, g2, mods6, mods6, wg, wu, wd, g3, mods6)


def _router_kernel(lg_ref, lrow_ref, gt_ref, tab_ref, tot_ref, carry):
    tb = lg_ref.shape[0]

    @pl.when(pl.program_id(0) == 0)
    def _():
        carry[...] = jnp.zeros_like(carry)

    lane = lax.broadcasted_iota(jnp.int32, (tb, LANES), 1)
    lg = jnp.where(lane < N_EXPERTS, lg_ref[...], -jnp.inf)
    v1 = jnp.max(lg, axis=1, keepdims=True)
    i1 = jnp.min(jnp.where(lg == v1, lane, LANES), axis=1, keepdims=True)
    lg2 = jnp.where(lane == i1, -jnp.inf, lg)
    v2 = jnp.max(lg2, axis=1, keepdims=True)
    i2 = jnp.min(jnp.where(lg2 == v2, lane, LANES), axis=1, keepdims=True)
    ex = jnp.exp(v2 - v1)
    den = 1.0 + ex
    g1 = 1.0 / den
    g2 = ex / den
    oh1 = lane == i1
    oh2 = lane == i2
    onehot = jnp.where(oh1 | oh2, 1.0, 0.0)
    ri = lax.broadcasted_iota(jnp.int32, (tb, tb), 0)
    ci = lax.broadcasted_iota(jnp.int32, (tb, tb), 1)
    before = jnp.where(ri > ci, 1.0, 0.0).astype(BF16)
    rank = _dot(before, onehot.astype(BF16))
    tiles = jnp.ceil(jnp.sum(onehot, axis=0, keepdims=True) * (1.0 / SUBLANES))
    ei = lax.broadcasted_iota(jnp.int32, (LANES, LANES), 0)
    ej = lax.broadcasted_iota(jnp.int32, (LANES, LANES), 1)
    earlier = jnp.where(ei < ej, 1.0, 0.0).astype(BF16)
    off = _dot(jnp.broadcast_to(tiles, (SUBLANES, LANES)).astype(BF16), earlier)[0:1] * SUBLANES
    pos = rank + off
    r1 = jnp.sum(jnp.where(oh1, pos, 0.0), axis=1, keepdims=True)
    r2 = jnp.sum(jnp.where(oh2, pos, 0.0), axis=1, keepdims=True)
    lrow_ref[...] = jnp.where(lane == 0, r1, jnp.where(lane == 1, r2, 0.0)).astype(jnp.int32)
    gt_ref[...] = jnp.where(lane == 0, g1, jnp.where(lane == 1, g2, 0.0))
    sub = lax.broadcasted_iota(jnp.int32, (SUBLANES, LANES), 0)
    tab = jnp.where(sub == 0, tiles, jnp.where(sub == 1, off, jnp.where(sub == 2, carry[...], 0.0)))
    tab_ref[...] = tab.astype(jnp.int32)
    carry[...] = carry[...] + tiles * SUBLANES
    tot_ref[...] = carry[...].astype(jnp.int32)


def _router(logits):
    t = logits.shape[0]
    tb = ROW_TILE
    blk = pl.BlockSpec((tb, LANES), lambda i: (i, 0))
    return pl.pallas_call(
        _router_kernel,
        grid=(t // tb,),
        in_specs=[blk],
        out_specs=[blk, blk, pl.BlockSpec((SUBLANES, LANES), lambda i: (i, 0)),
                   pl.BlockSpec((1, LANES), lambda i: (0, 0))],
        out_shape=[jax.ShapeDtypeStruct((t, LANES), jnp.int32), jax.ShapeDtypeStruct((t, LANES), F32),
                   jax.ShapeDtypeStruct((t // tb * SUBLANES, LANES), jnp.int32),
                   jax.ShapeDtypeStruct((1, LANES), jnp.int32)],
        scratch_shapes=[pltpu.VMEM((1, LANES), F32)],
        compiler_params=_cparams(("arbitrary",), 16),
        name="moe_router",
    )(logits)


def _tile_copies(nt_ref, lo_ref, gd_ref, blk, make_copy, wait):
    for e in range(N_EXPERTS):
        idx = blk * N_EXPERTS + e
        lo, gd = lo_ref[idx], gd_ref[idx]

        def one(j, _):
            cp = make_copy(pl.multiple_of(lo + j * SUBLANES, SUBLANES), pl.multiple_of(gd + j * SUBLANES, SUBLANES))
            if wait:
                cp.wait()
            else:
                cp.start()
            return 0

        lax.fori_loop(0, nt_ref[idx], one, 0)


def _make_dispatch_kernel(tb, n_blocks):
    def kernel(ps_ref, nb_ref, nt_ref, lo_ref, gd_ref, lrow_ref, h_ref, xb_ref, xs, zbuf, sem):
        i = pl.program_id(0)

        @pl.when(i == 0)
        def _():
            zbuf[...] = jnp.zeros_like(zbuf)

            def zero_block(row0):
                dst = pl.multiple_of(row0, MOE_BLOCK)
                cp = pltpu.make_async_copy(zbuf, xb_ref.at[pl.ds(dst, MOE_BLOCK), :], sem.at[2])
                cp.start()
                cp.wait()

            for e in range(N_EXPERTS):
                @pl.when(nb_ref[e] > 0)
                def _():
                    zero_block(ps_ref[e] + (nb_ref[e] - 1) * MOE_BLOCK)

            used = ps_ref[N_EXPERTS - 1] // MOE_BLOCK + nb_ref[N_EXPERTS - 1]

            def tail(j, _):
                zero_block(j * MOE_BLOCK)
                return 0

            lax.fori_loop(used, n_blocks, tail, 0)

        slot = i % 2
        lr = lrow_ref[...]
        r_iota = lax.broadcasted_iota(jnp.int32, (tb, STAGE_ROWS), 1)
        sel = jnp.where((r_iota == lr[:, 0:1]) | (r_iota == lr[:, 1:2]), 1.0, 0.0).astype(BF16)
        xs[slot] = _dot_tn(sel, h_ref[...].astype(BF16))

        def copies_from(buf):
            def make_copy(lo, gd):
                return pltpu.make_async_copy(xs.at[buf, pl.ds(lo, SUBLANES), :], xb_ref.at[pl.ds(gd, SUBLANES), :],
                                             sem.at[buf])
            return make_copy

        _tile_copies(nt_ref, lo_ref, gd_ref, i, copies_from(slot), wait=False)

        @pl.when(i > 0)
        def _():
            _tile_copies(nt_ref, lo_ref, gd_ref, i - 1, copies_from(1 - slot), wait=True)

        @pl.when(i == pl.num_programs(0) - 1)
        def _():
            _tile_copies(nt_ref, lo_ref, gd_ref, i, copies_from(slot), wait=True)

    return kernel


def _dispatch(h2, lrow, pad_start, nblk, ntile, loff, gdest, n_blocks):
    t, d = h2.shape
    tb = ROW_TILE
    grid_spec = pltpu.PrefetchScalarGridSpec(
        num_scalar_prefetch=5, grid=(t // tb,),
        in_specs=[pl.BlockSpec((tb, LANES), lambda i, *_: (i, 0)),
                  pl.BlockSpec((tb, d), lambda i, *_: (i, 0))],
        out_specs=pl.BlockSpec(memory_space=pl.ANY),
        scratch_shapes=[pltpu.VMEM((2, STAGE_ROWS, d), F32), pltpu.VMEM((MOE_BLOCK, d), F32),
                        pltpu.SemaphoreType.DMA((3,))])
    return pl.pallas_call(
        _make_dispatch_kernel(tb, n_blocks), grid_spec=grid_spec,
        out_shape=jax.ShapeDtypeStruct((n_blocks * MOE_BLOCK, d), F32),
        compiler_params=_cparams(("arbitrary",), 24),
        name="moe_dispatch",
    )(pad_start, nblk, ntile, loff, gdest, lrow, h2)


def _expert_kernel(be_ref, nu_ref, xb_ref, wg_ref, wu_ref, wd_ref, yb_ref):
    i = pl.program_id(0)

    @pl.when(i < nu_ref[0])
    def _():
        xb = xb_ref[...].astype(BF16)
        a = (_silu(_dot(xb, wg_ref[...])) * _dot(xb, wu_ref[...])).astype(BF16)
        yb_ref[...] = _dot(a, wd_ref[...])

    @pl.when(i >= nu_ref[0])
    def _():
        yb_ref[...] = jnp.zeros_like(yb_ref)


def _experts(xb, block_e, n_used, wg, wu, wd):
    _, d, f = wg.shape
    n_blocks = xb.shape[0] // MOE_BLOCK
    blk = pl.BlockSpec((MOE_BLOCK, d), lambda i, be, nu: (i, 0))
    grid_spec = pltpu.PrefetchScalarGridSpec(
        num_scalar_prefetch=2, grid=(n_blocks,),
        in_specs=[blk,
                  pl.BlockSpec((None, d, f), lambda i, be, nu: (be[i], 0, 0)),
                  pl.BlockSpec((None, d, f), lambda i, be, nu: (be[i], 0, 0)),
                  pl.BlockSpec((None, f, d), lambda i, be, nu: (be[i], 0, 0))],
        out_specs=blk)
    return pl.pallas_call(
        _expert_kernel, grid_spec=grid_spec,
        out_shape=jax.ShapeDtypeStruct(xb.shape, F32),
        compiler_params=_cparams(("arbitrary",), 56),
        name="moe_experts",
    )(block_e, n_used, xb, wg, wu, wd)


def _make_combine_kernel(tb):
    def kernel(nt_ref, lo_ref, gd_ref, yb_ref, lrow_ref, gt_ref, x_ref, g3_ref, fga_ref, o_ref, ys, sem):
        i = pl.program_id(0)
        slot = i % 2

        def copies_into(buf):
            def make_copy(lo, gd):
                return pltpu.make_async_copy(yb_ref.at[pl.ds(gd, SUBLANES), :], ys.at[buf, pl.ds(lo, SUBLANES), :],
                                             sem.at[buf])
            return make_copy

        @pl.when(i == 0)
        def _():
            ys[...] = jnp.zeros_like(ys)
            _tile_copies(nt_ref, lo_ref, gd_ref, 0, copies_into(0), wait=False)

        @pl.when(i + 1 < pl.num_programs(0))
        def _():
            _tile_copies(nt_ref, lo_ref, gd_ref, i + 1, copies_into(1 - slot), wait=False)

        _tile_copies(nt_ref, lo_ref, gd_ref, i, copies_into(slot), wait=True)

        lr = lrow_ref[...]
        gt = gt_ref[...]
        r_iota = lax.broadcasted_iota(jnp.int32, (tb, STAGE_ROWS), 1)
        q = jnp.where(r_iota == lr[:, 0:1], gt[:, 0:1], 0.0) + jnp.where(r_iota == lr[:, 1:2], gt[:, 1:2], 0.0)
        qh, ql = _split_bf16(q)
        yh, yl = _split_bf16(ys[slot])
        f = _dot(qh, yh) + _dot(ql, yh) + _dot(qh, yl)
        o_ref[...] = x_ref[...] + fga_ref[...] * _rms(f, g3_ref[...])

    return kernel


def _combine(yb, lrow, ntile, loff, gdest, gates, x, mods6, layer, first_row, rows_per_batch, g3):
    t, d = x.shape
    tb = ROW_TILE
    if rows_per_batch is None:
        fga_map = lambda i, *_: (layer, first_row, 5, 0, 0)
    else:
        fga_map = lambda i, *_: (layer, first_row + i // rows_per_batch, 5, 0, 0)
    grid_spec = pltpu.PrefetchScalarGridSpec(
        num_scalar_prefetch=3, grid=(t // tb,),
        in_specs=[pl.BlockSpec(memory_space=pl.ANY),
                  pl.BlockSpec((tb, LANES), lambda i, *_: (i, 0)),
                  pl.BlockSpec((tb, LANES), lambda i, *_: (i, 0)),
                  pl.BlockSpec((tb, d), lambda i, *_: (i, 0)),
                  pl.BlockSpec((1, d), lambda i, *_: (0, 0)),
                  pl.BlockSpec((None, None, None, 1, d), fga_map)],
        out_specs=pl.BlockSpec((tb, d), lambda i, *_: (i, 0)),
        scratch_shapes=[pltpu.VMEM((2, STAGE_ROWS, d), F32), pltpu.SemaphoreType.DMA((2,))])
    return pl.pallas_call(
        _make_combine_kernel(tb), grid_spec=grid_spec,
        out_shape=jax.ShapeDtypeStruct((t, d), F32),
        compiler_params=_cparams(("arbitrary",), 32),
        name="moe_combine",
    )(ntile, loff, gdest, yb, lrow, gates, x, g3, mods6)


def _moe(h2, logits, x, mods6, layer, first_row, rows_per_batch, g3, wg, wu, wd):
    t = x.shape[0]
    n_tok_blocks = t // ROW_TILE
    max_rows = t * TOP_K + n_tok_blocks * N_EXPERTS * (SUBLANES - 1)
    n_blocks = -(-max_rows // MOE_BLOCK) + N_EXPERTS
    lrow, gates, tab, tot = _router(logits)
    tab = tab.reshape(n_tok_blocks, SUBLANES, LANES)[:, :, :N_EXPERTS]
    ntile, loff, prior = tab[:, 0], tab[:, 1], tab[:, 2]
    nblk = (tot[0, :N_EXPERTS] + MOE_BLOCK - 1) // MOE_BLOCK
    blk_end = jnp.cumsum(nblk)
    pad_start = ((blk_end - nblk) * MOE_BLOCK).astype(jnp.int32)
    gdest = (pad_start[None, :] + prior).astype(jnp.int32)
    n_used = blk_end[-1:].astype(jnp.int32)
    blk = jnp.minimum(jnp.arange(n_blocks, dtype=jnp.int32), n_used[0] - 1)
    block_e = jnp.minimum(jnp.sum(blk[:, None] >= blk_end[None, :], axis=1), N_EXPERTS - 1).astype(jnp.int32)
    ntile, loff, gdest = ntile.reshape(-1), loff.reshape(-1), gdest.reshape(-1)
    xb = _dispatch(h2, lrow, pad_start, nblk.astype(jnp.int32), ntile, loff, gdest, n_blocks)
    yb = _experts(xb, block_e, n_used, wg, wu, wd)
    return _combine(yb, lrow, ntile, loff, gdest, gates, x, mods6, layer, first_row, rows_per_batch, g3)


def _pad_cols(w, n):
    return jnp.pad(w, ((0, 0), (0, n - w.shape[1])))


def kernel(x_prompt, x_sample, state_mlstm_C, state_mlstm_n, state_mlstm_m, state_ret_S, c, c_ctx, mod_w, mod_b, norm_g, mlstm_w_in, mlstm_gate_b, mlstm_conv_w, mlstm_conv_b, mlstm_head_g, mlstm_w_out, ret_w_in, ret_decay_logit, ret_head_g, ret_w_out, ffn_w_gate, ffn_w_up, ffn_w_down, moe_router, moe_w_gate, moe_w_up, moe_w_down):
    bp, n_p, d = x_prompt.shape
    bs, n_s, _ = x_sample.shape
    depth = mod_w.shape[0]
    assert depth == 2 and CHUNK % GRID_W == 0 and n_s % GRID_W == 0

    cond = jnp.zeros((MOD_ROWS, d), F32).at[0].set(c_ctx).at[1:1 + bs].set(c)
    mods6 = _modulation(cond, mod_w, mod_b).reshape(depth, MOD_ROWS, N_MOD, 1, d)

    groups = [dict(x=x_prompt.reshape(bp * n_p, d), first=0, rpb=None, nseq=bp, ntok=n_p, prompt=True),
              dict(x=x_sample.reshape(bs * n_s, d), first=1, rpb=n_s // ROW_TILE, nseq=bs, ntok=n_s, prompt=False)]

    j = 0
    ml_qk = (mlstm_w_in.shape[2] - 4 * HEADS) // 2
    ml_v = ml_qk // 2
    w_in = mlstm_w_in[j]
    wqk = w_in[:, :ml_qk].astype(BF16)
    wv = w_in[:, ml_qk:ml_qk + ml_v].astype(BF16)
    wo = w_in[:, ml_qk + ml_v:ml_qk + 2 * ml_v].astype(BF16)
    w_gate = w_in[:, ml_qk + 2 * ml_v:]
    wgh, wgl = _split_bf16(_pad_cols(w_gate, LANES))
    wgth, wgtl = _split_bf16(w_gate.T)
    bcol = _pad_cols(mlstm_gate_b[j][None, :], LANES)
    brow = mlstm_gate_b[j][:, None]
    g = norm_g[0]
    w_out0 = mlstm_w_out[j].astype(BF16)
    fwg, fwu, fwd = ffn_w_gate[j].astype(BF16), ffn_w_up[j].astype(BF16), ffn_w_down[j].astype(BF16)
    new_c = new_n = new_m = None
    for grp in groups:
        args = (mods6, 0, grp["first"], grp["rpb"])
        q, kt, v, o, gcol, grow = _proj_mlstm(grp["x"], grp["ntok"], *args, g[0:1], wqk, wv, wo, wgh, wgl, wgth, wgtl,
                                              bcol, brow, mlstm_conv_w[j], mlstm_conv_b[j][None, :])
        grow3 = grow.reshape(grow.shape[0], 1, grow.shape[1])
        if grp["prompt"]:
            m0 = jnp.zeros((grp["nseq"] * 2 * HEADS,), F32)
            hs, new_c, new_n, new_m = _mlstm_scan(q, kt, v, gcol, grow3, m0, None, grp["nseq"], grp["ntok"], True)
        else:
            (hs,) = _mlstm_scan(q, kt, v, gcol, grow3, state_mlstm_m[:, j].reshape(-1),
                                (state_mlstm_C[:, j], state_mlstm_n[:, j]), grp["nseq"], grp["ntok"], False)
        (x1,) = _mixer_out(hs, o, 0, mlstm_head_g[j][None, :], w_out0, grp["x"], *args, g[1:2], True)
        grp["x"] = _ffn(x1, *args, g[2:3], g[3:4], fwg, fwu, fwd)

    ret_qk = ret_w_in.shape[2] // 3
    w_in = ret_w_in[j]
    rwqk = w_in[:, :ret_qk].astype(BF16)
    rwv = w_in[:, ret_qk:2 * ret_qk].astype(BF16)
    rwg = w_in[:, 2 * ret_qk:].astype(BF16)
    g = norm_g[1]
    w_out1 = ret_w_out[j].astype(BF16)
    wrh, wrl = _split_bf16(_pad_cols(moe_router[j], LANES))
    ewg, ewu, ewd = moe_w_gate[j].astype(BF16), moe_w_up[j].astype(BF16), moe_w_down[j].astype(BF16)
    decay_flat = ret_decay_logit[j].reshape(-1)
    new_s = None
    for grp in groups:
        args = (mods6, 1, grp["first"], grp["rpb"])
        qk, v, gate = _proj_ret(grp["x"], *args, g[0:1], rwqk, rwv, rwg)
        if grp["prompt"]:
            hs, new_s = _ret_scan(qk, v, decay_flat, None, None, grp["nseq"], grp["ntok"], True)
        else:
            (hs,) = _ret_scan(qk, v, decay_flat, _rope_tables(grp["ntok"], ret_qk // (2 * HEADS)), state_ret_S[:, j],
                              grp["nseq"], grp["ntok"], False)
        x1, h2, logits = _mixer_out(hs, gate, 0, ret_head_g[j][None, :], w_out1, grp["x"], *args, g[1:2], False,
                                    router=(g[2:3], wrh, wrl))
        grp["x"] = _moe(h2, logits, x1, *args, g[3:4], ewg, ewu, ewd)

    y_prompt = groups[0]["x"].reshape(bp, n_p, d)
    y_sample = groups[1]["x"].reshape(bs, n_s, d)
    return (y_prompt, y_sample, new_c[:, None], new_n[:, None], new_m[:, None, :, :, 0], new_s[:, None])
```

```python
import functools
import math

import jax
import jax.numpy as jnp
from jax import lax
from jax.experimental import pallas as pl
from jax.experimental.pallas import tpu as pltpu

F32 = jnp.float32
BF16 = jnp.bfloat16

EPS = 1e-6
N_MOD = 6
HEADS = 4
CHUNK = 128
GRID_W = 64
ROPE_BASE = 10000.0
N_EXPERTS = 8
TOP_K = 2
MOE_BLOCK = 256
LANES = 128
SUBLANES = 8
ROW_TILE = 256
MOD_ROWS = 8
STAGE_ROWS = -(-(TOP_K * ROW_TILE + N_EXPERTS * (SUBLANES - 1)) // LANES) * LANES
MIB = 1024 * 1024


def _cparams(sem, vmem_mib):
    return pltpu.CompilerParams(dimension_semantics=sem, vmem_limit_bytes=vmem_mib * MIB)


def _dot(a, b):
    return jnp.dot(a, b, preferred_element_type=F32)


def _dot_nt(a, b):
    return lax.dot_general(a, b, (((1,), (1,)), ((), ())), preferred_element_type=F32)


def _dot_tn(a, b):
    return lax.dot_general(a, b, (((0,), (0,)), ((), ())), preferred_element_type=F32)


def _split_bf16(x):
    hi = x.astype(BF16)
    lo = (x - hi.astype(F32)).astype(BF16)
    return hi, lo


def _rms(x, g):
    return x * lax.rsqrt(jnp.mean(x * x, -1, keepdims=True) + EPS) * g


def _adaln(x, g, shift, scale):
    return _rms(x, g) * (1.0 + scale) + shift


def _silu(x):
    return x * jax.nn.sigmoid(x)


def _logsig(x):
    return jnp.minimum(x, 0.0) - jnp.log1p(jnp.exp(-jnp.abs(x)))


def _const_spec(shape):
    nd = len(shape)
    return pl.BlockSpec(shape, lambda *_: (0,) * nd, pipeline_mode=pl.Buffered(1))


def _mod_spec(d, layer, comp, rows_per_batch, first_row):
    if rows_per_batch is None:
        return pl.BlockSpec((None, None, None, 1, d), lambda i: (layer, first_row, comp, 0, 0))
    return pl.BlockSpec((None, None, None, 1, d), lambda i: (layer, first_row + i // rows_per_batch, comp, 0, 0))


def _mod_kernel(c_ref, w_ref, b_ref, o_ref):
    s = _silu(c_ref[...]).astype(BF16)
    o_ref[...] = _dot(s, w_ref[...].astype(BF16)) + b_ref[...]


def _modulation(cond, mod_w, mod_b):
    depth, d, n = mod_w.shape
    tn = n // 4
    return pl.pallas_call(
        _mod_kernel,
        grid=(depth, n // tn),
        in_specs=[pl.BlockSpec((MOD_ROWS, d), lambda l, j: (0, 0)),
                  pl.BlockSpec((None, d, tn), lambda l, j: (l, 0, j)),
                  pl.BlockSpec((None, 1, tn), lambda l, j: (l, 0, j))],
        out_specs=pl.BlockSpec((None, MOD_ROWS, tn), lambda l, j: (l, 0, j)),
        out_shape=jax.ShapeDtypeStruct((depth, MOD_ROWS, n), F32),
        compiler_params=_cparams(("arbitrary", "arbitrary"), 40),
        name="modulation",
    )(cond, mod_w, mod_b.reshape(depth, 1, n))


def _proj_mlstm_kernel(tiles_per_seq, x_ref, xp_ref, xn_ref, g_ref, sh_ref, sc_ref, wqk_ref, wv_ref, wo_ref,
                       wgh_ref, wgl_ref, bcol_ref, cw_ref, cb_ref,
                       q_ref, kt_ref, v_ref, o_ref, gcol_ref, grow_ref):
    tm = x_ref.shape[0]
    nq = q_ref.shape[1]
    h = _adaln(x_ref[...], g_ref[...], sh_ref[...], sc_ref[...])
    hb, hl = _split_bf16(h)
    x_halo = jnp.concatenate([xp_ref[...], xn_ref[...]], axis=0)
    hb_halo = _adaln(x_halo, g_ref[...], sh_ref[...], sc_ref[...]).astype(BF16)
    hb_all = jnp.concatenate([hb, hb_halo], axis=0)
    pos = pl.program_id(0) % tiles_per_seq
    has_prev = jnp.where(pos > 0, 1.0, 0.0)
    has_next = jnp.where(pos < tiles_per_seq - 1, 1.0, 0.0)
    wc = nq // HEADS
    rowi = lax.broadcasted_iota(jnp.int32, (tm, wc), 0)
    for c in range(2 * HEADS):
        sl = slice(c * wc, (c + 1) * wc)
        p_all = _dot(hb_all, wqk_ref[:, sl])
        p = p_all[:tm]
        p_prev = p_all[tm + SUBLANES - 1:tm + SUBLANES, :] * has_prev
        p_next = p_all[tm + SUBLANES:tm + SUBLANES + 1, :] * has_next
        prev = jnp.where(rowi == 0, p_prev, pltpu.roll(p, 1, axis=0))
        nxt = jnp.where(rowi == tm - 1, p_next, pltpu.roll(p, tm - 1, axis=0))
        y = prev * cw_ref[0:1, sl] + p * cw_ref[1:2, sl] + nxt * cw_ref[2:3, sl] + cb_ref[:, sl]
        if c < HEADS:
            q_ref[:, sl] = (y * wc ** -0.5).astype(BF16)
        else:
            kt_ref[(c - HEADS) * wc:(c - HEADS + 1) * wc, :] = y.T.astype(BF16)
    v_ref[...] = _dot(hb, wv_ref[...]).astype(BF16)
    o_ref[...] = _dot(hb, wo_ref[...])
    gc = _dot(hb, wgh_ref[...]) + _dot(hl, wgh_ref[...]) + _dot(hb, wgl_ref[...]) + bcol_ref[...]
    for hh in range(HEADS):
        gcol_ref[hh] = gc if hh == 0 else pltpu.roll(gc, LANES - hh, axis=1)
    grow_ref[...] = gc.T[:grow_ref.shape[0], :]


def _proj_mlstm(x, n_tok, mods6, layer, first_row, rows_per_batch, g, wqk, wv, wo, wgh, wgl, bcol, ng, conv_w, conv_b):
    t, d = x.shape
    tm = ROW_TILE
    ms = lambda comp: _mod_spec(d, layer, comp, rows_per_batch, first_row)
    nq = wqk.shape[1] // 2
    tps = tm // SUBLANES
    last = t // SUBLANES - 1
    return pl.pallas_call(
        functools.partial(_proj_mlstm_kernel, n_tok // tm),
        grid=(t // tm,),
        in_specs=[pl.BlockSpec((tm, d), lambda i: (i, 0)),
                  pl.BlockSpec((SUBLANES, d), lambda i: (jnp.maximum(i * tps - 1, 0), 0)),
                  pl.BlockSpec((SUBLANES, d), lambda i: (jnp.minimum((i + 1) * tps, last), 0)),
                  _const_spec((1, d)), ms(0), ms(1),
                  _const_spec(wqk.shape), _const_spec(wv.shape), _const_spec(wo.shape),
                  _const_spec(wgh.shape), _const_spec(wgl.shape), _const_spec(bcol.shape),
                  _const_spec(conv_w.shape), _const_spec(conv_b.shape)],
        out_specs=[pl.BlockSpec((tm, nq), lambda i: (i, 0)),
                   pl.BlockSpec((nq, tm), lambda i: (0, i)),
                   pl.BlockSpec((tm, wv.shape[1]), lambda i: (i, 0)),
                   pl.BlockSpec((tm, wo.shape[1]), lambda i: (i, 0)),
                   pl.BlockSpec((HEADS, tm, LANES), lambda i: (0, i, 0)),
                   pl.BlockSpec((ng, tm), lambda i: (0, i))],
        out_shape=[jax.ShapeDtypeStruct((t, nq), BF16),
                   jax.ShapeDtypeStruct((nq, t), BF16),
                   jax.ShapeDtypeStruct((t, wv.shape[1]), BF16),
                   jax.ShapeDtypeStruct((t, wo.shape[1]), F32),
                   jax.ShapeDtypeStruct((HEADS, t, LANES), F32),
                   jax.ShapeDtypeStruct((ng, t), F32)],
        compiler_params=_cparams(("arbitrary",), 52),
        name="proj_mlstm",
    )(x, x, x, g, mods6, mods6, wqk, wv, wo, wgh, wgl, bcol, conv_w, conv_b)


def _make_mlstm_scan_kernel(n_tok, dk, dv, has_state, emit_state):
    L = CHUNK
    nc = n_tok // L
    assert nc % 2 == 0 and L == LANES

    def kernel(*refs):
        it = iter(refs)
        m0_ref = next(it)
        q_ref, kt_ref, v_ref, gcol_ref = next(it), next(it), next(it), next(it)
        gi_refs = (next(it), next(it))
        if has_state:
            c0_ref, n0_ref = next(it), next(it)
        out_ref = next(it)
        if emit_state:
            cout_ref, nout_ref, mout_ref = next(it), next(it), next(it)
        brep, rmrep, rrow, stats, cst, cbf, nrep, nbf = (next(it) for _ in range(8))

        b = pl.program_id(0)
        hh = pl.program_id(1)

        ri = lax.broadcasted_iota(jnp.int32, (L, L), 0)
        ci = lax.broadcasted_iota(jnp.int32, (L, L), 1)
        lower = ri >= ci
        upper = ri <= ci
        masks = ((lower, upper), (upper, lower))

        tri = tuple(jnp.where(masks[d][0], 1.0, 0.0).astype(BF16) for d in range(2))
        rowid = lax.broadcasted_iota(jnp.int32, (L, LANES), 0)

        def running_max(x, reverse):
            s = 1
            while s < L:
                if reverse:
                    x = jnp.maximum(x, jnp.where(rowid < L - s, pltpu.roll(x, L - s, axis=0), -jnp.inf))
                else:
                    x = jnp.maximum(x, jnp.where(rowid >= s, pltpu.roll(x, s, axis=0), -jnp.inf))
                s *= 2
            return x

        def prep_chunk(c, _):
            r0 = pl.multiple_of(c * L, L)
            gc = gcol_ref[pl.ds(r0, L), :]
            for d in range(2):
                mk, mkt = masks[d]
                fr = jnp.broadcast_to(_logsig(gc[:, 8 * d + 4:8 * d + 5]), (L, LANES))
                ir = jnp.broadcast_to(gc[:, 8 * d:8 * d + 1], (L, LANES))
                f1 = fr.astype(BF16)
                e1 = fr - f1.astype(F32)
                f2 = e1.astype(BF16)
                f3 = (e1 - f2.astype(F32)).astype(BF16)
                b_rep = _dot(tri[d], f1) + _dot(tri[d], f2) + _dot(tri[d], f3)
                rm_rep = running_max(ir - b_rep, reverse=(d == 1))
                b_row = jnp.sum(jnp.where(mkt, fr, 0.0), axis=0, keepdims=True)
                brep[d, pl.ds(r0, L), :] = b_rep
                rmrep[d, pl.ds(r0, L), :] = rm_rep
                rrow[d, :, pl.ds(r0, L)] = gi_refs[d][:, pl.ds(r0, L)] - b_row
                end = 0 if d == 1 else L - 1
                stats[c, pl.ds(2 * d, 1), :] = b_rep[end:end + 1, :]
                stats[c, pl.ds(2 * d + 1, 1), :] = rm_rep[end:end + 1, :]
            return 0

        lax.fori_loop(0, nc, prep_chunk, 0, unroll=2)

        if has_state:
            cst[...] = c0_ref[...]
            for d in range(2):
                nrep[d] = jnp.broadcast_to(n0_ref[d], (dk, LANES))
        else:
            cst[...] = jnp.zeros_like(cst)
            nrep[...] = jnp.zeros_like(nrep)
        cbf[...] = cst[...].astype(BF16)
        nbf[...] = nrep[...].astype(BF16)
        m_init = tuple(jnp.full((1, LANES), m0_ref[(b * 2 + d) * HEADS + hh], F32) for d in range(2))

        def lanes(x, n):
            return jnp.concatenate([x] * (n // LANES), axis=1)

        def chunk(d, c, m):
            mk = masks[d][0]
            r0 = pl.multiple_of(c * L, L)
            q = q_ref[pl.ds(r0, L), :]
            kt = kt_ref[:, pl.ds(r0, L)]
            v = v_ref[pl.ds(r0, L), :]
            rr = rrow[d, :, pl.ds(r0, L)]
            st = stats[c]
            b_end, rm_end = st[2 * d:2 * d + 1, :], st[2 * d + 1:2 * d + 2, :]
            mm = jnp.maximum(m, rmrep[d, pl.ds(r0, L), :])
            qkn = _dot(q, jnp.concatenate([kt, nbf[d]], axis=1))
            s = qkn[:, :L] * jnp.where(mk, jnp.exp(rr - mm), 0.0)
            w_inter = jnp.exp(m - mm)
            num = lanes(w_inter, dv) * _dot(q, cbf[d]) + _dot(s.astype(BF16), v)
            den = w_inter * qkn[:, L:] + jnp.sum(s, axis=1, keepdims=True)
            inv = 1.0 / jnp.maximum(jnp.abs(den), jnp.exp(-(brep[d, pl.ds(r0, L), :] + mm)))
            h = num * lanes(inv, dv)
            m_end = jnp.maximum(m, rm_end)
            decay = jnp.exp(m - m_end)
            kw = kt.astype(F32) * jnp.exp(rr - m_end)
            c_new = lanes(decay, dv) * cst[d] + _dot(kw.astype(BF16), v)
            n_new = decay * nrep[d] + jnp.sum(kw, axis=1, keepdims=True)
            cst[d] = c_new
            cbf[d] = c_new.astype(BF16)
            nrep[d] = n_new
            nbf[d] = n_new.astype(BF16)
            return h, b_end + m_end

        def make_body(accumulate):
            def body(i, carry):
                mf, mb = carry
                hf, mf = chunk(0, i, mf)
                rf = pl.multiple_of(i * L, L)
                cb = nc - 1 - i
                hb, mb = chunk(1, cb, mb)
                rb = pl.multiple_of(cb * L, L)
                if accumulate:
                    out_ref[pl.ds(rf, L), :] += hf
                    out_ref[pl.ds(rb, L), :] += hb
                else:
                    out_ref[pl.ds(rf, L), :] = hf
                    out_ref[pl.ds(rb, L), :] = hb
                return mf, mb
            return body

        unroll = 2 if nc % 4 == 0 else 1
        carry = lax.fori_loop(0, nc // 2, make_body(False), m_init, unroll=unroll)
        carry = lax.fori_loop(nc // 2, nc, make_body(True), carry, unroll=unroll)

        if emit_state:
            cout_ref[...] = cst[...]
            for d in range(2):
                nout_ref[d, pl.ds(hh, 1), :] = nrep[d].T[0:1, :]
                mout_ref[d, pl.ds(hh, 1), :] = carry[d]

    return kernel


def _mlstm_scan(q, kt, v, gcol, grow3, m0_flat, state, n_seq, n_tok, emit_state):
    dk = q.shape[1] // HEADS
    dv = v.shape[1] // HEADS
    has_state = state is not None
    kern = _make_mlstm_scan_kernel(n_tok, dk, dv, has_state, emit_state)
    grow_spec = lambda r: pl.BlockSpec((None, 1, n_tok), lambda b, h, m: (r * HEADS + h, 0, b))
    in_specs = [pl.BlockSpec((n_tok, dk), lambda b, h, m: (b, h)),
                pl.BlockSpec((dk, n_tok), lambda b, h, m: (h, b)),
                pl.BlockSpec((n_tok, dv), lambda b, h, m: (b, h)),
                pl.BlockSpec((None, n_tok, LANES), lambda b, h, m: (h, b, 0)),
                grow_spec(0), grow_spec(2)]
    args = [q, kt, v, gcol, grow3, grow3]
    if has_state:
        in_specs += [pl.BlockSpec((None, 2, None, dk, dv), lambda b, h, m: (b, 0, h, 0, 0)),
                     pl.BlockSpec((None, 2, None, dk, 1), lambda b, h, m: (b, 0, h, 0, 0))]
        args += [state[0], state[1][..., None]]
    out_specs = [pl.BlockSpec((n_tok, dv), lambda b, h, m: (b, h))]
    out_shape = [jax.ShapeDtypeStruct((n_seq * n_tok, HEADS * dv), F32)]
    if emit_state:
        out_specs += [pl.BlockSpec((None, 2, None, dk, dv), lambda b, h, m: (b, 0, h, 0, 0)),
                      pl.BlockSpec((None, 2, HEADS, dk), lambda b, h, m: (b, 0, 0, 0)),
                      pl.BlockSpec((None, 2, HEADS, LANES), lambda b, h, m: (b, 0, 0, 0))]
        out_shape += [jax.ShapeDtypeStruct((n_seq, 2, HEADS, dk, dv), F32),
                      jax.ShapeDtypeStruct((n_seq, 2, HEADS, dk), F32),
                      jax.ShapeDtypeStruct((n_seq, 2, HEADS, LANES), F32)]
    grid_spec = pltpu.PrefetchScalarGridSpec(
        num_scalar_prefetch=1, grid=(n_seq, HEADS), in_specs=in_specs, out_specs=out_specs,
        scratch_shapes=[pltpu.VMEM((2, n_tok, LANES), F32),
                        pltpu.VMEM((2, n_tok, LANES), F32),
                        pltpu.VMEM((2, 1, n_tok), F32),
                        pltpu.VMEM((n_tok // CHUNK, SUBLANES, LANES), F32),
                        pltpu.VMEM((2, dk, dv), F32), pltpu.VMEM((2, dk, dv), BF16),
                        pltpu.VMEM((2, dk, LANES), F32), pltpu.VMEM((2, dk, LANES), BF16)])
    return pl.pallas_call(
        kern, grid_spec=grid_spec, out_shape=out_shape,
        compiler_params=_cparams(("arbitrary", "arbitrary"), 56),
        name="mlstm_scan",
    )(m0_flat, *args)


def _proj_ret_kernel(rope, *refs):
    it = iter(refs)
    x_ref, g_ref, sh_ref, sc_ref, wqk_ref, wv_ref, wg_ref = (next(it) for _ in range(7))
    if rope:
        rcos_ref, rsin_ref, ccos_ref, csin_ref = (next(it) for _ in range(4))
    q_ref, kt_ref, v_ref, gate_ref = (next(it) for _ in range(4))
    nq = q_ref.shape[1]
    wc = nq // HEADS
    hb = _adaln(x_ref[...], g_ref[...], sh_ref[...], sc_ref[...]).astype(BF16)
    if rope:
        cos = jnp.concatenate([rcos_ref[...], ccos_ref[...]], axis=1)
        sin = jnp.concatenate([rsin_ref[...], csin_ref[...]], axis=1)
    for c in range(2 * HEADS):
        p = _dot(hb, wqk_ref[:, c * wc:(c + 1) * wc])
        if rope:
            swapped = jnp.concatenate([pltpu.roll(p[:, j * LANES:(j + 1) * LANES], LANES // 2, axis=1)
                                       for j in range(wc // LANES)], axis=1)
            p = p * cos + swapped * sin
        if c < HEADS:
            q_ref[:, c * wc:(c + 1) * wc] = p.astype(BF16)
        else:
            kt_ref[(c - HEADS) * wc:(c - HEADS + 1) * wc, :] = (p * wc ** -0.5).T.astype(BF16)
    v_ref[...] = _dot(hb, wv_ref[...]).astype(BF16)
    gate_ref[...] = _dot(hb, wg_ref[...])


def _proj_ret(x, n_tok, mods6, layer, first_row, rows_per_batch, g, wqk, wv, wg, rope_tabs):
    t, d = x.shape
    tm = ROW_TILE
    ms = lambda comp: _mod_spec(d, layer, comp, rows_per_batch, first_row)
    nq = wqk.shape[1] // 2
    rope = rope_tabs is not None
    in_specs = [pl.BlockSpec((tm, d), lambda i: (i, 0)), _const_spec((1, d)), ms(0), ms(1),
                _const_spec(wqk.shape), _const_spec(wv.shape), _const_spec(wg.shape)]
    args = [x, g, mods6, mods6, wqk, wv, wg]
    if rope:
        tiles_per_seq = n_tok // tm
        row_spec = pl.BlockSpec((tm, LANES), lambda i: (i % tiles_per_seq, 0))
        in_specs += [row_spec, row_spec, _const_spec((tm, LANES)), _const_spec((tm, LANES))]
        args += list(rope_tabs)
    return pl.pallas_call(
        functools.partial(_proj_ret_kernel, rope),
        grid=(t // tm,),
        in_specs=in_specs,
        out_specs=[pl.BlockSpec((tm, nq), lambda i: (i, 0)),
                   pl.BlockSpec((nq, tm), lambda i: (0, i)),
                   pl.BlockSpec((tm, wv.shape[1]), lambda i: (i, 0)),
                   pl.BlockSpec((tm, wg.shape[1]), lambda i: (i, 0))],
        out_shape=[jax.ShapeDtypeStruct((t, nq), BF16),
                   jax.ShapeDtypeStruct((nq, t), BF16),
                   jax.ShapeDtypeStruct((t, wv.shape[1]), BF16),
                   jax.ShapeDtypeStruct((t, wg.shape[1]), F32)],
        compiler_params=_cparams(("arbitrary",), 48),
        name="proj_ret",
    )(*args)


def _make_ret_scan_kernel(n_tok, dk, dv, has_state, emit_state):
    L = CHUNK
    nc = n_tok // L
    assert nc % 2 == 0 and L == LANES

    def kernel(*refs):
        it = iter(refs)
        dl_ref = next(it)
        q_ref, kt_ref, v_ref = next(it), next(it), next(it)
        if has_state:
            s0_ref = next(it)
        out_ref = next(it)
        if emit_state:
            sout_ref = next(it)
        sst, sbf = next(it), next(it)

        hh = pl.program_id(1)
        ri = lax.broadcasted_iota(jnp.int32, (L, L), 0)
        ci = lax.broadcasted_iota(jnp.int32, (L, L), 1)
        rel = (ri - ci).astype(F32)
        pos_col = ri.astype(F32)
        pos_row = ci[0:1, :].astype(F32)

        dmat, q_dec, k_dec, c_dec = [], [], [], []
        for d in range(2):
            lg = _logsig(jnp.full((1, LANES), dl_ref[d * HEADS + hh], F32))
            if d == 0:
                dmat.append(jnp.where(ri >= ci, jnp.exp(lg * jnp.maximum(rel, 0.0)), 0.0))
                q_dec.append(jnp.exp(lg * (pos_col + 1.0)))
                k_dec.append(jnp.exp(lg * (L - 1.0 - pos_row)))
            else:
                dmat.append(jnp.where(ri <= ci, jnp.exp(lg * jnp.maximum(-rel, 0.0)), 0.0))
                q_dec.append(jnp.exp(lg * (L - pos_col)))
                k_dec.append(jnp.exp(lg * pos_row))
            c_dec.append(jnp.exp(lg * float(L)))

        if has_state:
            sst[...] = s0_ref[...]
        else:
            sst[...] = jnp.zeros_like(sst)
        sbf[...] = sst[...].astype(BF16)

        def lanes(x, n):
            return jnp.concatenate([x] * (n // LANES), axis=1)

        def chunk(d, c):
            r0 = pl.multiple_of(c * L, L)
            q = q_ref[pl.ds(r0, L), :]
            kt = kt_ref[:, pl.ds(r0, L)]
            v = v_ref[pl.ds(r0, L), :]
            s = _dot(q, kt) * dmat[d]
            out = _dot(s.astype(BF16), v) + lanes(q_dec[d], dv) * _dot(q, sbf[d])
            s_new = lanes(c_dec[d], dv) * sst[d] + _dot((kt.astype(F32) * k_dec[d]).astype(BF16), v)
            sst[d] = s_new
            sbf[d] = s_new.astype(BF16)
            return out

        def body1(i, _):
            rf = pl.multiple_of(i * L, L)
            out_ref[pl.ds(rf, L), :] = chunk(0, i)
            cb = nc - 1 - i
            rb = pl.multiple_of(cb * L, L)
            out_ref[pl.ds(rb, L), :] = chunk(1, cb)
            return 0

        def body2(i, _):
            rf = pl.multiple_of(i * L, L)
            out_ref[pl.ds(rf, L), :] += chunk(0, i)
            cb = nc - 1 - i
            rb = pl.multiple_of(cb * L, L)
            out_ref[pl.ds(rb, L), :] += chunk(1, cb)
            return 0

        unroll = 2 if nc % 4 == 0 else 1
        lax.fori_loop(0, nc // 2, body1, 0, unroll=unroll)
        lax.fori_loop(nc // 2, nc, body2, 0, unroll=unroll)
        if emit_state:
            sout_ref[...] = sst[...]

    return kernel


def _ret_scan(q, kt, v, decay_flat, state, n_seq, n_tok, emit_state):
    dk = q.shape[1] // HEADS
    dv = v.shape[1] // HEADS
    has_state = state is not None
    kern = _make_ret_scan_kernel(n_tok, dk, dv, has_state, emit_state)
    in_specs = [pl.BlockSpec((n_tok, dk), lambda b, h, m: (b, h)),
                pl.BlockSpec((dk, n_tok), lambda b, h, m: (h, b)),
                pl.BlockSpec((n_tok, dv), lambda b, h, m: (b, h))]
    args = [q, kt, v]
    if has_state:
        in_specs.append(pl.BlockSpec((None, 2, None, dk, dv), lambda b, h, m: (b, 0, h, 0, 0)))
        args.append(state)
    out_specs = [pl.BlockSpec((n_tok, dv), lambda b, h, m: (b, h))]
    out_shape = [jax.ShapeDtypeStruct((n_seq * n_tok, HEADS * dv), F32)]
    if emit_state:
        out_specs.append(pl.BlockSpec((None, 2, None, dk, dv), lambda b, h, m: (b, 0, h, 0, 0)))
        out_shape.append(jax.ShapeDtypeStruct((n_seq, 2, HEADS, dk, dv), F32))
    grid_spec = pltpu.PrefetchScalarGridSpec(
        num_scalar_prefetch=1, grid=(n_seq, HEADS), in_specs=in_specs, out_specs=out_specs,
        scratch_shapes=[pltpu.VMEM((2, dk, dv), F32), pltpu.VMEM((2, dk, dv), BF16)])
    return pl.pallas_call(
        kern, grid_spec=grid_spec, out_shape=out_shape,
        compiler_params=_cparams(("arbitrary", "arbitrary"), 56),
        name="ret_scan",
    )(decay_flat, *args)


def _rope_tables(n_tok, dk):
    r = dk // 4
    inv = 1.0 / (ROPE_BASE ** (jnp.arange(r, dtype=F32) / r))
    sign = jnp.concatenate([-jnp.ones((r,), F32), jnp.ones((r,), F32)])
    rows = (jnp.arange(n_tok) // GRID_W).astype(F32)[:, None] * inv
    cols = (jnp.arange(ROW_TILE) % GRID_W).astype(F32)[:, None] * inv
    two = lambda a: jnp.concatenate([a, a], axis=-1)
    return (two(jnp.cos(rows)), two(jnp.sin(rows)) * sign, two(jnp.cos(cols)), two(jnp.sin(cols)) * sign)


def _make_mixer_out_kernel(dv, sigmoid_gate, with_router, with_ffn):
    def kernel(*refs):
        it = iter(refs)
        hs_ref, gate_ref, hg_ref, w_ref, x_ref, g1_ref, ga_ref = (next(it) for _ in range(7))
        if with_router:
            g2_ref, fsh_ref, fsc_ref, wrh_ref, wrl_ref = (next(it) for _ in range(5))
        if with_ffn:
            g2_ref, fsh_ref, fsc_ref, wg_ref, wu_ref, wd_ref, g3_ref, fga_ref = (next(it) for _ in range(8))
        xo_ref = next(it)
        if with_router:
            h2_ref, lg_ref = next(it), next(it)
        z_ref = next(it)
        for hh in range(HEADS):
            sl = slice(hh * dv, (hh + 1) * dv)
            seg = hs_ref[:, sl]
            y = seg * lax.rsqrt(jnp.mean(seg * seg, -1, keepdims=True) + EPS) * hg_ref[:, sl]
            gt = gate_ref[:, sl]
            act = jax.nn.sigmoid(gt) if sigmoid_gate else _silu(gt)
            z_ref[:, sl] = (act * y).astype(BF16)
        y = _dot(z_ref[...], w_ref[...])
        xn = x_ref[...] + ga_ref[...] * _rms(y, g1_ref[...])
        if with_ffn:
            hb = _adaln(xn, g2_ref[...], fsh_ref[...], fsc_ref[...]).astype(BF16)
            a = (_silu(_dot(hb, wg_ref[...])) * _dot(hb, wu_ref[...])).astype(BF16)
            xn = xn + fga_ref[...] * _rms(_dot(a, wd_ref[...]), g3_ref[...])
        xo_ref[...] = xn
        if with_router:
            h2 = _adaln(xn, g2_ref[...], fsh_ref[...], fsc_ref[...])
            h2_ref[...] = h2
            hb, hl = _split_bf16(h2)
            lg_ref[...] = _dot(hb, wrh_ref[...]) + _dot(hl, wrh_ref[...]) + _dot(hb, wrl_ref[...])
    return kernel


def _mixer_out(hs, gate, head_g, w_out, x, mods6, layer, first_row, rows_per_batch, g1, sigmoid_gate,
               router=None, ffn=None):
    t, d = x.shape
    vdim = hs.shape[1]
    tm = ROW_TILE
    ms = lambda comp: _mod_spec(d, layer, comp, rows_per_batch, first_row)
    with_router = router is not None
    with_ffn = ffn is not None
    in_specs = [pl.BlockSpec((tm, vdim), lambda i: (i, 0)),
                pl.BlockSpec((tm, vdim), lambda i: (i, 0)),
                _const_spec((1, vdim)), _const_spec(w_out.shape),
                pl.BlockSpec((tm, d), lambda i: (i, 0)), _const_spec((1, d)), ms(2)]
    args = [hs, gate, head_g, w_out, x, g1, mods6]
    out_specs = [pl.BlockSpec((tm, d), lambda i: (i, 0))]
    out_shape = [jax.ShapeDtypeStruct((t, d), F32)]
    if with_router:
        g2, wrh, wrl = router
        in_specs += [_const_spec((1, d)), ms(3), ms(4), _const_spec(wrh.shape), _const_spec(wrl.shape)]
        args += [g2, mods6, mods6, wrh, wrl]
        out_specs += [pl.BlockSpec((tm, d), lambda i: (i, 0)), pl.BlockSpec((tm, LANES), lambda i: (i, 0))]
        out_shape += [jax.ShapeDtypeStruct((t, d), F32), jax.ShapeDtypeStruct((t, LANES), F32)]
    if with_ffn:
        g2, g3, wg, wu, wd = ffn
        in_specs += [_const_spec((1, d)), ms(3), ms(4), _const_spec(wg.shape), _const_spec(wu.shape),
                     _const_spec(wd.shape), _const_spec((1, d)), ms(5)]
        args += [g2, mods6, mods6, wg, wu, wd, g3, mods6]
    return pl.pallas_call(
        _make_mixer_out_kernel(vdim // HEADS, sigmoid_gate, with_router, with_ffn),
        grid=(t // tm,), in_specs=in_specs, out_specs=out_specs, out_shape=out_shape,
        scratch_shapes=[pltpu.VMEM((tm, vdim), BF16)],
        compiler_params=_cparams(("arbitrary",), 48),
        name="mixer_ffn" if with_ffn else "mixer_out",
    )(*args)


def _router_kernel(lg_ref, lrow_ref, gt_ref, tab_ref, tot_ref, carry):
    tb = lg_ref.shape[0]

    @pl.when(pl.program_id(0) == 0)
    def _():
        carry[...] = jnp.zeros_like(carry)

    lane = lax.broadcasted_iota(jnp.int32, (tb, LANES), 1)
    lg = jnp.where(lane < N_EXPERTS, lg_ref[...], -jnp.inf)
    v1 = jnp.max(lg, axis=1, keepdims=True)
    i1 = jnp.min(jnp.where(lg == v1, lane, LANES), axis=1, keepdims=True)
    lg2 = jnp.where(lane == i1, -jnp.inf, lg)
    v2 = jnp.max(lg2, axis=1, keepdims=True)
    i2 = jnp.min(jnp.where(lg2 == v2, lane, LANES), axis=1, keepdims=True)
    ex = jnp.exp(v2 - v1)
    den = 1.0 + ex
    g1 = 1.0 / den
    g2 = ex / den
    oh1 = lane == i1
    oh2 = lane == i2
    onehot = jnp.where(oh1 | oh2, 1.0, 0.0)
    ri = lax.broadcasted_iota(jnp.int32, (tb, tb), 0)
    ci = lax.broadcasted_iota(jnp.int32, (tb, tb), 1)
    before = jnp.where(ri > ci, 1.0, 0.0).astype(BF16)
    rank = _dot(before, onehot.astype(BF16))
    tiles = jnp.ceil(jnp.sum(onehot, axis=0, keepdims=True) * (1.0 / SUBLANES))
    ei = lax.broadcasted_iota(jnp.int32, (LANES, LANES), 0)
    ej = lax.broadcasted_iota(jnp.int32, (LANES, LANES), 1)
    earlier = jnp.where(ei < ej, 1.0, 0.0).astype(BF16)
    off = _dot(jnp.broadcast_to(tiles, (SUBLANES, LANES)).astype(BF16), earlier)[0:1] * SUBLANES
    pos = rank + off
    r1 = jnp.sum(jnp.where(oh1, pos, 0.0), axis=1, keepdims=True)
    r2 = jnp.sum(jnp.where(oh2, pos, 0.0), axis=1, keepdims=True)
    lrow_ref[...] = jnp.where(lane == 0, r1, jnp.where(lane == 1, r2, 0.0)).astype(jnp.int32)
    gt_ref[...] = jnp.where(lane == 0, g1, jnp.where(lane == 1, g2, 0.0))
    sub = lax.broadcasted_iota(jnp.int32, (SUBLANES, LANES), 0)
    tab = jnp.where(sub == 0, tiles, jnp.where(sub == 1, off, jnp.where(sub == 2, carry[...], 0.0)))
    tab_ref[...] = tab.astype(jnp.int32)
    carry[...] = carry[...] + tiles * SUBLANES
    tot_ref[...] = carry[...].astype(jnp.int32)


def _router(logits):
    t = logits.shape[0]
    tb = ROW_TILE
    blk = pl.BlockSpec((tb, LANES), lambda i: (i, 0))
    return pl.pallas_call(
        _router_kernel,
        grid=(t // tb,),
        in_specs=[blk],
        out_specs=[blk, blk, pl.BlockSpec((SUBLANES, LANES), lambda i: (i, 0)),
                   pl.BlockSpec((1, LANES), lambda i: (0, 0))],
        out_shape=[jax.ShapeDtypeStruct((t, LANES), jnp.int32), jax.ShapeDtypeStruct((t, LANES), F32),
                   jax.ShapeDtypeStruct((t // tb * SUBLANES, LANES), jnp.int32),
                   jax.ShapeDtypeStruct((1, LANES), jnp.int32)],
        scratch_shapes=[pltpu.VMEM((1, LANES), F32)],
        compiler_params=_cparams(("arbitrary",), 16),
        name="moe_router",
    )(logits)


def _tile_copies(nt_ref, lo_ref, gd_ref, blk, make_copy, wait):
    for e in range(N_EXPERTS):
        idx = blk * N_EXPERTS + e
        lo, gd = lo_ref[idx], gd_ref[idx]

        def one(j, _):
            cp = make_copy(pl.multiple_of(lo + j * SUBLANES, SUBLANES), pl.multiple_of(gd + j * SUBLANES, SUBLANES))
            if wait:
                cp.wait()
            else:
                cp.start()
            return 0

        lax.fori_loop(0, nt_ref[idx], one, 0)


def _make_dispatch_kernel(tb, n_blocks):
    def kernel(ps_ref, nb_ref, nt_ref, lo_ref, gd_ref, lrow_ref, h_ref, xb_ref, xs, zbuf, sem):
        i = pl.program_id(0)

        @pl.when(i == 0)
        def _():
            zbuf[...] = jnp.zeros_like(zbuf)

            def zero_block(row0):
                dst = pl.multiple_of(row0, MOE_BLOCK)
                cp = pltpu.make_async_copy(zbuf, xb_ref.at[pl.ds(dst, MOE_BLOCK), :], sem.at[2])
                cp.start()
                cp.wait()

            for e in range(N_EXPERTS):
                @pl.when(nb_ref[e] > 0)
                def _():
                    zero_block(ps_ref[e] + (nb_ref[e] - 1) * MOE_BLOCK)

            used = ps_ref[N_EXPERTS - 1] // MOE_BLOCK + nb_ref[N_EXPERTS - 1]

            def tail(j, _):
                zero_block(j * MOE_BLOCK)
                return 0

            lax.fori_loop(used, n_blocks, tail, 0)

        slot = i % 2
        lr = lrow_ref[...]
        r_iota = lax.broadcasted_iota(jnp.int32, (tb, STAGE_ROWS), 1)
        sel = jnp.where((r_iota == lr[:, 0:1]) | (r_iota == lr[:, 1:2]), 1.0, 0.0).astype(BF16)
        xs[slot] = _dot_tn(sel, h_ref[...].astype(BF16))

        def copies_from(buf):
            def make_copy(lo, gd):
                return pltpu.make_async_copy(xs.at[buf, pl.ds(lo, SUBLANES), :], xb_ref.at[pl.ds(gd, SUBLANES), :],
                                             sem.at[buf])
            return make_copy

        _tile_copies(nt_ref, lo_ref, gd_ref, i, copies_from(slot), wait=False)

        @pl.when(i > 0)
        def _():
            _tile_copies(nt_ref, lo_ref, gd_ref, i - 1, copies_from(1 - slot), wait=True)

        @pl.when(i == pl.num_programs(0) - 1)
        def _():
            _tile_copies(nt_ref, lo_ref, gd_ref, i, copies_from(slot), wait=True)

    return kernel


def _dispatch(h2, lrow, pad_start, nblk, ntile, loff, gdest, n_blocks):
    t, d = h2.shape
    tb = ROW_TILE
    grid_spec = pltpu.PrefetchScalarGridSpec(
        num_scalar_prefetch=5, grid=(t // tb,),
        in_specs=[pl.BlockSpec((tb, LANES), lambda i, *_: (i, 0)),
                  pl.BlockSpec((tb, d), lambda i, *_: (i, 0))],
        out_specs=pl.BlockSpec(memory_space=pl.ANY),
        scratch_shapes=[pltpu.VMEM((2, STAGE_ROWS, d), F32), pltpu.VMEM((MOE_BLOCK, d), F32),
                        pltpu.SemaphoreType.DMA((3,))])
    return pl.pallas_call(
        _make_dispatch_kernel(tb, n_blocks), grid_spec=grid_spec,
        out_shape=jax.ShapeDtypeStruct((n_blocks * MOE_BLOCK, d), F32),
        compiler_params=_cparams(("arbitrary",), 24),
        name="moe_dispatch",
    )(pad_start, nblk, ntile, loff, gdest, lrow, h2)


def _expert_kernel(be_ref, nu_ref, xb_ref, wg_ref, wu_ref, wd_ref, yb_ref):
    i = pl.program_id(0)

    @pl.when(i < nu_ref[0])
    def _():
        xb = xb_ref[...].astype(BF16)
        a = (_silu(_dot(xb, wg_ref[...])) * _dot(xb, wu_ref[...])).astype(BF16)
        yb_ref[...] = _dot(a, wd_ref[...])

    @pl.when(i >= nu_ref[0])
    def _():
        yb_ref[...] = jnp.zeros_like(yb_ref)


def _experts(xb, block_e, n_used, wg, wu, wd):
    _, d, f = wg.shape
    n_blocks = xb.shape[0] // MOE_BLOCK
    blk = pl.BlockSpec((MOE_BLOCK, d), lambda i, be, nu: (i, 0))
    grid_spec = pltpu.PrefetchScalarGridSpec(
        num_scalar_prefetch=2, grid=(n_blocks,),
        in_specs=[blk,
                  pl.BlockSpec((None, d, f), lambda i, be, nu: (be[i], 0, 0)),
                  pl.BlockSpec((None, d, f), lambda i, be, nu: (be[i], 0, 0)),
                  pl.BlockSpec((None, f, d), lambda i, be, nu: (be[i], 0, 0))],
        out_specs=blk)
    return pl.pallas_call(
        _expert_kernel, grid_spec=grid_spec,
        out_shape=jax.ShapeDtypeStruct(xb.shape, F32),
        compiler_params=_cparams(("arbitrary",), 56),
        name="moe_experts",
    )(block_e, n_used, xb, wg, wu, wd)


def _make_combine_kernel(tb):
    def kernel(nt_ref, lo_ref, gd_ref, yb_ref, lrow_ref, gt_ref, x_ref, g3_ref, fga_ref, o_ref, ys, sem):
        i = pl.program_id(0)
        slot = i % 2

        def copies_into(buf):
            def make_copy(lo, gd):
                return pltpu.make_async_copy(yb_ref.at[pl.ds(gd, SUBLANES), :], ys.at[buf, pl.ds(lo, SUBLANES), :],
                                             sem.at[buf])
            return make_copy

        @pl.when(i == 0)
        def _():
            ys[...] = jnp.zeros_like(ys)
            _tile_copies(nt_ref, lo_ref, gd_ref, 0, copies_into(0), wait=False)

        @pl.when(i + 1 < pl.num_programs(0))
        def _():
            _tile_copies(nt_ref, lo_ref, gd_ref, i + 1, copies_into(1 - slot), wait=False)

        _tile_copies(nt_ref, lo_ref, gd_ref, i, copies_into(slot), wait=True)

        lr = lrow_ref[...]
        gt = gt_ref[...]
        r_iota = lax.broadcasted_iota(jnp.int32, (tb, STAGE_ROWS), 1)
        q = jnp.where(r_iota == lr[:, 0:1], gt[:, 0:1], 0.0) + jnp.where(r_iota == lr[:, 1:2], gt[:, 1:2], 0.0)
        qh, ql = _split_bf16(q)
        yh, yl = _split_bf16(ys[slot])
        f = _dot(qh, yh) + _dot(ql, yh) + _dot(qh, yl)
        o_ref[...] = x_ref[...] + fga_ref[...] * _rms(f, g3_ref[...])

    return kernel


def _combine(yb, lrow, ntile, loff, gdest, gates, x, mods6, layer, first_row, rows_per_batch, g3):
    t, d = x.shape
    tb = ROW_TILE
    if rows_per_batch is None:
        fga_map = lambda i, *_: (layer, first_row, 5, 0, 0)
    else:
        fga_map = lambda i, *_: (layer, first_row + i // rows_per_batch, 5, 0, 0)
    grid_spec = pltpu.PrefetchScalarGridSpec(
        num_scalar_prefetch=3, grid=(t // tb,),
        in_specs=[pl.BlockSpec(memory_space=pl.ANY),
                  pl.BlockSpec((tb, LANES), lambda i, *_: (i, 0)),
                  pl.BlockSpec((tb, LANES), lambda i, *_: (i, 0)),
                  pl.BlockSpec((tb, d), lambda i, *_: (i, 0)),
                  pl.BlockSpec((1, d), lambda i, *_: (0, 0)),
                  pl.BlockSpec((None, None, None, 1, d), fga_map)],
        out_specs=pl.BlockSpec((tb, d), lambda i, *_: (i, 0)),
        scratch_shapes=[pltpu.VMEM((2, STAGE_ROWS, d), F32), pltpu.SemaphoreType.DMA((2,))])
    return pl.pallas_call(
        _make_combine_kernel(tb), grid_spec=grid_spec,
        out_shape=jax.ShapeDtypeStruct((t, d), F32),
        compiler_params=_cparams(("arbitrary",), 32),
        name="moe_combine",
    )(ntile, loff, gdest, yb, lrow, gates, x, g3, mods6)


def _moe(h2, logits, x, mods6, layer, first_row, rows_per_batch, g3, wg, wu, wd):
    t = x.shape[0]
    n_tok_blocks = t // ROW_TILE
    max_rows = t * TOP_K + n_tok_blocks * N_EXPERTS * (SUBLANES - 1)
    n_blocks = -(-max_rows // MOE_BLOCK) + N_EXPERTS
    lrow, gates, tab, tot = _router(logits)
    tab = tab.reshape(n_tok_blocks, SUBLANES, LANES)[:, :, :N_EXPERTS]
    ntile, loff, prior = tab[:, 0], tab[:, 1], tab[:, 2]
    nblk = (tot[0, :N_EXPERTS] + MOE_BLOCK - 1) // MOE_BLOCK
    blk_end = jnp.cumsum(nblk)
    pad_start = ((blk_end - nblk) * MOE_BLOCK).astype(jnp.int32)
    gdest = (pad_start[None, :] + prior).astype(jnp.int32)
    n_used = blk_end[-1:].astype(jnp.int32)
    blk = jnp.minimum(jnp.arange(n_blocks, dtype=jnp.int32), n_used[0] - 1)
    block_e = jnp.minimum(jnp.sum(blk[:, None] >= blk_end[None, :], axis=1), N_EXPERTS - 1).astype(jnp.int32)
    ntile, loff, gdest = ntile.reshape(-1), loff.reshape(-1), gdest.reshape(-1)
    xb = _dispatch(h2, lrow, pad_start, nblk.astype(jnp.int32), ntile, loff, gdest, n_blocks)
    yb = _experts(xb, block_e, n_used, wg, wu, wd)
    return _combine(yb, lrow, ntile, loff, gdest, gates, x, mods6, layer, first_row, rows_per_batch, g3)


def _pad_cols(w, n):
    return jnp.pad(w, ((0, 0), (0, n - w.shape[1])))


def kernel(x_prompt, x_sample, state_mlstm_C, state_mlstm_n, state_mlstm_m, state_ret_S, c, c_ctx, mod_w, mod_b, norm_g, mlstm_w_in, mlstm_gate_b, mlstm_conv_w, mlstm_conv_b, mlstm_head_g, mlstm_w_out, ret_w_in, ret_decay_logit, ret_head_g, ret_w_out, ffn_w_gate, ffn_w_up, ffn_w_down, moe_router, moe_w_gate, moe_w_up, moe_w_down):
    bp, n_p, d = x_prompt.shape
    bs, n_s, _ = x_sample.shape
    depth = mod_w.shape[0]
    assert depth == 2 and CHUNK % GRID_W == 0 and n_s % GRID_W == 0

    cond = jnp.zeros((MOD_ROWS, d), F32).at[0].set(c_ctx).at[1:1 + bs].set(c)
    mods6 = _modulation(cond, mod_w, mod_b).reshape(depth, MOD_ROWS, N_MOD, 1, d)

    groups = [dict(x=x_prompt.reshape(bp * n_p, d), first=0, rpb=None, nseq=bp, ntok=n_p, prompt=True),
              dict(x=x_sample.reshape(bs * n_s, d), first=1, rpb=n_s // ROW_TILE, nseq=bs, ntok=n_s, prompt=False)]

    j = 0
    ml_qk = (mlstm_w_in.shape[2] - 4 * HEADS) // 2
    ml_v = ml_qk // 2
    w_in = mlstm_w_in[j]
    wqk = w_in[:, :ml_qk].astype(BF16)
    wv = w_in[:, ml_qk:ml_qk + ml_v].astype(BF16)
    wo = w_in[:, ml_qk + ml_v:ml_qk + 2 * ml_v].astype(BF16)
    w_gate = w_in[:, ml_qk + 2 * ml_v:]
    wgh, wgl = _split_bf16(_pad_cols(w_gate, LANES))
    bcol = _pad_cols(mlstm_gate_b[j][None, :], LANES)
    g = norm_g[0]
    w_out0 = mlstm_w_out[j].astype(BF16)
    fwg, fwu, fwd = ffn_w_gate[j].astype(BF16), ffn_w_up[j].astype(BF16), ffn_w_down[j].astype(BF16)
    new_c = new_n = new_m = None
    for grp in groups:
        args = (mods6, 0, grp["first"], grp["rpb"])
        q, kt, v, o, gcol, grow = _proj_mlstm(grp["x"], grp["ntok"], *args, g[0:1], wqk, wv, wo, wgh, wgl, bcol,
                                              w_gate.shape[1], mlstm_conv_w[j], mlstm_conv_b[j][None, :])
        grow3 = grow.reshape(grow.shape[0], 1, grow.shape[1])
        if grp["prompt"]:
            m0 = jnp.zeros((grp["nseq"] * 2 * HEADS,), F32)
            hs, new_c, new_n, new_m = _mlstm_scan(q, kt, v, gcol, grow3, m0, None, grp["nseq"], grp["ntok"], True)
        else:
            (hs,) = _mlstm_scan(q, kt, v, gcol, grow3, state_mlstm_m[:, j].reshape(-1),
                                (state_mlstm_C[:, j], state_mlstm_n[:, j]), grp["nseq"], grp["ntok"], False)
        (grp["x"],) = _mixer_out(hs, o, mlstm_head_g[j][None, :], w_out0, grp["x"], *args, g[1:2], True,
                                 ffn=(g[2:3], g[3:4], fwg, fwu, fwd))

    ret_qk = ret_w_in.shape[2] // 3
    w_in = ret_w_in[j]
    rwqk = w_in[:, :ret_qk].astype(BF16)
    rwv = w_in[:, ret_qk:2 * ret_qk].astype(BF16)
    rwg = w_in[:, 2 * ret_qk:].astype(BF16)
    g = norm_g[1]
    w_out1 = ret_w_out[j].astype(BF16)
    wrh, wrl = _split_bf16(_pad_cols(moe_router[j], LANES))
    ewg, ewu, ewd = moe_w_gate[j].astype(BF16), moe_w_up[j].astype(BF16), moe_w_down[j].astype(BF16)
    decay_flat = ret_decay_logit[j].reshape(-1)
    new_s = None
    for grp in groups:
        args = (mods6, 1, grp["first"], grp["rpb"])
        rope_tabs = None if grp["prompt"] else _rope_tables(grp["ntok"], ret_qk // (2 * HEADS))
        q, kt, v, gate = _proj_ret(grp["x"], grp["ntok"], *args, g[0:1], rwqk, rwv, rwg, rope_tabs)
        if grp["prompt"]:
            hs, new_s = _ret_scan(q, kt, v, decay_flat, None, grp["nseq"], grp["ntok"], True)
        else:
            (hs,) = _ret_scan(q, kt, v, decay_flat, state_ret_S[:, j], grp["nseq"], grp["ntok"], False)
        x1, h2, logits = _mixer_out(hs, gate, ret_head_g[j][None, :], w_out1, grp["x"], *args, g[1:2], False,
                                    router=(g[2:3], wrh, wrl))
        grp["x"] = _moe(h2, logits, x1, *args, g[3:4], ewg, ewu, ewd)

    y_prompt = groups[0]["x"].reshape(bp, n_p, d)
    y_sample = groups[1]["x"].reshape(bs, n_s, d)
    return (y_prompt, y_sample, new_c[:, None], new_n[:, None], new_m[:, None, :, :, 0], new_s[:, None])
```

```python
import functools
import math

import jax
import jax.numpy as jnp
from jax import lax
from jax.experimental import pallas as pl
from jax.experimental.pallas import tpu as pltpu

F32 = jnp.float32
BF16 = jnp.bfloat16

EPS = 1e-6
N_MOD = 6
HEADS = 4
CHUNK = 128
GRID_W = 64
ROPE_BASE = 10000.0
N_EXPERTS = 8
TOP_K = 2
MOE_BLOCK = 256
LANES = 128
SUBLANES = 8
ROW_TILE = 256
MOD_ROWS = 8
STAGE_ROWS = -(-(TOP_K * ROW_TILE + N_EXPERTS * (SUBLANES - 1)) // LANES) * LANES
MIB = 1024 * 1024


def _cparams(sem, vmem_mib):
    return pltpu.CompilerParams(dimension_semantics=sem, vmem_limit_bytes=vmem_mib * MIB)


def _dot(a, b):
    return jnp.dot(a, b, preferred_element_type=F32)


def _dot_nt(a, b):
    return lax.dot_general(a, b, (((1,), (1,)), ((), ())), preferred_element_type=F32)


def _dot_tn(a, b):
    return lax.dot_general(a, b, (((0,), (0,)), ((), ())), preferred_element_type=F32)


def _split_bf16(x):
    hi = x.astype(BF16)
    lo = (x - hi.astype(F32)).astype(BF16)
    return hi, lo


def _rms(x, g):
    return x * lax.rsqrt(jnp.mean(x * x, -1, keepdims=True) + EPS) * g


def _adaln(x, g, shift, scale):
    return _rms(x, g) * (1.0 + scale) + shift


def _silu(x):
    return x * jax.nn.sigmoid(x)


def _logsig(x):
    return jnp.minimum(x, 0.0) - jnp.log1p(jnp.exp(-jnp.abs(x)))


def _const_spec(shape):
    nd = len(shape)
    return pl.BlockSpec(shape, lambda *_: (0,) * nd, pipeline_mode=pl.Buffered(1))


def _mod_spec(d, layer, comp, rows_per_batch, first_row):
    if rows_per_batch is None:
        return pl.BlockSpec((None, None, None, 1, d), lambda i: (layer, first_row, comp, 0, 0))
    return pl.BlockSpec((None, None, None, 1, d), lambda i: (layer, first_row + i // rows_per_batch, comp, 0, 0))


def _mod_kernel(c_ref, w_ref, b_ref, o_ref):
    s = _silu(c_ref[...]).astype(BF16)
    o_ref[...] = _dot(s, w_ref[...].astype(BF16)) + b_ref[...]


def _modulation(cond, mod_w, mod_b):
    depth, d, n = mod_w.shape
    tn = n // 4
    return pl.pallas_call(
        _mod_kernel,
        grid=(depth, n // tn),
        in_specs=[pl.BlockSpec((MOD_ROWS, d), lambda l, j: (0, 0)),
                  pl.BlockSpec((None, d, tn), lambda l, j: (l, 0, j)),
                  pl.BlockSpec((None, 1, tn), lambda l, j: (l, 0, j))],
        out_specs=pl.BlockSpec((None, MOD_ROWS, tn), lambda l, j: (l, 0, j)),
        out_shape=jax.ShapeDtypeStruct((depth, MOD_ROWS, n), F32),
        compiler_params=_cparams(("arbitrary", "arbitrary"), 40),
        name="modulation",
    )(cond, mod_w, mod_b.reshape(depth, 1, n))


BF16_ROWS = 16


def _cast_plan(arrays, n_steps):
    in_specs, out_specs, out_shape, nblks = [], [], [], []
    for a in arrays:
        rows, cols = a.shape
        nblk = max(n for n in range(1, n_steps + 1) if rows % n == 0 and (rows // n) % BF16_ROWS == 0)
        spec = pl.BlockSpec((rows // nblk, cols), lambda i, nb=nblk: (jnp.minimum(i, nb - 1), 0))
        in_specs.append(spec)
        out_specs.append(spec)
        out_shape.append(jax.ShapeDtypeStruct(a.shape, BF16))
        nblks.append(nblk)
    return in_specs, out_specs, out_shape, nblks


def _cast_slabs(srcs, dsts, nblks):
    for src, dst, nb in zip(srcs, dsts, nblks):
        @pl.when(pl.program_id(0) < nb)
        def _():
            dst[...] = src[...].astype(BF16)


def _proj_mlstm_kernel(tiles_per_seq, cast_nblks, *refs):
    nc = len(cast_nblks)
    (x_ref, xp_ref, xn_ref, g_ref, sh_ref, sc_ref, wqk_ref, wv_ref, wo_ref, wgh_ref, wgl_ref, bcol_ref, cw_ref,
     cb_ref) = refs[:14]
    q_ref, kt_ref, v_ref, o_ref, gcol_ref, grow_ref = refs[14 + nc:20 + nc]
    _cast_slabs(refs[14:14 + nc], refs[20 + nc:], cast_nblks)
    tm = x_ref.shape[0]
    nq = q_ref.shape[1]
    h = _adaln(x_ref[...], g_ref[...], sh_ref[...], sc_ref[...])
    hb, hl = _split_bf16(h)
    x_halo = jnp.concatenate([xp_ref[...], xn_ref[...]], axis=0)
    hb_halo = _adaln(x_halo, g_ref[...], sh_ref[...], sc_ref[...]).astype(BF16)
    hb_all = jnp.concatenate([hb, hb_halo], axis=0)
    pos = pl.program_id(0) % tiles_per_seq
    has_prev = jnp.where(pos > 0, 1.0, 0.0)
    has_next = jnp.where(pos < tiles_per_seq - 1, 1.0, 0.0)
    wc = nq // HEADS
    rowi = lax.broadcasted_iota(jnp.int32, (tm, wc), 0)
    for c in range(2 * HEADS):
        sl = slice(c * wc, (c + 1) * wc)
        p_all = _dot(hb_all, wqk_ref[:, sl])
        p = p_all[:tm]
        p_prev = p_all[tm + SUBLANES - 1:tm + SUBLANES, :] * has_prev
        p_next = p_all[tm + SUBLANES:tm + SUBLANES + 1, :] * has_next
        prev = jnp.where(rowi == 0, p_prev, pltpu.roll(p, 1, axis=0))
        nxt = jnp.where(rowi == tm - 1, p_next, pltpu.roll(p, tm - 1, axis=0))
        y = prev * cw_ref[0:1, sl] + p * cw_ref[1:2, sl] + nxt * cw_ref[2:3, sl] + cb_ref[:, sl]
        if c < HEADS:
            q_ref[:, sl] = (y * wc ** -0.5).astype(BF16)
        else:
            kt_ref[(c - HEADS) * wc:(c - HEADS + 1) * wc, :] = y.T.astype(BF16)
    v_ref[...] = _dot(hb, wv_ref[...]).astype(BF16)
    o_ref[...] = _dot(hb, wo_ref[...])
    gc = _dot(hb, wgh_ref[...]) + _dot(hl, wgh_ref[...]) + _dot(hb, wgl_ref[...]) + bcol_ref[...]
    for hh in range(HEADS):
        gcol_ref[hh] = gc if hh == 0 else pltpu.roll(gc, LANES - hh, axis=1)
    grow_ref[...] = gc.T[:grow_ref.shape[0], :]


def _proj_mlstm(x, n_tok, mods6, layer, first_row, rows_per_batch, g, wqk, wv, wo, wgh, wgl, bcol, ng, conv_w, conv_b,
                casts=()):
    t, d = x.shape
    tm = ROW_TILE
    ms = lambda comp: _mod_spec(d, layer, comp, rows_per_batch, first_row)
    nq = wqk.shape[1] // 2
    tps = tm // SUBLANES
    last = t // SUBLANES - 1
    c_in, c_out, c_shape, c_nblks = _cast_plan(casts, t // tm)
    return pl.pallas_call(
        functools.partial(_proj_mlstm_kernel, n_tok // tm, tuple(c_nblks)),
        grid=(t // tm,),
        in_specs=[pl.BlockSpec((tm, d), lambda i: (i, 0)),
                  pl.BlockSpec((SUBLANES, d), lambda i: (jnp.maximum(i * tps - 1, 0), 0)),
                  pl.BlockSpec((SUBLANES, d), lambda i: (jnp.minimum((i + 1) * tps, last), 0)),
                  _const_spec((1, d)), ms(0), ms(1),
                  _const_spec(wqk.shape), _const_spec(wv.shape), _const_spec(wo.shape),
                  _const_spec(wgh.shape), _const_spec(wgl.shape), _const_spec(bcol.shape),
                  _const_spec(conv_w.shape), _const_spec(conv_b.shape)] + c_in,
        out_specs=[pl.BlockSpec((tm, nq), lambda i: (i, 0)),
                   pl.BlockSpec((nq, tm), lambda i: (0, i)),
                   pl.BlockSpec((tm, wv.shape[1]), lambda i: (i, 0)),
                   pl.BlockSpec((tm, wo.shape[1]), lambda i: (i, 0)),
                   pl.BlockSpec((HEADS, tm, LANES), lambda i: (0, i, 0)),
                   pl.BlockSpec((ng, tm), lambda i: (0, i))] + c_out,
        out_shape=[jax.ShapeDtypeStruct((t, nq), BF16),
                   jax.ShapeDtypeStruct((nq, t), BF16),
                   jax.ShapeDtypeStruct((t, wv.shape[1]), BF16),
                   jax.ShapeDtypeStruct((t, wo.shape[1]), F32),
                   jax.ShapeDtypeStruct((HEADS, t, LANES), F32),
                   jax.ShapeDtypeStruct((ng, t), F32)] + c_shape,
        compiler_params=_cparams(("arbitrary",), 56),
        name="proj_mlstm",
    )(x, x, x, g, mods6, mods6, wqk, wv, wo, wgh, wgl, bcol, conv_w, conv_b, *casts)


def _make_mlstm_scan_kernel(n_tok, dk, dv, has_state, emit_state):
    L = CHUNK
    nc = n_tok // L
    assert nc % 2 == 0 and L == LANES

    def kernel(*refs):
        it = iter(refs)
        m0_ref = next(it)
        q_ref, kt_ref, v_ref, gcol_ref = next(it), next(it), next(it), next(it)
        gi_refs = (next(it), next(it))
        if has_state:
            c0_ref, n0_ref = next(it), next(it)
        out_ref = next(it)
        if emit_state:
            cout_ref, nout_ref, mout_ref = next(it), next(it), next(it)
        brep, rmrep, rrow, stats, cst, cbf, nrep, nbf = (next(it) for _ in range(8))

        b = pl.program_id(0)
        hh = pl.program_id(1)

        ri = lax.broadcasted_iota(jnp.int32, (L, L), 0)
        ci = lax.broadcasted_iota(jnp.int32, (L, L), 1)
        lower = ri >= ci
        upper = ri <= ci
        masks = ((lower, upper), (upper, lower))

        tri = tuple(jnp.where(masks[d][0], 1.0, 0.0).astype(BF16) for d in range(2))
        rowid = lax.broadcasted_iota(jnp.int32, (L, LANES), 0)

        def running_max(x, reverse):
            s = 1
            while s < L:
                if reverse:
                    x = jnp.maximum(x, jnp.where(rowid < L - s, pltpu.roll(x, L - s, axis=0), -jnp.inf))
                else:
                    x = jnp.maximum(x, jnp.where(rowid >= s, pltpu.roll(x, s, axis=0), -jnp.inf))
                s *= 2
            return x

        def prep_chunk(c, _):
            r0 = pl.multiple_of(c * L, L)
            gc = gcol_ref[pl.ds(r0, L), :]
            for d in range(2):
                mk, mkt = masks[d]
                fr = jnp.broadcast_to(_logsig(gc[:, 8 * d + 4:8 * d + 5]), (L, LANES))
                ir = jnp.broadcast_to(gc[:, 8 * d:8 * d + 1], (L, LANES))
                f1 = fr.astype(BF16)
                e1 = fr - f1.astype(F32)
                f2 = e1.astype(BF16)
                f3 = (e1 - f2.astype(F32)).astype(BF16)
                b_rep = _dot(tri[d], f1) + _dot(tri[d], f2) + _dot(tri[d], f3)
                rm_rep = running_max(ir - b_rep, reverse=(d == 1))
                b_row = jnp.sum(jnp.where(mkt, fr, 0.0), axis=0, keepdims=True)
                brep[d, pl.ds(r0, L), :] = b_rep
                rmrep[d, pl.ds(r0, L), :] = rm_rep
                rrow[d, :, pl.ds(r0, L)] = gi_refs[d][:, pl.ds(r0, L)] - b_row
                end = 0 if d == 1 else L - 1
                stats[c, pl.ds(2 * d, 1), :] = b_rep[end:end + 1, :]
                stats[c, pl.ds(2 * d + 1, 1), :] = rm_rep[end:end + 1, :]
            return 0

        lax.fori_loop(0, nc, prep_chunk, 0, unroll=2)

        if has_state:
            cst[...] = c0_ref[...]
            for d in range(2):
                nrep[d] = jnp.broadcast_to(n0_ref[d], (dk, LANES))
        else:
            cst[...] = jnp.zeros_like(cst)
            nrep[...] = jnp.zeros_like(nrep)
        cbf[...] = cst[...].astype(BF16)
        nbf[...] = nrep[...].astype(BF16)
        m_init = tuple(jnp.full((1, LANES), m0_ref[(b * 2 + d) * HEADS + hh], F32) for d in range(2))

        def lanes(x, n):
            return jnp.concatenate([x] * (n // LANES), axis=1)

        def chunk(d, c, m):
            mk = masks[d][0]
            r0 = pl.multiple_of(c * L, L)
            q = q_ref[pl.ds(r0, L), :]
            kt = kt_ref[:, pl.ds(r0, L)]
            v = v_ref[pl.ds(r0, L), :]
            rr = rrow[d, :, pl.ds(r0, L)]
            st = stats[c]
            b_end, rm_end = st[2 * d:2 * d + 1, :], st[2 * d + 1:2 * d + 2, :]
            mm = jnp.maximum(m, rmrep[d, pl.ds(r0, L), :])
            qkn = _dot(q, jnp.concatenate([kt, nbf[d]], axis=1))
            s = qkn[:, :L] * jnp.where(mk, jnp.exp(rr - mm), 0.0)
            w_inter = jnp.exp(m - mm)
            num = lanes(w_inter, dv) * _dot(q, cbf[d]) + _dot(s.astype(BF16), v)
            den = w_inter * qkn[:, L:] + jnp.sum(s, axis=1, keepdims=True)
            inv = 1.0 / jnp.maximum(jnp.abs(den), jnp.exp(-(brep[d, pl.ds(r0, L), :] + mm)))
            h = num * lanes(inv, dv)
            m_end = jnp.maximum(m, rm_end)
            decay = jnp.exp(m - m_end)
            kw = kt.astype(F32) * jnp.exp(rr - m_end)
            c_new = lanes(decay, dv) * cst[d] + _dot(kw.astype(BF16), v)
            n_new = decay * nrep[d] + jnp.sum(kw, axis=1, keepdims=True)
            cst[d] = c_new
            cbf[d] = c_new.astype(BF16)
            nrep[d] = n_new
            nbf[d] = n_new.astype(BF16)
            return h, b_end + m_end

        def make_body(accumulate):
            def body(i, carry):
                mf, mb = carry
                hf, mf = chunk(0, i, mf)
                rf = pl.multiple_of(i * L, L)
                cb = nc - 1 - i
                hb, mb = chunk(1, cb, mb)
                rb = pl.multiple_of(cb * L, L)
                if accumulate:
                    out_ref[pl.ds(rf, L), :] += hf
                    out_ref[pl.ds(rb, L), :] += hb
                else:
                    out_ref[pl.ds(rf, L), :] = hf
                    out_ref[pl.ds(rb, L), :] = hb
                return mf, mb
            return body

        unroll = 2 if nc % 4 == 0 else 1
        carry = lax.fori_loop(0, nc // 2, make_body(False), m_init, unroll=unroll)
        carry = lax.fori_loop(nc // 2, nc, make_body(True), carry, unroll=unroll)

        if emit_state:
            cout_ref[...] = cst[...]
            for d in range(2):
                nout_ref[d, pl.ds(hh, 1), :] = nrep[d].T[0:1, :]
                mout_ref[d, pl.ds(hh, 1), :] = carry[d]

    return kernel


def _mlstm_scan(q, kt, v, gcol, grow3, m0_flat, state, n_seq, n_tok, emit_state):
    dk = q.shape[1] // HEADS
    dv = v.shape[1] // HEADS
    has_state = state is not None
    kern = _make_mlstm_scan_kernel(n_tok, dk, dv, has_state, emit_state)
    grow_spec = lambda r: pl.BlockSpec((None, 1, n_tok), lambda b, h, m: (r * HEADS + h, 0, b))
    in_specs = [pl.BlockSpec((n_tok, dk), lambda b, h, m: (b, h)),
                pl.BlockSpec((dk, n_tok), lambda b, h, m: (h, b)),
                pl.BlockSpec((n_tok, dv), lambda b, h, m: (b, h)),
                pl.BlockSpec((None, n_tok, LANES), lambda b, h, m: (h, b, 0)),
                grow_spec(0), grow_spec(2)]
    args = [q, kt, v, gcol, grow3, grow3]
    if has_state:
        in_specs += [pl.BlockSpec((None, 2, None, dk, dv), lambda b, h, m: (b, 0, h, 0, 0)),
                     pl.BlockSpec((None, 2, None, dk, 1), lambda b, h, m: (b, 0, h, 0, 0))]
        args += [state[0], state[1][..., None]]
    out_specs = [pl.BlockSpec((n_tok, dv), lambda b, h, m: (b, h))]
    out_shape = [jax.ShapeDtypeStruct((n_seq * n_tok, HEADS * dv), F32)]
    if emit_state:
        out_specs += [pl.BlockSpec((None, 2, None, dk, dv), lambda b, h, m: (b, 0, h, 0, 0)),
                      pl.BlockSpec((None, 2, HEADS, dk), lambda b, h, m: (b, 0, 0, 0)),
                      pl.BlockSpec((None, 2, HEADS, LANES), lambda b, h, m: (b, 0, 0, 0))]
        out_shape += [jax.ShapeDtypeStruct((n_seq, 2, HEADS, dk, dv), F32),
                      jax.ShapeDtypeStruct((n_seq, 2, HEADS, dk), F32),
                      jax.ShapeDtypeStruct((n_seq, 2, HEADS, LANES), F32)]
    grid_spec = pltpu.PrefetchScalarGridSpec(
        num_scalar_prefetch=1, grid=(n_seq, HEADS), in_specs=in_specs, out_specs=out_specs,
        scratch_shapes=[pltpu.VMEM((2, n_tok, LANES), F32),
                        pltpu.VMEM((2, n_tok, LANES), F32),
                        pltpu.VMEM((2, 1, n_tok), F32),
                        pltpu.VMEM((n_tok // CHUNK, SUBLANES, LANES), F32),
                        pltpu.VMEM((2, dk, dv), F32), pltpu.VMEM((2, dk, dv), BF16),
                        pltpu.VMEM((2, dk, LANES), F32), pltpu.VMEM((2, dk, LANES), BF16)])
    return pl.pallas_call(
        kern, grid_spec=grid_spec, out_shape=out_shape,
        compiler_params=_cparams(("arbitrary", "arbitrary"), 56),
        name="mlstm_scan",
    )(m0_flat, *args)


def _proj_ret_kernel(rope, cast_nblks, *refs):
    nc = len(cast_nblks)
    n_in = 5 + (4 if rope else 0)
    x_ref, g_ref, sh_ref, sc_ref, w_ref = refs[:5]
    if rope:
        rcos_ref, rsin_ref, ccos_ref, csin_ref = refs[5:9]
    q_ref, kt_ref, v_ref, gate_ref = refs[n_in + nc:n_in + nc + 4]
    _cast_slabs(refs[n_in:n_in + nc], refs[n_in + nc + 4:], cast_nblks)
    nq, nv = q_ref.shape[1], v_ref.shape[1]
    wc = nq // HEADS
    hb = _adaln(x_ref[...], g_ref[...], sh_ref[...], sc_ref[...]).astype(BF16)
    if rope:
        cos = jnp.concatenate([rcos_ref[...], ccos_ref[...]], axis=1)
        sin = jnp.concatenate([rsin_ref[...], csin_ref[...]], axis=1)
    for c in range(2 * HEADS):
        p = _dot(hb, w_ref[:, c * wc:(c + 1) * wc])
        if rope:
            swapped = jnp.concatenate([pltpu.roll(p[:, j * LANES:(j + 1) * LANES], LANES // 2, axis=1)
                                       for j in range(wc // LANES)], axis=1)
            p = p * cos + swapped * sin
        if c < HEADS:
            q_ref[:, c * wc:(c + 1) * wc] = p.astype(BF16)
        else:
            kt_ref[(c - HEADS) * wc:(c - HEADS + 1) * wc, :] = (p * wc ** -0.5).T.astype(BF16)
    v_ref[...] = _dot(hb, w_ref[:, 2 * nq:2 * nq + nv]).astype(BF16)
    gate_ref[...] = _dot(hb, w_ref[:, 2 * nq + nv:])


def _proj_ret(x, n_tok, mods6, layer, first_row, rows_per_batch, g, w_in, nq, nv, rope_tabs, casts=()):
    t, d = x.shape
    tm = ROW_TILE
    ms = lambda comp: _mod_spec(d, layer, comp, rows_per_batch, first_row)
    ng = w_in.shape[1] - 2 * nq - nv
    rope = rope_tabs is not None
    in_specs = [pl.BlockSpec((tm, d), lambda i: (i, 0)), _const_spec((1, d)), ms(0), ms(1), _const_spec(w_in.shape)]
    args = [x, g, mods6, mods6, w_in]
    if rope:
        tiles_per_seq = n_tok // tm
        row_spec = pl.BlockSpec((tm, LANES), lambda i: (i % tiles_per_seq, 0))
        in_specs += [row_spec, row_spec, _const_spec((tm, LANES)), _const_spec((tm, LANES))]
        args += list(rope_tabs)
    c_in, c_out, c_shape, c_nblks = _cast_plan(casts, t // tm)
    return pl.pallas_call(
        functools.partial(_proj_ret_kernel, rope, tuple(c_nblks)),
        grid=(t // tm,),
        in_specs=in_specs + c_in,
        out_specs=[pl.BlockSpec((tm, nq), lambda i: (i, 0)),
                   pl.BlockSpec((nq, tm), lambda i: (0, i)),
                   pl.BlockSpec((tm, nv), lambda i: (i, 0)),
                   pl.BlockSpec((tm, ng), lambda i: (i, 0))] + c_out,
        out_shape=[jax.ShapeDtypeStruct((t, nq), BF16),
                   jax.ShapeDtypeStruct((nq, t), BF16),
                   jax.ShapeDtypeStruct((t, nv), BF16),
                   jax.ShapeDtypeStruct((t, ng), F32)] + c_shape,
        compiler_params=_cparams(("arbitrary",), 56),
        name="proj_ret",
    )(*args, *casts)


def _make_ret_scan_kernel(n_tok, dk, dv, has_state, emit_state):
    L = CHUNK
    nc = n_tok // L
    assert nc % 2 == 0 and L == LANES

    def kernel(*refs):
        it = iter(refs)
        dl_ref = next(it)
        q_ref, kt_ref, v_ref = next(it), next(it), next(it)
        if has_state:
            s0_ref = next(it)
        out_ref = next(it)
        if emit_state:
            sout_ref = next(it)
        sst, sbf = next(it), next(it)

        hh = pl.program_id(1)
        ri = lax.broadcasted_iota(jnp.int32, (L, L), 0)
        ci = lax.broadcasted_iota(jnp.int32, (L, L), 1)
        rel = (ri - ci).astype(F32)
        pos_col = ri.astype(F32)
        pos_row = ci[0:1, :].astype(F32)

        dmat, q_dec, k_dec, c_dec = [], [], [], []
        for d in range(2):
            lg = _logsig(jnp.full((1, LANES), dl_ref[d * HEADS + hh], F32))
            if d == 0:
                dmat.append(jnp.where(ri >= ci, jnp.exp(lg * jnp.maximum(rel, 0.0)), 0.0))
                q_dec.append(jnp.exp(lg * (pos_col + 1.0)))
                k_dec.append(jnp.exp(lg * (L - 1.0 - pos_row)))
            else:
                dmat.append(jnp.where(ri <= ci, jnp.exp(lg * jnp.maximum(-rel, 0.0)), 0.0))
                q_dec.append(jnp.exp(lg * (L - pos_col)))
                k_dec.append(jnp.exp(lg * pos_row))
            c_dec.append(jnp.exp(lg * float(L)))

        if has_state:
            sst[...] = s0_ref[...]
        else:
            sst[...] = jnp.zeros_like(sst)
        sbf[...] = sst[...].astype(BF16)

        def lanes(x, n):
            return jnp.concatenate([x] * (n // LANES), axis=1)

        def chunk(d, c):
            r0 = pl.multiple_of(c * L, L)
            q = q_ref[pl.ds(r0, L), :]
            kt = kt_ref[:, pl.ds(r0, L)]
            v = v_ref[pl.ds(r0, L), :]
            s = _dot(q, kt) * dmat[d]
            out = _dot(s.astype(BF16), v) + lanes(q_dec[d], dv) * _dot(q, sbf[d])
            s_new = lanes(c_dec[d], dv) * sst[d] + _dot((kt.astype(F32) * k_dec[d]).astype(BF16), v)
            sst[d] = s_new
            sbf[d] = s_new.astype(BF16)
            return out

        def body1(i, _):
            rf = pl.multiple_of(i * L, L)
            out_ref[pl.ds(rf, L), :] = chunk(0, i)
            cb = nc - 1 - i
            rb = pl.multiple_of(cb * L, L)
            out_ref[pl.ds(rb, L), :] = chunk(1, cb)
            return 0

        def body2(i, _):
            rf = pl.multiple_of(i * L, L)
            out_ref[pl.ds(rf, L), :] += chunk(0, i)
            cb = nc - 1 - i
            rb = pl.multiple_of(cb * L, L)
            out_ref[pl.ds(rb, L), :] += chunk(1, cb)
            return 0

        unroll = 2 if nc % 4 == 0 else 1
        lax.fori_loop(0, nc // 2, body1, 0, unroll=unroll)
        lax.fori_loop(nc // 2, nc, body2, 0, unroll=unroll)
        if emit_state:
            sout_ref[...] = sst[...]

    return kernel


def _ret_scan(q, kt, v, decay_flat, state, n_seq, n_tok, emit_state):
    dk = q.shape[1] // HEADS
    dv = v.shape[1] // HEADS
    has_state = state is not None
    kern = _make_ret_scan_kernel(n_tok, dk, dv, has_state, emit_state)
    in_specs = [pl.BlockSpec((n_tok, dk), lambda b, h, m: (b, h)),
                pl.BlockSpec((dk, n_tok), lambda b, h, m: (h, b)),
                pl.BlockSpec((n_tok, dv), lambda b, h, m: (b, h))]
    args = [q, kt, v]
    if has_state:
        in_specs.append(pl.BlockSpec((None, 2, None, dk, dv), lambda b, h, m: (b, 0, h, 0, 0)))
        args.append(state)
    out_specs = [pl.BlockSpec((n_tok, dv), lambda b, h, m: (b, h))]
    out_shape = [jax.ShapeDtypeStruct((n_seq * n_tok, HEADS * dv), F32)]
    if emit_state:
        out_specs.append(pl.BlockSpec((None, 2, None, dk, dv), lambda b, h, m: (b, 0, h, 0, 0)))
        out_shape.append(jax.ShapeDtypeStruct((n_seq, 2, HEADS, dk, dv), F32))
    grid_spec = pltpu.PrefetchScalarGridSpec(
        num_scalar_prefetch=1, grid=(n_seq, HEADS), in_specs=in_specs, out_specs=out_specs,
        scratch_shapes=[pltpu.VMEM((2, dk, dv), F32), pltpu.VMEM((2, dk, dv), BF16)])
    return pl.pallas_call(
        kern, grid_spec=grid_spec, out_shape=out_shape,
        compiler_params=_cparams(("arbitrary", "arbitrary"), 56),
        name="ret_scan",
    )(decay_flat, *args)


def _rope_tables(n_tok, dk):
    r = dk // 4
    inv = 1.0 / (ROPE_BASE ** (jnp.arange(r, dtype=F32) / r))
    sign = jnp.concatenate([-jnp.ones((r,), F32), jnp.ones((r,), F32)])
    rows = (jnp.arange(n_tok) // GRID_W).astype(F32)[:, None] * inv
    cols = (jnp.arange(ROW_TILE) % GRID_W).astype(F32)[:, None] * inv
    two = lambda a: jnp.concatenate([a, a], axis=-1)
    return (two(jnp.cos(rows)), two(jnp.sin(rows)) * sign, two(jnp.cos(cols)), two(jnp.sin(cols)) * sign)


def _make_mixer_out_kernel(dv, sigmoid_gate, with_router, with_ffn):
    def kernel(*refs):
        it = iter(refs)
        hs_ref, gate_ref, hg_ref, w_ref, x_ref, g1_ref, ga_ref = (next(it) for _ in range(7))
        if with_router:
            g2_ref, fsh_ref, fsc_ref, wrh_ref, wrl_ref = (next(it) for _ in range(5))
        if with_ffn:
            g2_ref, fsh_ref, fsc_ref, wg_ref, wu_ref, wd_ref, g3_ref, fga_ref = (next(it) for _ in range(8))
        xo_ref = next(it)
        if with_router:
            h2_ref, lg_ref = next(it), next(it)
        z_ref = next(it)
        for hh in range(HEADS):
            sl = slice(hh * dv, (hh + 1) * dv)
            seg = hs_ref[:, sl]
            y = seg * lax.rsqrt(jnp.mean(seg * seg, -1, keepdims=True) + EPS) * hg_ref[:, sl]
            gt = gate_ref[:, sl]
            act = jax.nn.sigmoid(gt) if sigmoid_gate else _silu(gt)
            z_ref[:, sl] = (act * y).astype(BF16)
        y = _dot(z_ref[...], w_ref[...])
        xn = x_ref[...] + ga_ref[...] * _rms(y, g1_ref[...])
        if with_ffn:
            hb = _adaln(xn, g2_ref[...], fsh_ref[...], fsc_ref[...]).astype(BF16)
            a = (_silu(_dot(hb, wg_ref[...])) * _dot(hb, wu_ref[...])).astype(BF16)
            xn = xn + fga_ref[...] * _rms(_dot(a, wd_ref[...]), g3_ref[...])
        xo_ref[...] = xn
        if with_router:
            h2 = _adaln(xn, g2_ref[...], fsh_ref[...], fsc_ref[...])
            h2_ref[...] = h2
            hb, hl = _split_bf16(h2)
            lg_ref[...] = _dot(hb, wrh_ref[...]) + _dot(hl, wrh_ref[...]) + _dot(hb, wrl_ref[...])
    return kernel


def _mixer_out(hs, gate, head_g, w_out, x, mods6, layer, first_row, rows_per_batch, g1, sigmoid_gate,
               router=None, ffn=None):
    t, d = x.shape
    vdim = hs.shape[1]
    tm = ROW_TILE
    ms = lambda comp: _mod_spec(d, layer, comp, rows_per_batch, first_row)
    with_router = router is not None
    with_ffn = ffn is not None
    in_specs = [pl.BlockSpec((tm, vdim), lambda i: (i, 0)),
                pl.BlockSpec((tm, vdim), lambda i: (i, 0)),
                _const_spec((1, vdim)), _const_spec(w_out.shape),
                pl.BlockSpec((tm, d), lambda i: (i, 0)), _const_spec((1, d)), ms(2)]
    args = [hs, gate, head_g, w_out, x, g1, mods6]
    out_specs = [pl.BlockSpec((tm, d), lambda i: (i, 0))]
    out_shape = [jax.ShapeDtypeStruct((t, d), F32)]
    if with_router:
        g2, wrh, wrl = router
        in_specs += [_const_spec((1, d)), ms(3), ms(4), _const_spec(wrh.shape), _const_spec(wrl.shape)]
        args += [g2, mods6, mods6, wrh, wrl]
        out_specs += [pl.BlockSpec((tm, d), lambda i: (i, 0)), pl.BlockSpec((tm, LANES), lambda i: (i, 0))]
        out_shape += [jax.ShapeDtypeStruct((t, d), F32), jax.ShapeDtypeStruct((t, LANES), F32)]
    if with_ffn:
        g2, g3, wg, wu, wd = ffn
        in_specs += [_const_spec((1, d)), ms(3), ms(4), _const_spec(wg.shape), _const_spec(wu.shape),
                     _const_spec(wd.shape), _const_spec((1, d)), ms(5)]
        args += [g2, mods6, mods6, wg, wu, wd, g3, mods6]
    return pl.pallas_call(
        _make_mixer_out_kernel(vdim // HEADS, sigmoid_gate, with_router, with_ffn),
        grid=(t // tm,), in_specs=in_specs, out_specs=out_specs, out_shape=out_shape,
        scratch_shapes=[pltpu.VMEM((tm, vdim), BF16)],
        compiler_params=_cparams(("arbitrary",), 48),
        name="mixer_ffn" if with_ffn else "mixer_out",
    )(*args)


def _router_kernel(lg_ref, lrow_ref, gt_ref, tab_ref, tot_ref, carry):
    tb = lg_ref.shape[0]

    @pl.when(pl.program_id(0) == 0)
    def _():
        carry[...] = jnp.zeros_like(carry)

    lane = lax.broadcasted_iota(jnp.int32, (tb, LANES), 1)
    lg = jnp.where(lane < N_EXPERTS, lg_ref[...], -jnp.inf)
    v1 = jnp.max(lg, axis=1, keepdims=True)
    i1 = jnp.min(jnp.where(lg == v1, lane, LANES), axis=1, keepdims=True)
    lg2 = jnp.where(lane == i1, -jnp.inf, lg)
    v2 = jnp.max(lg2, axis=1, keepdims=True)
    i2 = jnp.min(jnp.where(lg2 == v2, lane, LANES), axis=1, keepdims=True)
    ex = jnp.exp(v2 - v1)
    den = 1.0 + ex
    g1 = 1.0 / den
    g2 = ex / den
    oh1 = lane == i1
    oh2 = lane == i2
    onehot = jnp.where(oh1 | oh2, 1.0, 0.0)
    ri = lax.broadcasted_iota(jnp.int32, (tb, tb), 0)
    ci = lax.broadcasted_iota(jnp.int32, (tb, tb), 1)
    before = jnp.where(ri > ci, 1.0, 0.0).astype(BF16)
    rank = _dot(before, onehot.astype(BF16))
    tiles = jnp.ceil(jnp.sum(onehot, axis=0, keepdims=True) * (1.0 / SUBLANES))
    ei = lax.broadcasted_iota(jnp.int32, (LANES, LANES), 0)
    ej = lax.broadcasted_iota(jnp.int32, (LANES, LANES), 1)
    earlier = jnp.where(ei < ej, 1.0, 0.0).astype(BF16)
    off = _dot(jnp.broadcast_to(tiles, (SUBLANES, LANES)).astype(BF16), earlier)[0:1] * SUBLANES
    pos = rank + off
    r1 = jnp.sum(jnp.where(oh1, pos, 0.0), axis=1, keepdims=True)
    r2 = jnp.sum(jnp.where(oh2, pos, 0.0), axis=1, keepdims=True)
    lrow_ref[...] = jnp.where(lane == 0, r1, jnp.where(lane == 1, r2, 0.0)).astype(jnp.int32)
    gt_ref[...] = jnp.where(lane == 0, g1, jnp.where(lane == 1, g2, 0.0))
    sub = lax.broadcasted_iota(jnp.int32, (SUBLANES, LANES), 0)
    tab = jnp.where(sub == 0, tiles, jnp.where(sub == 1, off, jnp.where(sub == 2, carry[...], 0.0)))
    tab_ref[...] = tab.astype(jnp.int32)
    carry[...] = carry[...] + tiles * SUBLANES
    tot_ref[...] = carry[...].astype(jnp.int32)


def _router(logits):
    t = logits.shape[0]
    tb = ROW_TILE
    blk = pl.BlockSpec((tb, LANES), lambda i: (i, 0))
    return pl.pallas_call(
        _router_kernel,
        grid=(t // tb,),
        in_specs=[blk],
        out_specs=[blk, blk, pl.BlockSpec((SUBLANES, LANES), lambda i: (i, 0)),
                   pl.BlockSpec((1, LANES), lambda i: (0, 0))],
        out_shape=[jax.ShapeDtypeStruct((t, LANES), jnp.int32), jax.ShapeDtypeStruct((t, LANES), F32),
                   jax.ShapeDtypeStruct((t // tb * SUBLANES, LANES), jnp.int32),
                   jax.ShapeDtypeStruct((1, LANES), jnp.int32)],
        scratch_shapes=[pltpu.VMEM((1, LANES), F32)],
        compiler_params=_cparams(("arbitrary",), 16),
        name="moe_router",
    )(logits)


RUN_TILES = 8


def _tile_copies(nt_ref, lo_ref, gd_ref, blk, make_copy, wait):
    def go(lo, gd, rows):
        cp = make_copy(pl.multiple_of(lo, SUBLANES), pl.multiple_of(gd, SUBLANES), rows)
        if wait:
            cp.wait()
        else:
            cp.start()

    for e in range(N_EXPERTS):
        idx = blk * N_EXPERTS + e
        nt, lo, gd = nt_ref[idx], lo_ref[idx], gd_ref[idx]

        def run(j, _):
            off = j * (RUN_TILES * SUBLANES)
            go(lo + off, gd + off, RUN_TILES * SUBLANES)
            return 0

        lax.fori_loop(0, nt // RUN_TILES, run, 0)
        k = RUN_TILES // 2
        while k >= 1:
            @pl.when(nt % (2 * k) >= k)
            def _():
                off = (nt // (2 * k)) * (2 * k) * SUBLANES
                go(lo + off, gd + off, k * SUBLANES)
            k //= 2


def _make_dispatch_kernel(tb, n_blocks):
    def kernel(ps_ref, nb_ref, nt_ref, lo_ref, gd_ref, lrow_ref, h_ref, xb_ref, xs, zbuf, sem):
        i = pl.program_id(0)

        @pl.when(i == 0)
        def _():
            zbuf[...] = jnp.zeros_like(zbuf)

            def zero_block(row0):
                dst = pl.multiple_of(row0, MOE_BLOCK)
                cp = pltpu.make_async_copy(zbuf, xb_ref.at[pl.ds(dst, MOE_BLOCK), :], sem.at[2])
                cp.start()
                cp.wait()

            for e in range(N_EXPERTS):
                @pl.when(nb_ref[e] > 0)
                def _():
                    zero_block(ps_ref[e] + (nb_ref[e] - 1) * MOE_BLOCK)

            used = ps_ref[N_EXPERTS - 1] // MOE_BLOCK + nb_ref[N_EXPERTS - 1]

            def tail(j, _):
                zero_block(j * MOE_BLOCK)
                return 0

            lax.fori_loop(used, n_blocks, tail, 0)

        slot = i % 2
        lr = lrow_ref[...]
        r_iota = lax.broadcasted_iota(jnp.int32, (tb, STAGE_ROWS), 1)
        sel = jnp.where((r_iota == lr[:, 0:1]) | (r_iota == lr[:, 1:2]), 1.0, 0.0).astype(BF16)
        xs[slot] = _dot_tn(sel, h_ref[...].astype(BF16))

        def copies_from(buf):
            def make_copy(lo, gd, rows):
                return pltpu.make_async_copy(xs.at[buf, pl.ds(lo, rows), :], xb_ref.at[pl.ds(gd, rows), :],
                                             sem.at[buf])
            return make_copy

        _tile_copies(nt_ref, lo_ref, gd_ref, i, copies_from(slot), wait=False)

        @pl.when(i > 0)
        def _():
            _tile_copies(nt_ref, lo_ref, gd_ref, i - 1, copies_from(1 - slot), wait=True)

        @pl.when(i == pl.num_programs(0) - 1)
        def _():
            _tile_copies(nt_ref, lo_ref, gd_ref, i, copies_from(slot), wait=True)

    return kernel


def _dispatch(h2, lrow, pad_start, nblk, ntile, loff, gdest, n_blocks):
    t, d = h2.shape
    tb = ROW_TILE
    grid_spec = pltpu.PrefetchScalarGridSpec(
        num_scalar_prefetch=5, grid=(t // tb,),
        in_specs=[pl.BlockSpec((tb, LANES), lambda i, *_: (i, 0)),
                  pl.BlockSpec((tb, d), lambda i, *_: (i, 0))],
        out_specs=pl.BlockSpec(memory_space=pl.ANY),
        scratch_shapes=[pltpu.VMEM((2, STAGE_ROWS, d), F32), pltpu.VMEM((MOE_BLOCK, d), F32),
                        pltpu.SemaphoreType.DMA((3,))])
    return pl.pallas_call(
        _make_dispatch_kernel(tb, n_blocks), grid_spec=grid_spec,
        out_shape=jax.ShapeDtypeStruct((n_blocks * MOE_BLOCK, d), F32),
        compiler_params=_cparams(("arbitrary",), 24),
        name="moe_dispatch",
    )(pad_start, nblk, ntile, loff, gdest, lrow, h2)


def _expert_kernel(be_ref, nu_ref, xb_ref, wg_ref, wu_ref, wd_ref, yb_ref):
    i = pl.program_id(0)

    @pl.when(i < nu_ref[0])
    def _():
        xb = xb_ref[...].astype(BF16)
        a = (_silu(_dot(xb, wg_ref[...])) * _dot(xb, wu_ref[...])).astype(BF16)
        yb_ref[...] = _dot(a, wd_ref[...]).astype(BF16).astype(F32)

    @pl.when(i >= nu_ref[0])
    def _():
        yb_ref[...] = jnp.zeros_like(yb_ref)


def _experts(xb, block_e, n_used, wg, wu, wd):
    _, d, f = wg.shape
    n_blocks = xb.shape[0] // MOE_BLOCK
    blk = pl.BlockSpec((MOE_BLOCK, d), lambda i, be, nu: (i, 0))
    grid_spec = pltpu.PrefetchScalarGridSpec(
        num_scalar_prefetch=2, grid=(n_blocks,),
        in_specs=[blk,
                  pl.BlockSpec((None, d, f), lambda i, be, nu: (be[i], 0, 0)),
                  pl.BlockSpec((None, d, f), lambda i, be, nu: (be[i], 0, 0)),
                  pl.BlockSpec((None, f, d), lambda i, be, nu: (be[i], 0, 0))],
        out_specs=blk)
    return pl.pallas_call(
        _expert_kernel, grid_spec=grid_spec,
        out_shape=jax.ShapeDtypeStruct(xb.shape, F32),
        compiler_params=_cparams(("arbitrary",), 56),
        name="moe_experts",
    )(block_e, n_used, xb, wg, wu, wd)


def _make_combine_kernel(tb):
    def kernel(nt_ref, lo_ref, gd_ref, yb_ref, lrow_ref, gt_ref, x_ref, g3_ref, fga_ref, o_ref, ys, sem):
        i = pl.program_id(0)
        slot = i % 2

        def copies_into(buf):
            def make_copy(lo, gd, rows):
                return pltpu.make_async_copy(yb_ref.at[pl.ds(gd, rows), :], ys.at[buf, pl.ds(lo, rows), :],
                                             sem.at[buf])
            return make_copy

        @pl.when(i == 0)
        def _():
            ys[...] = jnp.zeros_like(ys)
            _tile_copies(nt_ref, lo_ref, gd_ref, 0, copies_into(0), wait=False)

        @pl.when(i + 1 < pl.num_programs(0))
        def _():
            _tile_copies(nt_ref, lo_ref, gd_ref, i + 1, copies_into(1 - slot), wait=False)

        _tile_copies(nt_ref, lo_ref, gd_ref, i, copies_into(slot), wait=True)

        lr = lrow_ref[...]
        gt = gt_ref[...]
        r_iota = lax.broadcasted_iota(jnp.int32, (tb, STAGE_ROWS), 1)
        q = jnp.where(r_iota == lr[:, 0:1], gt[:, 0:1], 0.0) + jnp.where(r_iota == lr[:, 1:2], gt[:, 1:2], 0.0)
        qh, ql = _split_bf16(q)
        y = ys[slot].astype(BF16)
        f = _dot(qh, y) + _dot(ql, y)
        o_ref[...] = x_ref[...] + fga_ref[...] * _rms(f, g3_ref[...])

    return kernel


def _combine(yb, lrow, ntile, loff, gdest, gates, x, mods6, layer, first_row, rows_per_batch, g3):
    t, d = x.shape
    tb = ROW_TILE
    if rows_per_batch is None:
        fga_map = lambda i, *_: (layer, first_row, 5, 0, 0)
    else:
        fga_map = lambda i, *_: (layer, first_row + i // rows_per_batch, 5, 0, 0)
    grid_spec = pltpu.PrefetchScalarGridSpec(
        num_scalar_prefetch=3, grid=(t // tb,),
        in_specs=[pl.BlockSpec(memory_space=pl.ANY),
                  pl.BlockSpec((tb, LANES), lambda i, *_: (i, 0)),
                  pl.BlockSpec((tb, LANES), lambda i, *_: (i, 0)),
                  pl.BlockSpec((tb, d), lambda i, *_: (i, 0)),
                  pl.BlockSpec((1, d), lambda i, *_: (0, 0)),
                  pl.BlockSpec((None, None, None, 1, d), fga_map)],
        out_specs=pl.BlockSpec((tb, d), lambda i, *_: (i, 0)),
        scratch_shapes=[pltpu.VMEM((2, STAGE_ROWS, d), F32), pltpu.SemaphoreType.DMA((2,))])
    return pl.pallas_call(
        _make_combine_kernel(tb), grid_spec=grid_spec,
        out_shape=jax.ShapeDtypeStruct((t, d), F32),
        compiler_params=_cparams(("arbitrary",), 32),
        name="moe_combine",
    )(ntile, loff, gdest, yb, lrow, gates, x, g3, mods6)


def _moe(h2, logits, x, mods6, layer, first_row, rows_per_batch, g3, wg, wu, wd):
    t = x.shape[0]
    n_tok_blocks = t // ROW_TILE
    max_rows = t * TOP_K + n_tok_blocks * N_EXPERTS * (SUBLANES - 1)
    n_blocks = -(-max_rows // MOE_BLOCK) + N_EXPERTS
    lrow, gates, tab, tot = _router(logits)
    tab = tab.reshape(n_tok_blocks, SUBLANES, LANES)[:, :, :N_EXPERTS]
    ntile, loff, prior = tab[:, 0], tab[:, 1], tab[:, 2]
    nblk = (tot[0, :N_EXPERTS] + MOE_BLOCK - 1) // MOE_BLOCK
    blk_end = jnp.cumsum(nblk)
    pad_start = ((blk_end - nblk) * MOE_BLOCK).astype(jnp.int32)
    gdest = (pad_start[None, :] + prior).astype(jnp.int32)
    n_used = blk_end[-1:].astype(jnp.int32)
    blk = jnp.minimum(jnp.arange(n_blocks, dtype=jnp.int32), n_used[0] - 1)
    block_e = jnp.minimum(jnp.sum(blk[:, None] >= blk_end[None, :], axis=1), N_EXPERTS - 1).astype(jnp.int32)
    ntile, loff, gdest = ntile.reshape(-1), loff.reshape(-1), gdest.reshape(-1)
    xb = _dispatch(h2, lrow, pad_start, nblk.astype(jnp.int32), ntile, loff, gdest, n_blocks)
    yb = _experts(xb, block_e, n_used, wg, wu, wd)
    return _combine(yb, lrow, ntile, loff, gdest, gates, x, mods6, layer, first_row, rows_per_batch, g3)


def _pad_cols(w, n):
    return jnp.pad(w, ((0, 0), (0, n - w.shape[1])))


def kernel(x_prompt, x_sample, state_mlstm_C, state_mlstm_n, state_mlstm_m, state_ret_S, c, c_ctx, mod_w, mod_b, norm_g, mlstm_w_in, mlstm_gate_b, mlstm_conv_w, mlstm_conv_b, mlstm_head_g, mlstm_w_out, ret_w_in, ret_decay_logit, ret_head_g, ret_w_out, ffn_w_gate, ffn_w_up, ffn_w_down, moe_router, moe_w_gate, moe_w_up, moe_w_down):
    bp, n_p, d = x_prompt.shape
    bs, n_s, _ = x_sample.shape
    depth = mod_w.shape[0]
    assert depth == 2 and CHUNK % GRID_W == 0 and n_s % GRID_W == 0

    cond = jnp.zeros((MOD_ROWS, d), F32).at[0].set(c_ctx).at[1:1 + bs].set(c)
    mods6 = _modulation(cond, mod_w, mod_b).reshape(depth, MOD_ROWS, N_MOD, 1, d)

    groups = [dict(x=x_prompt.reshape(bp * n_p, d), first=0, rpb=None, nseq=bp, ntok=n_p, prompt=True),
              dict(x=x_sample.reshape(bs * n_s, d), first=1, rpb=n_s // ROW_TILE, nseq=bs, ntok=n_s, prompt=False)]

    j = 0
    ml_qk = (mlstm_w_in.shape[2] - 4 * HEADS) // 2
    ml_v = ml_qk // 2
    w_in = mlstm_w_in[j]
    wqk = w_in[:, :ml_qk].astype(BF16)
    wv = w_in[:, ml_qk:ml_qk + ml_v].astype(BF16)
    wo = w_in[:, ml_qk + ml_v:ml_qk + 2 * ml_v].astype(BF16)
    w_gate = w_in[:, ml_qk + 2 * ml_v:]
    wgh, wgl = _split_bf16(_pad_cols(w_gate, LANES))
    bcol = _pad_cols(mlstm_gate_b[j][None, :], LANES)
    g = norm_g[0]
    w_out0 = mlstm_w_out[j].astype(BF16)
    new_c = new_n = new_m = None
    casts = (ffn_w_gate[j], ffn_w_up[j], ffn_w_down[j], ret_w_in[j], ret_w_out[j])
    proj = {}
    for grp in reversed(groups):
        args = (mods6, 0, grp["first"], grp["rpb"])
        outs = _proj_mlstm(grp["x"], grp["ntok"], *args, g[0:1], wqk, wv, wo, wgh, wgl, bcol, w_gate.shape[1],
                           mlstm_conv_w[j], mlstm_conv_b[j][None, :], casts=() if grp["prompt"] else casts)
        proj[grp["prompt"]] = outs[:6]
        if not grp["prompt"]:
            fwg, fwu, fwd, rw_in, w_out1 = outs[6:]
    for grp in groups:
        args = (mods6, 0, grp["first"], grp["rpb"])
        q, kt, v, o, gcol, grow = proj[grp["prompt"]]
        grow3 = grow.reshape(grow.shape[0], 1, grow.shape[1])
        if grp["prompt"]:
            m0 = jnp.zeros((grp["nseq"] * 2 * HEADS,), F32)
            hs, new_c, new_n, new_m = _mlstm_scan(q, kt, v, gcol, grow3, m0, None, grp["nseq"], grp["ntok"], True)
        else:
            (hs,) = _mlstm_scan(q, kt, v, gcol, grow3, state_mlstm_m[:, j].reshape(-1),
                                (state_mlstm_C[:, j], state_mlstm_n[:, j]), grp["nseq"], grp["ntok"], False)
        (grp["x"],) = _mixer_out(hs, o, mlstm_head_g[j][None, :], w_out0, grp["x"], *args, g[1:2], True,
                                 ffn=(g[2:3], g[3:4], fwg, fwu, fwd))

    ret_qk = ret_w_in.shape[2] // 3
    g = norm_g[1]
    wrh, wrl = _split_bf16(_pad_cols(moe_router[j], LANES))
    decay_flat = ret_decay_logit[j].reshape(-1)
    new_s = None
    n_exp, _, d_ff = moe_w_gate.shape[1:]
    casts = (moe_w_gate[j].reshape(n_exp * d, d_ff), moe_w_up[j].reshape(n_exp * d, d_ff),
             moe_w_down[j].reshape(n_exp * d_ff, d))
    proj = {}
    for grp in reversed(groups):
        args = (mods6, 1, grp["first"], grp["rpb"])
        rope_tabs = None if grp["prompt"] else _rope_tables(grp["ntok"], ret_qk // (2 * HEADS))
        outs = _proj_ret(grp["x"], grp["ntok"], *args, g[0:1], rw_in, ret_qk // 2, ret_qk, rope_tabs,
                         casts=() if grp["prompt"] else casts)
        proj[grp["prompt"]] = outs[:4]
        if not grp["prompt"]:
            ewg, ewu, ewd = (w.reshape(n_exp, -1, w.shape[1]) for w in outs[4:])
    for grp in groups:
        args = (mods6, 1, grp["first"], grp["rpb"])
        q, kt, v, gate = proj[grp["prompt"]]
        if grp["prompt"]:
            hs, new_s = _ret_scan(q, kt, v, decay_flat, None, grp["nseq"], grp["ntok"], True)
        else:
            (hs,) = _ret_scan(q, kt, v, decay_flat, state_ret_S[:, j], grp["nseq"], grp["ntok"], False)
        x1, h2, logits = _mixer_out(hs, gate, ret_head_g[j][None, :], w_out1, grp["x"], *args, g[1:2], False,
                                    router=(g[2:3], wrh, wrl))
        grp["x"] = _moe(h2, logits, x1, *args, g[3:4], ewg, ewu, ewd)

    y_prompt = groups[0]["x"].reshape(bp, n_p, d)
    y_sample = groups[1]["x"].reshape(bs, n_s, d)
    return (y_prompt, y_sample, new_c[:, None], new_n[:, None], new_m[:, None, :, :, 0], new_s[:, None])
```

```python
import functools
import math

import jax
import jax.numpy as jnp
from jax import lax
from jax.experimental import pallas as pl
from jax.experimental.pallas import tpu as pltpu

F32 = jnp.float32
BF16 = jnp.bfloat16

EPS = 1e-6
N_MOD = 6
HEADS = 4
CHUNK = 128
RET_CHUNK = 256
GRID_W = 64
ROPE_BASE = 10000.0
N_EXPERTS = 8
TOP_K = 2
MOE_BLOCK = 256
LANES = 128
SUBLANES = 8
ROW_TILE = 256
MOD_ROWS = 8
STAGE_ROWS = -(-(TOP_K * ROW_TILE + N_EXPERTS * (SUBLANES - 1)) // LANES) * LANES
MIB = 1024 * 1024


def _cparams(sem, vmem_mib):
    return pltpu.CompilerParams(dimension_semantics=sem, vmem_limit_bytes=vmem_mib * MIB)


def _dot(a, b):
    return jnp.dot(a, b, preferred_element_type=F32)


def _dot_nt(a, b):
    return lax.dot_general(a, b, (((1,), (1,)), ((), ())), preferred_element_type=F32)


def _dot_tn(a, b):
    return lax.dot_general(a, b, (((0,), (0,)), ((), ())), preferred_element_type=F32)


def _split_bf16(x):
    hi = x.astype(BF16)
    lo = (x - hi.astype(F32)).astype(BF16)
    return hi, lo


def _rms(x, g):
    return x * lax.rsqrt(jnp.mean(x * x, -1, keepdims=True) + EPS) * g


def _adaln(x, g, shift, scale):
    return _rms(x, g) * (1.0 + scale) + shift


def _silu(x):
    return x * jax.nn.sigmoid(x)


def _logsig(x):
    return jnp.minimum(x, 0.0) - jnp.log1p(jnp.exp(-jnp.abs(x)))


def _const_spec(shape):
    nd = len(shape)
    return pl.BlockSpec(shape, lambda *_: (0,) * nd, pipeline_mode=pl.Buffered(1))


def _mod_spec(d, layer, comp, rows_per_batch, first_row):
    if rows_per_batch is None:
        return pl.BlockSpec((None, None, None, 1, d), lambda i: (layer, first_row, comp, 0, 0))
    return pl.BlockSpec((None, None, None, 1, d), lambda i: (layer, first_row + i // rows_per_batch, comp, 0, 0))


def _mod_kernel(c_ref, w_ref, b_ref, o_ref):
    s = _silu(c_ref[...]).astype(BF16)
    o_ref[...] = _dot(s, w_ref[...].astype(BF16)) + b_ref[...]


def _modulation(cond, mod_w, mod_b):
    depth, d, n = mod_w.shape
    tn = n // 4
    return pl.pallas_call(
        _mod_kernel,
        grid=(depth, n // tn),
        in_specs=[pl.BlockSpec((MOD_ROWS, d), lambda l, j: (0, 0)),
                  pl.BlockSpec((None, d, tn), lambda l, j: (l, 0, j)),
                  pl.BlockSpec((None, 1, tn), lambda l, j: (l, 0, j))],
        out_specs=pl.BlockSpec((None, MOD_ROWS, tn), lambda l, j: (l, 0, j)),
        out_shape=jax.ShapeDtypeStruct((depth, MOD_ROWS, n), F32),
        compiler_params=_cparams(("arbitrary", "arbitrary"), 40),
        name="modulation",
    )(cond, mod_w, mod_b.reshape(depth, 1, n))


BF16_ROWS = 16


def _cast_plan(arrays, n_steps):
    in_specs, out_specs, out_shape, nblks = [], [], [], []
    for a in arrays:
        rows, cols = a.shape
        nblk = max(n for n in range(1, n_steps + 1) if rows % n == 0 and (rows // n) % BF16_ROWS == 0)
        spec = pl.BlockSpec((rows // nblk, cols), lambda i, nb=nblk: (jnp.minimum(i, nb - 1), 0))
        in_specs.append(spec)
        out_specs.append(spec)
        out_shape.append(jax.ShapeDtypeStruct(a.shape, BF16))
        nblks.append(nblk)
    return in_specs, out_specs, out_shape, nblks


def _cast_slabs(srcs, dsts, nblks):
    for src, dst, nb in zip(srcs, dsts, nblks):
        @pl.when(pl.program_id(0) < nb)
        def _():
            dst[...] = src[...].astype(BF16)


def _proj_mlstm_kernel(tiles_per_seq, cast_nblks, *refs):
    nc = len(cast_nblks)
    (x_ref, xp_ref, xn_ref, g_ref, sh_ref, sc_ref, wqk_ref, wv_ref, wo_ref, wgh_ref, wgl_ref, bcol_ref, cw_ref,
     cb_ref) = refs[:14]
    q_ref, kt_ref, v_ref, o_ref, gcol_ref, grow_ref = refs[14 + nc:20 + nc]
    _cast_slabs(refs[14:14 + nc], refs[20 + nc:], cast_nblks)
    tm = x_ref.shape[0]
    nq = q_ref.shape[1]
    h = _adaln(x_ref[...], g_ref[...], sh_ref[...], sc_ref[...])
    hb, hl = _split_bf16(h)
    x_halo = jnp.concatenate([xp_ref[...], xn_ref[...]], axis=0)
    hb_halo = _adaln(x_halo, g_ref[...], sh_ref[...], sc_ref[...]).astype(BF16)
    hb_all = jnp.concatenate([hb, hb_halo], axis=0)
    pos = pl.program_id(0) % tiles_per_seq
    has_prev = jnp.where(pos > 0, 1.0, 0.0)
    has_next = jnp.where(pos < tiles_per_seq - 1, 1.0, 0.0)
    wc = nq // HEADS
    rowi = lax.broadcasted_iota(jnp.int32, (tm, wc), 0)
    for c in range(2 * HEADS):
        sl = slice(c * wc, (c + 1) * wc)
        p_all = _dot(hb_all, wqk_ref[:, sl])
        p = p_all[:tm]
        p_prev = p_all[tm + SUBLANES - 1:tm + SUBLANES, :] * has_prev
        p_next = p_all[tm + SUBLANES:tm + SUBLANES + 1, :] * has_next
        prev = jnp.where(rowi == 0, p_prev, pltpu.roll(p, 1, axis=0))
        nxt = jnp.where(rowi == tm - 1, p_next, pltpu.roll(p, tm - 1, axis=0))
        y = prev * cw_ref[0:1, sl] + p * cw_ref[1:2, sl] + nxt * cw_ref[2:3, sl] + cb_ref[:, sl]
        if c < HEADS:
            q_ref[:, sl] = (y * wc ** -0.5).astype(BF16)
        else:
            kt_ref[(c - HEADS) * wc:(c - HEADS + 1) * wc, :] = y.T.astype(BF16)
    v_ref[...] = _dot(hb, wv_ref[...]).astype(BF16)
    o_ref[...] = jax.nn.sigmoid(_dot(hb, wo_ref[...]))
    gc = _dot(hb, wgh_ref[...]) + _dot(hl, wgh_ref[...]) + _dot(hb, wgl_ref[...]) + bcol_ref[...]
    for hh in range(HEADS):
        gcol_ref[hh] = gc if hh == 0 else pltpu.roll(gc, LANES - hh, axis=1)
    grow_ref[...] = gc.T[:grow_ref.shape[0], :]


def _proj_mlstm(x, n_tok, mods6, layer, first_row, rows_per_batch, g, wqk, wv, wo, wgh, wgl, bcol, ng, conv_w, conv_b,
                casts=()):
    t, d = x.shape
    tm = ROW_TILE
    ms = lambda comp: _mod_spec(d, layer, comp, rows_per_batch, first_row)
    nq = wqk.shape[1] // 2
    tps = tm // SUBLANES
    last = t // SUBLANES - 1
    c_in, c_out, c_shape, c_nblks = _cast_plan(casts, t // tm)
    return pl.pallas_call(
        functools.partial(_proj_mlstm_kernel, n_tok // tm, tuple(c_nblks)),
        grid=(t // tm,),
        in_specs=[pl.BlockSpec((tm, d), lambda i: (i, 0)),
                  pl.BlockSpec((SUBLANES, d), lambda i: (jnp.maximum(i * tps - 1, 0), 0)),
                  pl.BlockSpec((SUBLANES, d), lambda i: (jnp.minimum((i + 1) * tps, last), 0)),
                  _const_spec((1, d)), ms(0), ms(1),
                  _const_spec(wqk.shape), _const_spec(wv.shape), _const_spec(wo.shape),
                  _const_spec(wgh.shape), _const_spec(wgl.shape), _const_spec(bcol.shape),
                  _const_spec(conv_w.shape), _const_spec(conv_b.shape)] + c_in,
        out_specs=[pl.BlockSpec((tm, nq), lambda i: (i, 0)),
                   pl.BlockSpec((nq, tm), lambda i: (0, i)),
                   pl.BlockSpec((tm, wv.shape[1]), lambda i: (i, 0)),
                   pl.BlockSpec((tm, wo.shape[1]), lambda i: (i, 0)),
                   pl.BlockSpec((HEADS, tm, LANES), lambda i: (0, i, 0)),
                   pl.BlockSpec((ng, tm), lambda i: (0, i))] + c_out,
        out_shape=[jax.ShapeDtypeStruct((t, nq), BF16),
                   jax.ShapeDtypeStruct((nq, t), BF16),
                   jax.ShapeDtypeStruct((t, wv.shape[1]), BF16),
                   jax.ShapeDtypeStruct((t, wo.shape[1]), F32),
                   jax.ShapeDtypeStruct((HEADS, t, LANES), F32),
                   jax.ShapeDtypeStruct((ng, t), F32)] + c_shape,
        compiler_params=_cparams(("arbitrary",), 56),
        name="proj_mlstm",
    )(x, x, x, g, mods6, mods6, wqk, wv, wo, wgh, wgl, bcol, conv_w, conv_b, *casts)


def _make_mlstm_scan_kernel(n_tok, dk, dv, has_state, emit_state):
    L = CHUNK
    nc = n_tok // L
    assert nc % 2 == 0 and L == LANES

    def kernel(*refs):
        it = iter(refs)
        m0_ref = next(it)
        q_ref, kt_ref, v_ref, gcol_ref = next(it), next(it), next(it), next(it)
        gi_refs = (next(it), next(it))
        if has_state:
            c0_ref, n0_ref = next(it), next(it)
        out_ref = next(it)
        if emit_state:
            cout_ref, nout_ref, mout_ref = next(it), next(it), next(it)
        brep, rmrep, rrow, stats, cst, cbf, nrep, nbf = (next(it) for _ in range(8))

        b = pl.program_id(0)
        hh = pl.program_id(1)

        ri = lax.broadcasted_iota(jnp.int32, (L, L), 0)
        ci = lax.broadcasted_iota(jnp.int32, (L, L), 1)
        lower = ri >= ci
        upper = ri <= ci
        masks = ((lower, upper), (upper, lower))

        tri = tuple(jnp.where(masks[d][0], 1.0, 0.0).astype(BF16) for d in range(2))
        rowid = lax.broadcasted_iota(jnp.int32, (L, LANES), 0)

        def running_max(x, reverse):
            s = 1
            while s < L:
                if reverse:
                    x = jnp.maximum(x, jnp.where(rowid < L - s, pltpu.roll(x, L - s, axis=0), -jnp.inf))
                else:
                    x = jnp.maximum(x, jnp.where(rowid >= s, pltpu.roll(x, s, axis=0), -jnp.inf))
                s *= 2
            return x

        def prep_chunk(c, _):
            r0 = pl.multiple_of(c * L, L)
            gc = gcol_ref[pl.ds(r0, L), :]
            for d in range(2):
                mk, mkt = masks[d]
                fr = jnp.broadcast_to(_logsig(gc[:, 8 * d + 4:8 * d + 5]), (L, LANES))
                ir = jnp.broadcast_to(gc[:, 8 * d:8 * d + 1], (L, LANES))
                f1 = fr.astype(BF16)
                e1 = fr - f1.astype(F32)
                f2 = e1.astype(BF16)
                f3 = (e1 - f2.astype(F32)).astype(BF16)
                b_rep = _dot(tri[d], f1) + _dot(tri[d], f2) + _dot(tri[d], f3)
                rm_rep = running_max(ir - b_rep, reverse=(d == 1))
                b_row = jnp.sum(jnp.where(mkt, fr, 0.0), axis=0, keepdims=True)
                brep[d, pl.ds(r0, L), :] = b_rep
                rmrep[d, pl.ds(r0, L), :] = rm_rep
                rrow[d, :, pl.ds(r0, L)] = gi_refs[d][:, pl.ds(r0, L)] - b_row
                end = 0 if d == 1 else L - 1
                stats[c, pl.ds(2 * d, 1), :] = b_rep[end:end + 1, :]
                stats[c, pl.ds(2 * d + 1, 1), :] = rm_rep[end:end + 1, :]
            return 0

        lax.fori_loop(0, nc, prep_chunk, 0, unroll=2)

        if has_state:
            cst[...] = c0_ref[...]
            for d in range(2):
                nrep[d] = jnp.broadcast_to(n0_ref[d], (dk, LANES))
        else:
            cst[...] = jnp.zeros_like(cst)
            nrep[...] = jnp.zeros_like(nrep)
        cbf[...] = cst[...].astype(BF16)
        nbf[...] = nrep[...].astype(BF16)
        m_init = tuple(jnp.full((1, LANES), m0_ref[(b * 2 + d) * HEADS + hh], F32) for d in range(2))

        def lanes(x, n):
            return jnp.concatenate([x] * (n // LANES), axis=1)

        def chunk(d, c, m):
            mk = masks[d][0]
            r0 = pl.multiple_of(c * L, L)
            q = q_ref[pl.ds(r0, L), :]
            kt = kt_ref[:, pl.ds(r0, L)]
            v = v_ref[pl.ds(r0, L), :]
            rr = rrow[d, :, pl.ds(r0, L)]
            st = stats[c]
            b_end, rm_end = st[2 * d:2 * d + 1, :], st[2 * d + 1:2 * d + 2, :]
            mm = jnp.maximum(m, rmrep[d, pl.ds(r0, L), :])
            qkn = _dot(q, jnp.concatenate([kt, nbf[d]], axis=1))
            s = qkn[:, :L] * jnp.where(mk, jnp.exp(rr - mm), 0.0)
            w_inter = jnp.exp(m - mm)
            num = lanes(w_inter, dv) * _dot(q, cbf[d]) + _dot(s.astype(BF16), v)
            den = w_inter * qkn[:, L:] + jnp.sum(s, axis=1, keepdims=True)
            inv = 1.0 / jnp.maximum(jnp.abs(den), jnp.exp(-(brep[d, pl.ds(r0, L), :] + mm)))
            h = num * lanes(inv, dv)
            m_end = jnp.maximum(m, rm_end)
            decay = jnp.exp(m - m_end)
            kw = kt.astype(F32) * jnp.exp(rr - m_end)
            c_new = lanes(decay, dv) * cst[d] + _dot(kw.astype(BF16), v)
            n_new = decay * nrep[d] + jnp.sum(kw, axis=1, keepdims=True)
            cst[d] = c_new
            cbf[d] = c_new.astype(BF16)
            nrep[d] = n_new
            nbf[d] = n_new.astype(BF16)
            return h, b_end + m_end

        def make_body(accumulate):
            def body(i, carry):
                mf, mb = carry
                hf, mf = chunk(0, i, mf)
                rf = pl.multiple_of(i * L, L)
                cb = nc - 1 - i
                hb, mb = chunk(1, cb, mb)
                rb = pl.multiple_of(cb * L, L)
                if accumulate:
                    out_ref[pl.ds(rf, L), :] += hf
                    out_ref[pl.ds(rb, L), :] += hb
                else:
                    out_ref[pl.ds(rf, L), :] = hf
                    out_ref[pl.ds(rb, L), :] = hb
                return mf, mb
            return body

        unroll = 2 if nc % 4 == 0 else 1
        carry = lax.fori_loop(0, nc // 2, make_body(False), m_init, unroll=unroll)
        carry = lax.fori_loop(nc // 2, nc, make_body(True), carry, unroll=unroll)

        if emit_state:
            cout_ref[...] = cst[...]
            for d in range(2):
                nout_ref[d, pl.ds(hh, 1), :] = nrep[d].T[0:1, :]
                mout_ref[d, pl.ds(hh, 1), :] = carry[d]

    return kernel


def _mlstm_scan(q, kt, v, gcol, grow3, m0_flat, state, n_seq, n_tok, emit_state):
    dk = q.shape[1] // HEADS
    dv = v.shape[1] // HEADS
    has_state = state is not None
    kern = _make_mlstm_scan_kernel(n_tok, dk, dv, has_state, emit_state)
    grow_spec = lambda r: pl.BlockSpec((None, 1, n_tok), lambda b, h, m: (r * HEADS + h, 0, b))
    in_specs = [pl.BlockSpec((n_tok, dk), lambda b, h, m: (b, h)),
                pl.BlockSpec((dk, n_tok), lambda b, h, m: (h, b)),
                pl.BlockSpec((n_tok, dv), lambda b, h, m: (b, h)),
                pl.BlockSpec((None, n_tok, LANES), lambda b, h, m: (h, b, 0)),
                grow_spec(0), grow_spec(2)]
    args = [q, kt, v, gcol, grow3, grow3]
    if has_state:
        in_specs += [pl.BlockSpec((None, 2, None, dk, dv), lambda b, h, m: (b, 0, h, 0, 0)),
                     pl.BlockSpec((None, 2, None, dk, 1), lambda b, h, m: (b, 0, h, 0, 0))]
        args += [state[0], state[1][..., None]]
    out_specs = [pl.BlockSpec((n_tok, dv), lambda b, h, m: (b, h))]
    out_shape = [jax.ShapeDtypeStruct((n_seq * n_tok, HEADS * dv), F32)]
    if emit_state:
        out_specs += [pl.BlockSpec((None, 2, None, dk, dv), lambda b, h, m: (b, 0, h, 0, 0)),
                      pl.BlockSpec((None, 2, HEADS, dk), lambda b, h, m: (b, 0, 0, 0)),
                      pl.BlockSpec((None, 2, HEADS, LANES), lambda b, h, m: (b, 0, 0, 0))]
        out_shape += [jax.ShapeDtypeStruct((n_seq, 2, HEADS, dk, dv), F32),
                      jax.ShapeDtypeStruct((n_seq, 2, HEADS, dk), F32),
                      jax.ShapeDtypeStruct((n_seq, 2, HEADS, LANES), F32)]
    grid_spec = pltpu.PrefetchScalarGridSpec(
        num_scalar_prefetch=1, grid=(n_seq, HEADS), in_specs=in_specs, out_specs=out_specs,
        scratch_shapes=[pltpu.VMEM((2, n_tok, LANES), F32),
                        pltpu.VMEM((2, n_tok, LANES), F32),
                        pltpu.VMEM((2, 1, n_tok), F32),
                        pltpu.VMEM((n_tok // CHUNK, SUBLANES, LANES), F32),
                        pltpu.VMEM((2, dk, dv), F32), pltpu.VMEM((2, dk, dv), BF16),
                        pltpu.VMEM((2, dk, LANES), F32), pltpu.VMEM((2, dk, LANES), BF16)])
    return pl.pallas_call(
        kern, grid_spec=grid_spec, out_shape=out_shape,
        compiler_params=_cparams(("arbitrary", "arbitrary"), 56),
        name="mlstm_scan",
    )(m0_flat, *args)


def _proj_ret_kernel(rope, cast_nblks, *refs):
    nc = len(cast_nblks)
    n_in = 5 + (4 if rope else 0)
    x_ref, g_ref, sh_ref, sc_ref, w_ref = refs[:5]
    if rope:
        rcos_ref, rsin_ref, ccos_ref, csin_ref = refs[5:9]
    q_ref, kt_ref, v_ref, gate_ref = refs[n_in + nc:n_in + nc + 4]
    _cast_slabs(refs[n_in:n_in + nc], refs[n_in + nc + 4:], cast_nblks)
    nq, nv = q_ref.shape[1], v_ref.shape[1]
    wc = nq // HEADS
    hb = _adaln(x_ref[...], g_ref[...], sh_ref[...], sc_ref[...]).astype(BF16)
    if rope:
        cos = jnp.concatenate([rcos_ref[...], ccos_ref[...]], axis=1)
        sin = jnp.concatenate([rsin_ref[...], csin_ref[...]], axis=1)
    for c in range(2 * HEADS):
        p = _dot(hb, w_ref[:, c * wc:(c + 1) * wc])
        if rope:
            swapped = jnp.concatenate([pltpu.roll(p[:, j * LANES:(j + 1) * LANES], LANES // 2, axis=1)
                                       for j in range(wc // LANES)], axis=1)
            p = p * cos + swapped * sin
        if c < HEADS:
            q_ref[:, c * wc:(c + 1) * wc] = p.astype(BF16)
        else:
            kt_ref[(c - HEADS) * wc:(c - HEADS + 1) * wc, :] = (p * wc ** -0.5).T.astype(BF16)
    v_ref[...] = _dot(hb, w_ref[:, 2 * nq:2 * nq + nv]).astype(BF16)
    gate_ref[...] = _silu(_dot(hb, w_ref[:, 2 * nq + nv:])).astype(BF16)


def _proj_ret(x, n_tok, mods6, layer, first_row, rows_per_batch, g, w_in, nq, nv, rope_tabs, casts=()):
    t, d = x.shape
    tm = ROW_TILE
    ms = lambda comp: _mod_spec(d, layer, comp, rows_per_batch, first_row)
    ng = w_in.shape[1] - 2 * nq - nv
    rope = rope_tabs is not None
    in_specs = [pl.BlockSpec((tm, d), lambda i: (i, 0)), _const_spec((1, d)), ms(0), ms(1), _const_spec(w_in.shape)]
    args = [x, g, mods6, mods6, w_in]
    if rope:
        tiles_per_seq = n_tok // tm
        row_spec = pl.BlockSpec((tm, LANES), lambda i: (i % tiles_per_seq, 0))
        in_specs += [row_spec, row_spec, _const_spec((tm, LANES)), _const_spec((tm, LANES))]
        args += list(rope_tabs)
    c_in, c_out, c_shape, c_nblks = _cast_plan(casts, t // tm)
    return pl.pallas_call(
        functools.partial(_proj_ret_kernel, rope, tuple(c_nblks)),
        grid=(t // tm,),
        in_specs=in_specs + c_in,
        out_specs=[pl.BlockSpec((tm, nq), lambda i: (i, 0)),
                   pl.BlockSpec((nq, tm), lambda i: (0, i)),
                   pl.BlockSpec((tm, nv), lambda i: (i, 0)),
                   pl.BlockSpec((tm, ng), lambda i: (i, 0))] + c_out,
        out_shape=[jax.ShapeDtypeStruct((t, nq), BF16),
                   jax.ShapeDtypeStruct((nq, t), BF16),
                   jax.ShapeDtypeStruct((t, nv), BF16),
                   jax.ShapeDtypeStruct((t, ng), BF16)] + c_shape,
        compiler_params=_cparams(("arbitrary",), 56),
        name="proj_ret",
    )(*args, *casts)


def _make_ret_scan_kernel(n_tok, dk, dv, has_state, emit_state):
    L = RET_CHUNK
    nc = n_tok // L
    assert nc == 1 or nc % 2 == 0

    def kernel(*refs):
        it = iter(refs)
        dl_ref = next(it)
        q_ref, kt_ref, v_ref = next(it), next(it), next(it)
        if has_state:
            s0_ref = next(it)
        out_ref = next(it)
        if emit_state:
            sout_ref = next(it)
        sst, sbf = next(it), next(it)
        if nc > 1:
            acc = next(it)

        hh = pl.program_id(1)
        ri = lax.broadcasted_iota(jnp.int32, (L, L), 0)
        ci = lax.broadcasted_iota(jnp.int32, (L, L), 1)
        rel = (ri - ci).astype(F32)
        pos_col = lax.broadcasted_iota(jnp.int32, (L, LANES), 0).astype(F32)
        pos_row = lax.broadcasted_iota(jnp.int32, (1, L), 1).astype(F32)

        dmat, q_dec, k_dec, c_dec = [], [], [], []
        for d in range(2):
            lg = _logsig(jnp.full((1, 1), dl_ref[d * HEADS + hh], F32))
            if d == 0:
                dmat.append(jnp.where(ri >= ci, jnp.exp(lg * jnp.maximum(rel, 0.0)), 0.0))
                q_dec.append(jnp.exp(lg * (pos_col + 1.0)))
                k_dec.append(jnp.exp(lg * (L - 1.0 - pos_row)))
            else:
                dmat.append(jnp.where(ri <= ci, jnp.exp(lg * jnp.maximum(-rel, 0.0)), 0.0))
                q_dec.append(jnp.exp(lg * (L - pos_col)))
                k_dec.append(jnp.exp(lg * pos_row))
            c_dec.append(jnp.broadcast_to(jnp.exp(lg * float(L)), (1, LANES)))

        if has_state:
            sst[...] = s0_ref[...]
        else:
            sst[...] = jnp.zeros_like(sst)
        sbf[...] = sst[...].astype(BF16)

        def lanes(x, n):
            return jnp.concatenate([x] * (n // LANES), axis=1)

        def chunk(d, c):
            r0 = c * L if isinstance(c, int) else pl.multiple_of(c * L, L)
            q = q_ref[pl.ds(r0, L), :]
            kt = kt_ref[:, pl.ds(r0, L)]
            v = v_ref[pl.ds(r0, L), :]
            s = _dot(q, kt) * dmat[d]
            out = _dot(s.astype(BF16), v) + lanes(q_dec[d], dv) * _dot(q, sbf[d])
            s_new = lanes(c_dec[d], dv) * sst[d] + _dot((kt.astype(F32) * k_dec[d]).astype(BF16), v)
            sst[d] = s_new
            sbf[d] = s_new.astype(BF16)
            return out

        def body1(i, _):
            rf = pl.multiple_of(i * L, L)
            acc[pl.ds(rf, L), :] = chunk(0, i)
            cb = nc - 1 - i
            rb = pl.multiple_of(cb * L, L)
            acc[pl.ds(rb, L), :] = chunk(1, cb)
            return 0

        def body2(i, _):
            rf = pl.multiple_of(i * L, L)
            out_ref[pl.ds(rf, L), :] = (acc[pl.ds(rf, L), :] + chunk(0, i)).astype(BF16)
            cb = nc - 1 - i
            rb = pl.multiple_of(cb * L, L)
            out_ref[pl.ds(rb, L), :] = (acc[pl.ds(rb, L), :] + chunk(1, cb)).astype(BF16)
            return 0

        if nc == 1:
            out_ref[...] = (chunk(0, 0) + chunk(1, 0)).astype(BF16)
        else:
            unroll = 2 if nc % 4 == 0 else 1
            lax.fori_loop(0, nc // 2, body1, 0, unroll=unroll)
            lax.fori_loop(nc // 2, nc, body2, 0, unroll=unroll)
        if emit_state:
            sout_ref[...] = sst[...]

    return kernel


def _ret_scan(q, kt, v, decay_flat, state, n_seq, n_tok, emit_state):
    dk = q.shape[1] // HEADS
    dv = v.shape[1] // HEADS
    has_state = state is not None
    kern = _make_ret_scan_kernel(n_tok, dk, dv, has_state, emit_state)
    in_specs = [pl.BlockSpec((n_tok, dk), lambda b, h, m: (b, h)),
                pl.BlockSpec((dk, n_tok), lambda b, h, m: (h, b)),
                pl.BlockSpec((n_tok, dv), lambda b, h, m: (b, h))]
    args = [q, kt, v]
    if has_state:
        in_specs.append(pl.BlockSpec((None, 2, None, dk, dv), lambda b, h, m: (b, 0, h, 0, 0)))
        args.append(state)
    out_specs = [pl.BlockSpec((n_tok, dv), lambda b, h, m: (b, h))]
    out_shape = [jax.ShapeDtypeStruct((n_seq * n_tok, HEADS * dv), BF16)]
    if emit_state:
        out_specs.append(pl.BlockSpec((None, 2, None, dk, dv), lambda b, h, m: (b, 0, h, 0, 0)))
        out_shape.append(jax.ShapeDtypeStruct((n_seq, 2, HEADS, dk, dv), F32))
    grid_spec = pltpu.PrefetchScalarGridSpec(
        num_scalar_prefetch=1, grid=(n_seq, HEADS), in_specs=in_specs, out_specs=out_specs,
        scratch_shapes=[pltpu.VMEM((2, dk, dv), F32), pltpu.VMEM((2, dk, dv), BF16)]
        + ([pltpu.VMEM((n_tok, dv), F32)] if n_tok > RET_CHUNK else []))
    return pl.pallas_call(
        kern, grid_spec=grid_spec, out_shape=out_shape,
        compiler_params=_cparams(("arbitrary", "arbitrary"), 56),
        name="ret_scan",
    )(decay_flat, *args)


def _rope_tables(n_tok, dk):
    r = dk // 4
    inv = 1.0 / (ROPE_BASE ** (jnp.arange(r, dtype=F32) / r))
    sign = jnp.concatenate([-jnp.ones((r,), F32), jnp.ones((r,), F32)])
    rows = (jnp.arange(n_tok) // GRID_W).astype(F32)[:, None] * inv
    cols = (jnp.arange(ROW_TILE) % GRID_W).astype(F32)[:, None] * inv
    two = lambda a: jnp.concatenate([a, a], axis=-1)
    return (two(jnp.cos(rows)), two(jnp.sin(rows)) * sign, two(jnp.cos(cols)), two(jnp.sin(cols)) * sign)


def _make_mixer_out_kernel(dv, with_router, with_ffn):
    def kernel(*refs):
        it = iter(refs)
        hs_ref, gate_ref, hg_ref, w_ref, x_ref, g1_ref, ga_ref = (next(it) for _ in range(7))
        if with_router:
            g2_ref, fsh_ref, fsc_ref, wrh_ref, wrl_ref = (next(it) for _ in range(5))
        if with_ffn:
            g2_ref, fsh_ref, fsc_ref, wg_ref, wu_ref, wd_ref, g3_ref, fga_ref = (next(it) for _ in range(8))
        xo_ref = next(it)
        if with_router:
            h2_ref, lg_ref = next(it), next(it)
        z_ref = next(it)
        for hh in range(HEADS):
            sl = slice(hh * dv, (hh + 1) * dv)
            seg = hs_ref[:, sl].astype(F32)
            y = seg * lax.rsqrt(jnp.mean(seg * seg, -1, keepdims=True) + EPS) * hg_ref[:, sl]
            z_ref[:, sl] = (gate_ref[:, sl].astype(F32) * y).astype(BF16)
        y = _dot(z_ref[...], w_ref[...])
        xn = x_ref[...] + ga_ref[...] * _rms(y, g1_ref[...])
        if with_ffn:
            hb = _adaln(xn, g2_ref[...], fsh_ref[...], fsc_ref[...]).astype(BF16)
            a = (_silu(_dot(hb, wg_ref[...])) * _dot(hb, wu_ref[...])).astype(BF16)
            xn = xn + fga_ref[...] * _rms(_dot(a, wd_ref[...]), g3_ref[...])
        xo_ref[...] = xn
        if with_router:
            h2 = _adaln(xn, g2_ref[...], fsh_ref[...], fsc_ref[...])
            h2_ref[...] = h2
            hb, hl = _split_bf16(h2)
            lg_ref[...] = _dot(hb, wrh_ref[...]) + _dot(hl, wrh_ref[...]) + _dot(hb, wrl_ref[...])
    return kernel


def _mixer_out(hs, gate, head_g, w_out, x, mods6, layer, first_row, rows_per_batch, g1, router=None, ffn=None):
    t, d = x.shape
    vdim = hs.shape[1]
    tm = ROW_TILE
    ms = lambda comp: _mod_spec(d, layer, comp, rows_per_batch, first_row)
    with_router = router is not None
    with_ffn = ffn is not None
    in_specs = [pl.BlockSpec((tm, vdim), lambda i: (i, 0)),
                pl.BlockSpec((tm, vdim), lambda i: (i, 0)),
                _const_spec((1, vdim)), _const_spec(w_out.shape),
                pl.BlockSpec((tm, d), lambda i: (i, 0)), _const_spec((1, d)), ms(2)]
    args = [hs, gate, head_g, w_out, x, g1, mods6]
    out_specs = [pl.BlockSpec((tm, d), lambda i: (i, 0))]
    out_shape = [jax.ShapeDtypeStruct((t, d), F32)]
    if with_router:
        g2, wrh, wrl = router
        in_specs += [_const_spec((1, d)), ms(3), ms(4), _const_spec(wrh.shape), _const_spec(wrl.shape)]
        args += [g2, mods6, mods6, wrh, wrl]
        out_specs += [pl.BlockSpec((tm, d), lambda i: (i, 0)), pl.BlockSpec((tm, LANES), lambda i: (i, 0))]
        out_shape += [jax.ShapeDtypeStruct((t, d), F32), jax.ShapeDtypeStruct((t, LANES), F32)]
    if with_ffn:
        g2, g3, wg, wu, wd = ffn
        in_specs += [_const_spec((1, d)), ms(3), ms(4), _const_spec(wg.shape), _const_spec(wu.shape),
                     _const_spec(wd.shape), _const_spec((1, d)), ms(5)]
        args += [g2, mods6, mods6, wg, wu, wd, g3, mods6]
    return pl.pallas_call(
        _make_mixer_out_kernel(vdim // HEADS, with_router, with_ffn),
        grid=(t // tm,), in_specs=in_specs, out_specs=out_specs, out_shape=out_shape,
        scratch_shapes=[pltpu.VMEM((tm, vdim), BF16)],
        compiler_params=_cparams(("arbitrary",), 48),
        name="mixer_ffn" if with_ffn else "mixer_out",
    )(*args)


def _router_kernel(lg_ref, lrow_ref, gt_ref, tab_ref, tot_ref, carry):
    tb = lg_ref.shape[0]

    @pl.when(pl.program_id(0) == 0)
    def _():
        carry[...] = jnp.zeros_like(carry)

    lane = lax.broadcasted_iota(jnp.int32, (tb, LANES), 1)
    lg = jnp.where(lane < N_EXPERTS, lg_ref[...], -jnp.inf)
    v1 = jnp.max(lg, axis=1, keepdims=True)
    i1 = jnp.min(jnp.where(lg == v1, lane, LANES), axis=1, keepdims=True)
    lg2 = jnp.where(lane == i1, -jnp.inf, lg)
    v2 = jnp.max(lg2, axis=1, keepdims=True)
    i2 = jnp.min(jnp.where(lg2 == v2, lane, LANES), axis=1, keepdims=True)
    ex = jnp.exp(v2 - v1)
    den = 1.0 + ex
    g1 = 1.0 / den
    g2 = ex / den
    oh1 = lane == i1
    oh2 = lane == i2
    onehot = jnp.where(oh1 | oh2, 1.0, 0.0)
    ri = lax.broadcasted_iota(jnp.int32, (tb, tb), 0)
    ci = lax.broadcasted_iota(jnp.int32, (tb, tb), 1)
    before = jnp.where(ri > ci, 1.0, 0.0).astype(BF16)
    rank = _dot(before, onehot.astype(BF16))
    tiles = jnp.ceil(jnp.sum(onehot, axis=0, keepdims=True) * (1.0 / SUBLANES))
    ei = lax.broadcasted_iota(jnp.int32, (LANES, LANES), 0)
    ej = lax.broadcasted_iota(jnp.int32, (LANES, LANES), 1)
    earlier = jnp.where(ei < ej, 1.0, 0.0).astype(BF16)
    off = _dot(jnp.broadcast_to(tiles, (SUBLANES, LANES)).astype(BF16), earlier)[0:1] * SUBLANES
    pos = rank + off
    r1 = jnp.sum(jnp.where(oh1, pos, 0.0), axis=1, keepdims=True)
    r2 = jnp.sum(jnp.where(oh2, pos, 0.0), axis=1, keepdims=True)
    lrow_ref[...] = jnp.where(lane == 0, r1, jnp.where(lane == 1, r2, 0.0)).astype(jnp.int32)
    gt_ref[...] = jnp.where(lane == 0, g1, jnp.where(lane == 1, g2, 0.0))
    sub = lax.broadcasted_iota(jnp.int32, (SUBLANES, LANES), 0)
    tab = jnp.where(sub == 0, tiles, jnp.where(sub == 1, off, jnp.where(sub == 2, carry[...], 0.0)))
    tab_ref[...] = tab.astype(jnp.int32)
    carry[...] = carry[...] + tiles * SUBLANES
    tot_ref[...] = carry[...].astype(jnp.int32)


RUN_TILES = 8


def _tile_copies(nt_ref, lo_ref, gd_ref, blk, make_copy, wait):
    def go(lo, gd, rows):
        cp = make_copy(pl.multiple_of(lo, SUBLANES), pl.multiple_of(gd, SUBLANES), rows)
        if wait:
            cp.wait()
        else:
            cp.start()

    for e in range(N_EXPERTS):
        idx = blk * N_EXPERTS + e
        nt, lo, gd = nt_ref[idx], lo_ref[idx], gd_ref[idx]

        def run(j, _):
            off = j * (RUN_TILES * SUBLANES)
            go(lo + off, gd + off, RUN_TILES * SUBLANES)
            return 0

        lax.fori_loop(0, nt // RUN_TILES, run, 0)
        k = RUN_TILES // 2
        while k >= 1:
            @pl.when(nt % (2 * k) >= k)
            def _():
                off = (nt // (2 * k)) * (2 * k) * SUBLANES
                go(lo + off, gd + off, k * SUBLANES)
            k //= 2


def _make_dispatch_kernel(tb, n_blocks):
    def kernel(ps_ref, nb_ref, nt_ref, lo_ref, gd_ref, lrow_ref, h_ref, xb_ref, xs, zbuf, sem):
        i = pl.program_id(0)

        @pl.when(i == 0)
        def _():
            zbuf[...] = jnp.zeros_like(zbuf)

            def zero_block(row0):
                dst = pl.multiple_of(row0, MOE_BLOCK)
                cp = pltpu.make_async_copy(zbuf, xb_ref.at[pl.ds(dst, MOE_BLOCK), :], sem.at[2])
                cp.start()
                cp.wait()

            for e in range(N_EXPERTS):
                @pl.when(nb_ref[e] > 0)
                def _():
                    zero_block(ps_ref[e] + (nb_ref[e] - 1) * MOE_BLOCK)

            used = ps_ref[N_EXPERTS - 1] // MOE_BLOCK + nb_ref[N_EXPERTS - 1]

            def tail(j, _):
                zero_block(j * MOE_BLOCK)
                return 0

            lax.fori_loop(used, n_blocks, tail, 0)

        slot = i % 2
        lr = lrow_ref[...]
        r_iota = lax.broadcasted_iota(jnp.int32, (tb, STAGE_ROWS), 1)
        sel = jnp.where((r_iota == lr[:, 0:1]) | (r_iota == lr[:, 1:2]), 1.0, 0.0).astype(BF16)
        xs[slot] = _dot_tn(sel, h_ref[...].astype(BF16))

        def copies_from(buf):
            def make_copy(lo, gd, rows):
                return pltpu.make_async_copy(xs.at[buf, pl.ds(lo, rows), :], xb_ref.at[pl.ds(gd, rows), :],
                                             sem.at[buf])
            return make_copy

        _tile_copies(nt_ref, lo_ref, gd_ref, i, copies_from(slot), wait=False)

        @pl.when(i > 0)
        def _():
            _tile_copies(nt_ref, lo_ref, gd_ref, i - 1, copies_from(1 - slot), wait=True)

        @pl.when(i == pl.num_programs(0) - 1)
        def _():
            _tile_copies(nt_ref, lo_ref, gd_ref, i, copies_from(slot), wait=True)

    return kernel


def _dispatch(h2, lrow, pad_start, nblk, ntile, loff, gdest, n_blocks):
    t, d = h2.shape
    tb = ROW_TILE
    grid_spec = pltpu.PrefetchScalarGridSpec(
        num_scalar_prefetch=5, grid=(t // tb,),
        in_specs=[pl.BlockSpec((tb, LANES), lambda i, *_: (i, 0)),
                  pl.BlockSpec((tb, d), lambda i, *_: (i, 0))],
        out_specs=pl.BlockSpec(memory_space=pl.ANY),
        scratch_shapes=[pltpu.VMEM((2, STAGE_ROWS, d), F32), pltpu.VMEM((MOE_BLOCK, d), F32),
                        pltpu.SemaphoreType.DMA((3,))])
    return pl.pallas_call(
        _make_dispatch_kernel(tb, n_blocks), grid_spec=grid_spec,
        out_shape=jax.ShapeDtypeStruct((n_blocks * MOE_BLOCK, d), F32),
        compiler_params=_cparams(("arbitrary",), 24),
        name="moe_dispatch",
    )(pad_start, nblk, ntile, loff, gdest, lrow, h2)


def _expert_kernel(be_ref, nu_ref, xb_ref, wg_ref, wu_ref, wd_ref, yb_ref):
    i = pl.program_id(0)

    @pl.when(i < nu_ref[0])
    def _():
        xb = xb_ref[...].astype(BF16)
        a = (_silu(_dot(xb, wg_ref[...])) * _dot(xb, wu_ref[...])).astype(BF16)
        yb_ref[...] = _dot(a, wd_ref[...]).astype(BF16).astype(F32)

    @pl.when(i >= nu_ref[0])
    def _():
        yb_ref[...] = jnp.zeros_like(yb_ref)


def _experts(xb, block_e, n_used, wg, wu, wd):
    _, d, f = wg.shape
    n_blocks = xb.shape[0] // MOE_BLOCK
    blk = pl.BlockSpec((MOE_BLOCK, d), lambda i, be, nu: (i, 0))
    grid_spec = pltpu.PrefetchScalarGridSpec(
        num_scalar_prefetch=2, grid=(n_blocks,),
        in_specs=[blk,
                  pl.BlockSpec((None, d, f), lambda i, be, nu: (be[i], 0, 0)),
                  pl.BlockSpec((None, d, f), lambda i, be, nu: (be[i], 0, 0)),
                  pl.BlockSpec((None, f, d), lambda i, be, nu: (be[i], 0, 0))],
        out_specs=blk)
    return pl.pallas_call(
        _expert_kernel, grid_spec=grid_spec,
        out_shape=jax.ShapeDtypeStruct(xb.shape, F32),
        compiler_params=_cparams(("arbitrary",), 56),
        name="moe_experts",
    )(block_e, n_used, xb, wg, wu, wd)


def _make_combine_kernel(tb):
    def kernel(nt_ref, lo_ref, gd_ref, yb_ref, lrow_ref, gt_ref, x_ref, g3_ref, fga_ref, o_ref, ys, sem):
        i = pl.program_id(0)
        slot = i % 2

        def copies_into(buf):
            def make_copy(lo, gd, rows):
                return pltpu.make_async_copy(yb_ref.at[pl.ds(gd, rows), :], ys.at[buf, pl.ds(lo, rows), :],
                                             sem.at[buf])
            return make_copy

        @pl.when(i == 0)
        def _():
            ys[...] = jnp.zeros_like(ys)
            _tile_copies(nt_ref, lo_ref, gd_ref, 0, copies_into(0), wait=False)

        @pl.when(i + 1 < pl.num_programs(0))
        def _():
            _tile_copies(nt_ref, lo_ref, gd_ref, i + 1, copies_into(1 - slot), wait=False)

        _tile_copies(nt_ref, lo_ref, gd_ref, i, copies_into(slot), wait=True)

        lr = lrow_ref[...]
        gt = gt_ref[...]
        r_iota = lax.broadcasted_iota(jnp.int32, (tb, STAGE_ROWS), 1)
        q = jnp.where(r_iota == lr[:, 0:1], gt[:, 0:1], 0.0) + jnp.where(r_iota == lr[:, 1:2], gt[:, 1:2], 0.0)
        qh, ql = _split_bf16(q)
        y = ys[slot].astype(BF16)
        f = _dot(qh, y) + _dot(ql, y)
        o_ref[...] = x_ref[...] + fga_ref[...] * _rms(f, g3_ref[...])

    return kernel


def _combine(yb, lrow, ntile, loff, gdest, gates, x, mods6, layer, first_row, rows_per_batch, g3):
    t, d = x.shape
    tb = ROW_TILE
    if rows_per_batch is None:
        fga_map = lambda i, *_: (layer, first_row, 5, 0, 0)
    else:
        fga_map = lambda i, *_: (layer, first_row + i // rows_per_batch, 5, 0, 0)
    grid_spec = pltpu.PrefetchScalarGridSpec(
        num_scalar_prefetch=3, grid=(t // tb,),
        in_specs=[pl.BlockSpec(memory_space=pl.ANY),
                  pl.BlockSpec((tb, LANES), lambda i, *_: (i, 0)),
                  pl.BlockSpec((tb, LANES), lambda i, *_: (i, 0)),
                  pl.BlockSpec((tb, d), lambda i, *_: (i, 0)),
                  pl.BlockSpec((1, d), lambda i, *_: (0, 0)),
                  pl.BlockSpec((None, None, None, 1, d), fga_map)],
        out_specs=pl.BlockSpec((tb, d), lambda i, *_: (i, 0)),
        scratch_shapes=[pltpu.VMEM((2, STAGE_ROWS, d), F32), pltpu.SemaphoreType.DMA((2,))])
    return pl.pallas_call(
        _make_combine_kernel(tb), grid_spec=grid_spec,
        out_shape=jax.ShapeDtypeStruct((t, d), F32),
        compiler_params=_cparams(("arbitrary",), 32),
        name="moe_combine",
    )(ntile, loff, gdest, yb, lrow, gates, x, g3, mods6)


def _router(logits):
    t = logits.shape[0]
    tb = ROW_TILE
    blk = pl.BlockSpec((tb, LANES), lambda i: (i, 0))
    return pl.pallas_call(
        _router_kernel,
        grid=(t // tb,),
        in_specs=[blk],
        out_specs=[blk, blk, pl.BlockSpec((SUBLANES, LANES), lambda i: (i, 0)),
                   pl.BlockSpec((1, LANES), lambda i: (0, 0))],
        out_shape=[jax.ShapeDtypeStruct((t, LANES), jnp.int32), jax.ShapeDtypeStruct((t, LANES), F32),
                   jax.ShapeDtypeStruct((t // tb * SUBLANES, LANES), jnp.int32),
                   jax.ShapeDtypeStruct((1, LANES), jnp.int32)],
        scratch_shapes=[pltpu.VMEM((1, LANES), F32)],
        compiler_params=_cparams(("arbitrary",), 16),
        name="moe_router",
    )(logits)


def _moe(h2, logits, x, mods6, layer, first_row, rows_per_batch, g3, wg, wu, wd):
    t = x.shape[0]
    n_tok_blocks = t // ROW_TILE
    max_rows = t * TOP_K + n_tok_blocks * N_EXPERTS * (SUBLANES - 1)
    n_blocks = -(-max_rows // MOE_BLOCK) + N_EXPERTS
    lrow, gates, tab, tot = _router(logits)
    tab = tab.reshape(n_tok_blocks, SUBLANES, LANES)[:, :, :N_EXPERTS]
    ntile, loff, prior = tab[:, 0], tab[:, 1], tab[:, 2]
    nblk = (tot[0, :N_EXPERTS] + MOE_BLOCK - 1) // MOE_BLOCK
    blk_end = jnp.cumsum(nblk)
    pad_start = ((blk_end - nblk) * MOE_BLOCK).astype(jnp.int32)
    gdest = (pad_start[None, :] + prior).astype(jnp.int32)
    n_used = blk_end[-1:].astype(jnp.int32)
    blk = jnp.minimum(jnp.arange(n_blocks, dtype=jnp.int32), n_used[0] - 1)
    block_e = jnp.minimum(jnp.sum(blk[:, None] >= blk_end[None, :], axis=1), N_EXPERTS - 1).astype(jnp.int32)
    ntile, loff, gdest = ntile.reshape(-1), loff.reshape(-1), gdest.reshape(-1)
    xb = _dispatch(h2, lrow, pad_start, nblk.astype(jnp.int32), ntile, loff, gdest, n_blocks)
    yb = _experts(xb, block_e, n_used, wg, wu, wd)
    return _combine(yb, lrow, ntile, loff, gdest, gates, x, mods6, layer, first_row, rows_per_batch, g3)


def _pad_cols(w, n):
    return jnp.pad(w, ((0, 0), (0, n - w.shape[1])))


def kernel(x_prompt, x_sample, state_mlstm_C, state_mlstm_n, state_mlstm_m, state_ret_S, c, c_ctx, mod_w, mod_b, norm_g, mlstm_w_in, mlstm_gate_b, mlstm_conv_w, mlstm_conv_b, mlstm_head_g, mlstm_w_out, ret_w_in, ret_decay_logit, ret_head_g, ret_w_out, ffn_w_gate, ffn_w_up, ffn_w_down, moe_router, moe_w_gate, moe_w_up, moe_w_down):
    bp, n_p, d = x_prompt.shape
    bs, n_s, _ = x_sample.shape
    depth = mod_w.shape[0]
    assert depth == 2 and ROW_TILE % GRID_W == 0 and n_s % GRID_W == 0

    cond = jnp.zeros((MOD_ROWS, d), F32).at[0].set(c_ctx).at[1:1 + bs].set(c)
    mods6 = _modulation(cond, mod_w, mod_b).reshape(depth, MOD_ROWS, N_MOD, 1, d)

    groups = [dict(x=x_prompt.reshape(bp * n_p, d), first=0, rpb=None, nseq=bp, ntok=n_p, prompt=True),
              dict(x=x_sample.reshape(bs * n_s, d), first=1, rpb=n_s // ROW_TILE, nseq=bs, ntok=n_s, prompt=False)]

    j = 0
    ml_qk = (mlstm_w_in.shape[2] - 4 * HEADS) // 2
    ml_v = ml_qk // 2
    w_in = mlstm_w_in[j]
    wqk = w_in[:, :ml_qk].astype(BF16)
    wv = w_in[:, ml_qk:ml_qk + ml_v].astype(BF16)
    wo = w_in[:, ml_qk + ml_v:ml_qk + 2 * ml_v].astype(BF16)
    w_gate = w_in[:, ml_qk + 2 * ml_v:]
    wgh, wgl = _split_bf16(_pad_cols(w_gate, LANES))
    bcol = _pad_cols(mlstm_gate_b[j][None, :], LANES)
    g = norm_g[0]
    w_out0 = mlstm_w_out[j].astype(BF16)
    new_c = new_n = new_m = None
    casts = (ffn_w_gate[j], ffn_w_up[j], ffn_w_down[j], ret_w_in[j], ret_w_out[j])
    proj = {}
    for grp in reversed(groups):
        args = (mods6, 0, grp["first"], grp["rpb"])
        outs = _proj_mlstm(grp["x"], grp["ntok"], *args, g[0:1], wqk, wv, wo, wgh, wgl, bcol, w_gate.shape[1],
                           mlstm_conv_w[j], mlstm_conv_b[j][None, :], casts=() if grp["prompt"] else casts)
        proj[grp["prompt"]] = outs[:6]
        if not grp["prompt"]:
            fwg, fwu, fwd, rw_in, w_out1 = outs[6:]
    for grp in groups:
        args = (mods6, 0, grp["first"], grp["rpb"])
        q, kt, v, o, gcol, grow = proj[grp["prompt"]]
        grow3 = grow.reshape(grow.shape[0], 1, grow.shape[1])
        if grp["prompt"]:
            m0 = jnp.zeros((grp["nseq"] * 2 * HEADS,), F32)
            hs, new_c, new_n, new_m = _mlstm_scan(q, kt, v, gcol, grow3, m0, None, grp["nseq"], grp["ntok"], True)
        else:
            (hs,) = _mlstm_scan(q, kt, v, gcol, grow3, state_mlstm_m[:, j].reshape(-1),
                                (state_mlstm_C[:, j], state_mlstm_n[:, j]), grp["nseq"], grp["ntok"], False)
        (grp["x"],) = _mixer_out(hs, o, mlstm_head_g[j][None, :], w_out0, grp["x"], *args, g[1:2],
                                 ffn=(g[2:3], g[3:4], fwg, fwu, fwd))

    ret_qk = ret_w_in.shape[2] // 3
    g = norm_g[1]
    wrh, wrl = _split_bf16(_pad_cols(moe_router[j], LANES))
    decay_flat = ret_decay_logit[j].reshape(-1)
    new_s = None
    n_exp, _, d_ff = moe_w_gate.shape[1:]
    casts = (moe_w_gate[j].reshape(n_exp * d, d_ff), moe_w_up[j].reshape(n_exp * d, d_ff),
             moe_w_down[j].reshape(n_exp * d_ff, d))
    proj = {}
    for grp in reversed(groups):
        args = (mods6, 1, grp["first"], grp["rpb"])
        rope_tabs = None if grp["prompt"] else _rope_tables(grp["ntok"], ret_qk // (2 * HEADS))
        outs = _proj_ret(grp["x"], grp["ntok"], *args, g[0:1], rw_in, ret_qk // 2, ret_qk, rope_tabs,
                         casts=() if grp["prompt"] else casts)
        proj[grp["prompt"]] = outs[:4]
        if not grp["prompt"]:
            ewg, ewu, ewd = (w.reshape(n_exp, -1, w.shape[1]) for w in outs[4:])
    for grp in groups:
        args = (mods6, 1, grp["first"], grp["rpb"])
        q, kt, v, gate = proj[grp["prompt"]]
        if grp["prompt"]:
            hs, new_s = _ret_scan(q, kt, v, decay_flat, None, grp["nseq"], grp["ntok"], True)
        else:
            (hs,) = _ret_scan(q, kt, v, decay_flat, state_ret_S[:, j], grp["nseq"], grp["ntok"], False)
        x1, h2, logits = _mixer_out(hs, gate, ret_head_g[j][None, :], w_out1, grp["x"], *args, g[1:2],
                                    router=(g[2:3], wrh, wrl))
        grp["x"] = _moe(h2, logits, x1, *args, g[3:4], ewg, ewu, ewd)

    y_prompt = groups[0]["x"].reshape(bp, n_p, d)
    y_sample = groups[1]["x"].reshape(bs, n_s, d)
    return (y_prompt, y_sample, new_c[:, None], new_n[:, None], new_m[:, None, :, :, 0], new_s[:, None])
```

```python
import functools
import math

import jax
import jax.numpy as jnp
from jax import lax
from jax.experimental import pallas as pl
from jax.experimental.pallas import tpu as pltpu

F32 = jnp.float32
BF16 = jnp.bfloat16

EPS = 1e-6
N_MOD = 6
HEADS = 4
CHUNK = 128
RET_CHUNK = 256
GRID_W = 64
ROPE_BASE = 10000.0
N_EXPERTS = 8
TOP_K = 2
MOE_BLOCK = 256
LANES = 128
SUBLANES = 8
ROW_TILE = 256
MIXER_SUBTILES = 2
MOD_ROWS = 8
STAGE_ROWS = -(-(TOP_K * ROW_TILE + N_EXPERTS * (SUBLANES - 1)) // LANES) * LANES
MIB = 1024 * 1024


def _cparams(sem, vmem_mib):
    return pltpu.CompilerParams(dimension_semantics=sem, vmem_limit_bytes=vmem_mib * MIB)


def _dot(a, b):
    return jnp.dot(a, b, preferred_element_type=F32)


def _dot_nt(a, b):
    return lax.dot_general(a, b, (((1,), (1,)), ((), ())), preferred_element_type=F32)


def _dot_tn(a, b):
    return lax.dot_general(a, b, (((0,), (0,)), ((), ())), preferred_element_type=F32)


def _split_bf16(x):
    hi = x.astype(BF16)
    lo = (x - hi.astype(F32)).astype(BF16)
    return hi, lo


def _rms(x, g):
    return x * lax.rsqrt(jnp.mean(x * x, -1, keepdims=True) + EPS) * g


def _adaln(x, g, shift, scale):
    return _rms(x, g) * (1.0 + scale) + shift


def _silu(x):
    return x * jax.nn.sigmoid(x)


def _logsig(x):
    return jnp.minimum(x, 0.0) - jnp.log1p(jnp.exp(-jnp.abs(x)))


def _const_spec(shape):
    nd = len(shape)
    return pl.BlockSpec(shape, lambda *_: (0,) * nd, pipeline_mode=pl.Buffered(1))


def _mod_spec(d, layer, comp, rows_per_batch, first_row):
    if rows_per_batch is None:
        return pl.BlockSpec((None, None, None, 1, d), lambda i: (layer, first_row, comp, 0, 0))
    return pl.BlockSpec((None, None, None, 1, d), lambda i: (layer, first_row + i // rows_per_batch, comp, 0, 0))


def _mod_kernel(c_ref, w_ref, b_ref, o_ref):
    s = _silu(c_ref[...]).astype(BF16)
    o_ref[...] = _dot(s, w_ref[...].astype(BF16)) + b_ref[...]


def _modulation(cond, mod_w, mod_b):
    depth, d, n = mod_w.shape
    tn = n // 4
    return pl.pallas_call(
        _mod_kernel,
        grid=(depth, n // tn),
        in_specs=[pl.BlockSpec((MOD_ROWS, d), lambda l, j: (0, 0)),
                  pl.BlockSpec((None, d, tn), lambda l, j: (l, 0, j)),
                  pl.BlockSpec((None, 1, tn), lambda l, j: (l, 0, j))],
        out_specs=pl.BlockSpec((None, MOD_ROWS, tn), lambda l, j: (l, 0, j)),
        out_shape=jax.ShapeDtypeStruct((depth, MOD_ROWS, n), F32),
        compiler_params=_cparams(("arbitrary", "arbitrary"), 40),
        name="modulation",
    )(cond, mod_w, mod_b.reshape(depth, 1, n))


BF16_ROWS = 16


def _cast_plan(arrays, n_steps):
    in_specs, out_specs, out_shape, nblks = [], [], [], []
    for a in arrays:
        rows, cols = a.shape
        nblk = max(n for n in range(1, n_steps + 1) if rows % n == 0 and (rows // n) % BF16_ROWS == 0)
        spec = pl.BlockSpec((rows // nblk, cols), lambda i, nb=nblk: (jnp.minimum(i, nb - 1), 0))
        in_specs.append(spec)
        out_specs.append(spec)
        out_shape.append(jax.ShapeDtypeStruct(a.shape, BF16))
        nblks.append(nblk)
    return in_specs, out_specs, out_shape, nblks


def _cast_slabs(srcs, dsts, nblks):
    for src, dst, nb in zip(srcs, dsts, nblks):
        @pl.when(pl.program_id(0) < nb)
        def _():
            dst[...] = src[...].astype(BF16)


def _proj_mlstm_kernel(tiles_per_seq, cast_nblks, *refs):
    nc = len(cast_nblks)
    (x_ref, xp_ref, xn_ref, g_ref, sh_ref, sc_ref, wqk_ref, wv_ref, wo_ref, wgh_ref, wgl_ref, bcol_ref, cw_ref,
     cb_ref) = refs[:14]
    q_ref, kt_ref, v_ref, o_ref, grow_ref = refs[14 + nc:19 + nc]
    _cast_slabs(refs[14:14 + nc], refs[19 + nc:], cast_nblks)
    tm = x_ref.shape[0]
    nq = q_ref.shape[1]
    h = _adaln(x_ref[...], g_ref[...], sh_ref[...], sc_ref[...])
    hb, hl = _split_bf16(h)
    x_halo = jnp.concatenate([xp_ref[...], xn_ref[...]], axis=0)
    hb_halo = _adaln(x_halo, g_ref[...], sh_ref[...], sc_ref[...]).astype(BF16)
    hb_all = jnp.concatenate([hb, hb_halo], axis=0)
    pos = pl.program_id(0) % tiles_per_seq
    has_prev = jnp.where(pos > 0, 1.0, 0.0)
    has_next = jnp.where(pos < tiles_per_seq - 1, 1.0, 0.0)
    wc = nq // HEADS
    rowi = lax.broadcasted_iota(jnp.int32, (tm, wc), 0)
    for c in range(2 * HEADS):
        sl = slice(c * wc, (c + 1) * wc)
        p_all = _dot(hb_all, wqk_ref[:, sl])
        p = p_all[:tm]
        p_prev = p_all[tm + SUBLANES - 1:tm + SUBLANES, :] * has_prev
        p_next = p_all[tm + SUBLANES:tm + SUBLANES + 1, :] * has_next
        prev = jnp.where(rowi == 0, p_prev, pltpu.roll(p, 1, axis=0))
        nxt = jnp.where(rowi == tm - 1, p_next, pltpu.roll(p, tm - 1, axis=0))
        y = prev * cw_ref[0:1, sl] + p * cw_ref[1:2, sl] + nxt * cw_ref[2:3, sl] + cb_ref[:, sl]
        if c < HEADS:
            q_ref[:, sl] = (y * wc ** -0.5).astype(BF16)
        else:
            kt_ref[(c - HEADS) * wc:(c - HEADS + 1) * wc, :] = y.T.astype(BF16)
    v_ref[...] = _dot(hb, wv_ref[...]).astype(BF16)
    o_ref[...] = jax.nn.sigmoid(_dot(hb, wo_ref[...]))
    gc = _dot(hb, wgh_ref[...]) + _dot(hl, wgh_ref[...]) + _dot(hb, wgl_ref[...]) + bcol_ref[...]
    grow_ref[...] = gc.T[:grow_ref.shape[0], :]


def _proj_mlstm(x, n_tok, mods6, layer, first_row, rows_per_batch, g, wqk, wv, wo, wgh, wgl, bcol, ng, conv_w, conv_b,
                casts=()):
    t, d = x.shape
    tm = ROW_TILE
    ms = lambda comp: _mod_spec(d, layer, comp, rows_per_batch, first_row)
    nq = wqk.shape[1] // 2
    tps = tm // SUBLANES
    last = t // SUBLANES - 1
    c_in, c_out, c_shape, c_nblks = _cast_plan(casts, t // tm)
    return pl.pallas_call(
        functools.partial(_proj_mlstm_kernel, n_tok // tm, tuple(c_nblks)),
        grid=(t // tm,),
        in_specs=[pl.BlockSpec((tm, d), lambda i: (i, 0)),
                  pl.BlockSpec((SUBLANES, d), lambda i: (jnp.maximum(i * tps - 1, 0), 0)),
                  pl.BlockSpec((SUBLANES, d), lambda i: (jnp.minimum((i + 1) * tps, last), 0)),
                  _const_spec((1, d)), ms(0), ms(1),
                  _const_spec(wqk.shape), _const_spec(wv.shape), _const_spec(wo.shape),
                  _const_spec(wgh.shape), _const_spec(wgl.shape), _const_spec(bcol.shape),
                  _const_spec(conv_w.shape), _const_spec(conv_b.shape)] + c_in,
        out_specs=[pl.BlockSpec((tm, nq), lambda i: (i, 0)),
                   pl.BlockSpec((nq, tm), lambda i: (0, i)),
                   pl.BlockSpec((tm, wv.shape[1]), lambda i: (i, 0)),
                   pl.BlockSpec((tm, wo.shape[1]), lambda i: (i, 0)),
                   pl.BlockSpec((ng, tm), lambda i: (0, i))] + c_out,
        out_shape=[jax.ShapeDtypeStruct((t, nq), BF16),
                   jax.ShapeDtypeStruct((nq, t), BF16),
                   jax.ShapeDtypeStruct((t, wv.shape[1]), BF16),
                   jax.ShapeDtypeStruct((t, wo.shape[1]), F32),
                   jax.ShapeDtypeStruct((ng, t), F32)] + c_shape,
        compiler_params=_cparams(("arbitrary",), 56),
        name="proj_mlstm",
    )(x, x, x, g, mods6, mods6, wqk, wv, wo, wgh, wgl, bcol, conv_w, conv_b, *casts)


def _make_mlstm_scan_kernel(n_tok, dk, dv, has_state, emit_state):
    L = CHUNK
    nc = n_tok // L
    assert nc % 2 == 0 and L == LANES

    def kernel(*refs):
        it = iter(refs)
        m0_ref = next(it)
        q_ref, kt_ref, v_ref = next(it), next(it), next(it)
        g_ref = next(it)
        if has_state:
            c0_ref, n0_ref = next(it), next(it)
        out_ref = next(it)
        if emit_state:
            cout_ref, nout_ref, mout_ref = next(it), next(it), next(it)
        bsc, rsc, rmsc, cst, cbf, nrep, nbf = (next(it) for _ in range(7))

        b = pl.program_id(0)
        hh = pl.program_id(1)

        ri = lax.broadcasted_iota(jnp.int32, (L, L), 0)
        ci = lax.broadcasted_iota(jnp.int32, (L, L), 1)
        lower = ri >= ci
        upper = ri <= ci
        masks = ((lower, upper), (upper, lower))

        lane = lax.broadcasted_iota(jnp.int32, (nc, L), 1)

        def lane_scan(x, op, fill, reverse):
            s = 1
            while s < L:
                if reverse:
                    x = op(x, jnp.where(lane < L - s, pltpu.roll(x, L - s, axis=1), fill))
                else:
                    x = op(x, jnp.where(lane >= s, pltpu.roll(x, s, axis=1), fill))
                s *= 2
            return x

        @pl.when(hh == 0)
        def _():
            for d in range(2):
                for h in range(HEADS):
                    row_i, row_f = d * 2 * HEADS + h, d * 2 * HEADS + HEADS + h
                    b_all = lane_scan(_logsig(g_ref[row_f]), jnp.add, 0.0, reverse=(d == 1))
                    r_all = g_ref[row_i] - b_all
                    bsc[d, h] = b_all
                    rsc[d, h] = r_all
                    rmsc[d, h] = lane_scan(r_all, jnp.maximum, -jnp.inf, reverse=(d == 1))

        if has_state:
            cst[...] = c0_ref[...]
            for d in range(2):
                nrep[d] = jnp.broadcast_to(n0_ref[d], (dk, LANES))
        else:
            cst[...] = jnp.zeros_like(cst)
            nrep[...] = jnp.zeros_like(nrep)
        cbf[...] = cst[...].astype(BF16)
        nbf[...] = nrep[...].astype(BF16)
        m_init = tuple(jnp.full((1, LANES), m0_ref[(b * 2 + d) * HEADS + hh], F32) for d in range(2))

        def lanes(x, n):
            return jnp.concatenate([x] * (n // LANES), axis=1)

        def chunk(d, c, m):
            mk = masks[d][0]
            r0 = pl.multiple_of(c * L, L)
            q = q_ref[pl.ds(r0, L), :]
            kt = kt_ref[:, pl.ds(r0, L)]
            v = v_ref[pl.ds(r0, L), :]
            rr = rsc[d, hh, pl.ds(c, 1), :]
            b_rep = jnp.broadcast_to(bsc[d, hh, pl.ds(c, 1), :], (L, L)).T
            rm_rep = jnp.broadcast_to(rmsc[d, hh, pl.ds(c, 1), :], (L, L)).T
            end = 0 if d == 1 else L - 1
            b_end, rm_end = b_rep[end:end + 1, :], rm_rep[end:end + 1, :]
            mm = jnp.maximum(m, rm_rep)
            qkn = _dot(q, jnp.concatenate([kt, nbf[d]], axis=1))
            s = qkn[:, :L] * jnp.where(mk, jnp.exp(rr - mm), 0.0)
            w_inter = jnp.exp(m - mm)
            num = lanes(w_inter, dv) * _dot(q, cbf[d]) + _dot(s.astype(BF16), v)
            den = w_inter * qkn[:, L:] + jnp.sum(s, axis=1, keepdims=True)
            inv = 1.0 / jnp.maximum(jnp.abs(den), jnp.exp(-(b_rep + mm)))
            h = num * lanes(inv, dv)
            m_end = jnp.maximum(m, rm_end)
            decay = jnp.exp(m - m_end)
            kw = kt.astype(F32) * jnp.exp(rr - m_end)
            c_new = lanes(decay, dv) * cst[d] + _dot(kw.astype(BF16), v)
            n_new = decay * nrep[d] + jnp.sum(kw, axis=1, keepdims=True)
            cst[d] = c_new
            cbf[d] = c_new.astype(BF16)
            nrep[d] = n_new
            nbf[d] = n_new.astype(BF16)
            return h, b_end + m_end

        def make_body(accumulate):
            def body(i, carry):
                mf, mb = carry
                hf, mf = chunk(0, i, mf)
                rf = pl.multiple_of(i * L, L)
                cb = nc - 1 - i
                hb, mb = chunk(1, cb, mb)
                rb = pl.multiple_of(cb * L, L)
                if accumulate:
                    out_ref[pl.ds(rf, L), :] += hf
                    out_ref[pl.ds(rb, L), :] += hb
                else:
                    out_ref[pl.ds(rf, L), :] = hf
                    out_ref[pl.ds(rb, L), :] = hb
                return mf, mb
            return body

        unroll = 2 if nc % 4 == 0 else 1
        carry = lax.fori_loop(0, nc // 2, make_body(False), m_init, unroll=unroll)
        carry = lax.fori_loop(nc // 2, nc, make_body(True), carry, unroll=unroll)

        if emit_state:
            cout_ref[...] = cst[...]
            for d in range(2):
                nout_ref[d, pl.ds(hh, 1), :] = nrep[d].T[0:1, :]
                mout_ref[d, pl.ds(hh, 1), :] = carry[d]

    return kernel


def _mlstm_scan(q, kt, v, gates, m0_flat, state, n_seq, n_tok, emit_state):
    dk = q.shape[1] // HEADS
    dv = v.shape[1] // HEADS
    nc = n_tok // CHUNK
    has_state = state is not None
    kern = _make_mlstm_scan_kernel(n_tok, dk, dv, has_state, emit_state)
    in_specs = [pl.BlockSpec((n_tok, dk), lambda b, h, m: (b, h)),
                pl.BlockSpec((dk, n_tok), lambda b, h, m: (h, b)),
                pl.BlockSpec((n_tok, dv), lambda b, h, m: (b, h)),
                pl.BlockSpec((gates.shape[0], None, nc, CHUNK), lambda b, h, m: (0, b, 0, 0))]
    args = [q, kt, v, gates]
    if has_state:
        in_specs += [pl.BlockSpec((None, 2, None, dk, dv), lambda b, h, m: (b, 0, h, 0, 0)),
                     pl.BlockSpec((None, 2, None, dk, 1), lambda b, h, m: (b, 0, h, 0, 0))]
        args += [state[0], state[1][..., None]]
    out_specs = [pl.BlockSpec((n_tok, dv), lambda b, h, m: (b, h))]
    out_shape = [jax.ShapeDtypeStruct((n_seq * n_tok, HEADS * dv), F32)]
    if emit_state:
        out_specs += [pl.BlockSpec((None, 2, None, dk, dv), lambda b, h, m: (b, 0, h, 0, 0)),
                      pl.BlockSpec((None, 2, HEADS, dk), lambda b, h, m: (b, 0, 0, 0)),
                      pl.BlockSpec((None, 2, HEADS, LANES), lambda b, h, m: (b, 0, 0, 0))]
        out_shape += [jax.ShapeDtypeStruct((n_seq, 2, HEADS, dk, dv), F32),
                      jax.ShapeDtypeStruct((n_seq, 2, HEADS, dk), F32),
                      jax.ShapeDtypeStruct((n_seq, 2, HEADS, LANES), F32)]
    grid_spec = pltpu.PrefetchScalarGridSpec(
        num_scalar_prefetch=1, grid=(n_seq, HEADS), in_specs=in_specs, out_specs=out_specs,
        scratch_shapes=[pltpu.VMEM((2, HEADS, nc, CHUNK), F32),
                        pltpu.VMEM((2, HEADS, nc, CHUNK), F32),
                        pltpu.VMEM((2, HEADS, nc, CHUNK), F32),
                        pltpu.VMEM((2, dk, dv), F32), pltpu.VMEM((2, dk, dv), BF16),
                        pltpu.VMEM((2, dk, LANES), F32), pltpu.VMEM((2, dk, LANES), BF16)])
    return pl.pallas_call(
        kern, grid_spec=grid_spec, out_shape=out_shape,
        compiler_params=_cparams(("arbitrary", "arbitrary"), 56),
        name="mlstm_scan",
    )(m0_flat, *args)


def _proj_ret_kernel(rope, cast_nblks, *refs):
    nc = len(cast_nblks)
    n_in = 5 + (4 if rope else 0)
    x_ref, g_ref, sh_ref, sc_ref, w_ref = refs[:5]
    if rope:
        rcos_ref, rsin_ref, ccos_ref, csin_ref = refs[5:9]
    q_ref, kt_ref, v_ref, gate_ref = refs[n_in + nc:n_in + nc + 4]
    _cast_slabs(refs[n_in:n_in + nc], refs[n_in + nc + 4:], cast_nblks)
    nq, nv = q_ref.shape[1], v_ref.shape[1]
    wc = nq // HEADS
    hb = _adaln(x_ref[...], g_ref[...], sh_ref[...], sc_ref[...]).astype(BF16)
    if rope:
        cos = jnp.concatenate([rcos_ref[...], ccos_ref[...]], axis=1)
        sin = jnp.concatenate([rsin_ref[...], csin_ref[...]], axis=1)
    for c in range(2 * HEADS):
        p = _dot(hb, w_ref[:, c * wc:(c + 1) * wc])
        if rope:
            swapped = jnp.concatenate([pltpu.roll(p[:, j * LANES:(j + 1) * LANES], LANES // 2, axis=1)
                                       for j in range(wc // LANES)], axis=1)
            p = p * cos + swapped * sin
        if c < HEADS:
            q_ref[:, c * wc:(c + 1) * wc] = p.astype(BF16)
        else:
            kt_ref[(c - HEADS) * wc:(c - HEADS + 1) * wc, :] = (p * wc ** -0.5).T.astype(BF16)
    v_ref[...] = _dot(hb, w_ref[:, 2 * nq:2 * nq + nv]).astype(BF16)
    gate_ref[...] = _silu(_dot(hb, w_ref[:, 2 * nq + nv:])).astype(BF16)


def _proj_ret(x, n_tok, mods6, layer, first_row, rows_per_batch, g, w_in, nq, nv, rope_tabs, casts=()):
    t, d = x.shape
    tm = ROW_TILE
    ms = lambda comp: _mod_spec(d, layer, comp, rows_per_batch, first_row)
    ng = w_in.shape[1] - 2 * nq - nv
    rope = rope_tabs is not None
    in_specs = [pl.BlockSpec((tm, d), lambda i: (i, 0)), _const_spec((1, d)), ms(0), ms(1), _const_spec(w_in.shape)]
    args = [x, g, mods6, mods6, w_in]
    if rope:
        tiles_per_seq = n_tok // tm
        row_spec = pl.BlockSpec((tm, LANES), lambda i: (i % tiles_per_seq, 0))
        in_specs += [row_spec, row_spec, _const_spec((tm, LANES)), _const_spec((tm, LANES))]
        args += list(rope_tabs)
    c_in, c_out, c_shape, c_nblks = _cast_plan(casts, t // tm)
    return pl.pallas_call(
        functools.partial(_proj_ret_kernel, rope, tuple(c_nblks)),
        grid=(t // tm,),
        in_specs=in_specs + c_in,
        out_specs=[pl.BlockSpec((tm, nq), lambda i: (i, 0)),
                   pl.BlockSpec((nq, tm), lambda i: (0, i)),
                   pl.BlockSpec((tm, nv), lambda i: (i, 0)),
                   pl.BlockSpec((tm, ng), lambda i: (i, 0))] + c_out,
        out_shape=[jax.ShapeDtypeStruct((t, nq), BF16),
                   jax.ShapeDtypeStruct((nq, t), BF16),
                   jax.ShapeDtypeStruct((t, nv), BF16),
                   jax.ShapeDtypeStruct((t, ng), BF16)] + c_shape,
        compiler_params=_cparams(("arbitrary",), 56),
        name="proj_ret",
    )(*args, *casts)


def _make_ret_scan_kernel(n_tok, dk, dv, has_state, emit_state):
    L = RET_CHUNK
    nc = n_tok // L
    assert nc == 1 or nc % 2 == 0

    def kernel(*refs):
        it = iter(refs)
        dl_ref = next(it)
        q_ref, kt_ref, v_ref = next(it), next(it), next(it)
        if has_state:
            s0_ref = next(it)
        out_ref = next(it)
        if emit_state:
            sout_ref = next(it)
        sst, sbf = next(it), next(it)
        if nc > 1:
            acc = next(it)

        hh = pl.program_id(1)
        ri = lax.broadcasted_iota(jnp.int32, (L, L), 0)
        ci = lax.broadcasted_iota(jnp.int32, (L, L), 1)
        rel = (ri - ci).astype(F32)
        pos_col = lax.broadcasted_iota(jnp.int32, (L, LANES), 0).astype(F32)
        pos_row = lax.broadcasted_iota(jnp.int32, (1, L), 1).astype(F32)

        dmat, q_dec, k_dec, c_dec = [], [], [], []
        for d in range(2):
            lg = _logsig(jnp.full((1, 1), dl_ref[d * HEADS + hh], F32))
            if d == 0:
                dmat.append(jnp.where(ri >= ci, jnp.exp(lg * jnp.maximum(rel, 0.0)), 0.0))
                q_dec.append(jnp.exp(lg * (pos_col + 1.0)))
                k_dec.append(jnp.exp(lg * (L - 1.0 - pos_row)))
            else:
                dmat.append(jnp.where(ri <= ci, jnp.exp(lg * jnp.maximum(-rel, 0.0)), 0.0))
                q_dec.append(jnp.exp(lg * (L - pos_col)))
                k_dec.append(jnp.exp(lg * pos_row))
            c_dec.append(jnp.broadcast_to(jnp.exp(lg * float(L)), (1, LANES)))

        if has_state:
            sst[...] = s0_ref[...]
        else:
            sst[...] = jnp.zeros_like(sst)
        sbf[...] = sst[...].astype(BF16)

        def lanes(x, n):
            return jnp.concatenate([x] * (n // LANES), axis=1)

        def chunk(d, c):
            r0 = c * L if isinstance(c, int) else pl.multiple_of(c * L, L)
            q = q_ref[pl.ds(r0, L), :]
            kt = kt_ref[:, pl.ds(r0, L)]
            v = v_ref[pl.ds(r0, L), :]
            s = _dot(q, kt) * dmat[d]
            out = _dot(s.astype(BF16), v) + lanes(q_dec[d], dv) * _dot(q, sbf[d])
            s_new = lanes(c_dec[d], dv) * sst[d] + _dot((kt.astype(F32) * k_dec[d]).astype(BF16), v)
            sst[d] = s_new
            sbf[d] = s_new.astype(BF16)
            return out

        def body1(i, _):
            rf = pl.multiple_of(i * L, L)
            acc[pl.ds(rf, L), :] = chunk(0, i)
            cb = nc - 1 - i
            rb = pl.multiple_of(cb * L, L)
            acc[pl.ds(rb, L), :] = chunk(1, cb)
            return 0

        def body2(i, _):
            rf = pl.multiple_of(i * L, L)
            out_ref[pl.ds(rf, L), :] = (acc[pl.ds(rf, L), :] + chunk(0, i)).astype(BF16)
            cb = nc - 1 - i
            rb = pl.multiple_of(cb * L, L)
            out_ref[pl.ds(rb, L), :] = (acc[pl.ds(rb, L), :] + chunk(1, cb)).astype(BF16)
            return 0

        if nc == 1:
            out_ref[...] = (chunk(0, 0) + chunk(1, 0)).astype(BF16)
        else:
            unroll = 2 if nc % 4 == 0 else 1
            lax.fori_loop(0, nc // 2, body1, 0, unroll=unroll)
            lax.fori_loop(nc // 2, nc, body2, 0, unroll=unroll)
        if emit_state:
            sout_ref[...] = sst[...]

    return kernel


def _ret_scan(q, kt, v, decay_flat, state, n_seq, n_tok, emit_state):
    dk = q.shape[1] // HEADS
    dv = v.shape[1] // HEADS
    has_state = state is not None
    kern = _make_ret_scan_kernel(n_tok, dk, dv, has_state, emit_state)
    in_specs = [pl.BlockSpec((n_tok, dk), lambda b, h, m: (b, h)),
                pl.BlockSpec((dk, n_tok), lambda b, h, m: (h, b)),
                pl.BlockSpec((n_tok, dv), lambda b, h, m: (b, h))]
    args = [q, kt, v]
    if has_state:
        in_specs.append(pl.BlockSpec((None, 2, None, dk, dv), lambda b, h, m: (b, 0, h, 0, 0)))
        args.append(state)
    out_specs = [pl.BlockSpec((n_tok, dv), lambda b, h, m: (b, h))]
    out_shape = [jax.ShapeDtypeStruct((n_seq * n_tok, HEADS * dv), BF16)]
    if emit_state:
        out_specs.append(pl.BlockSpec((None, 2, None, dk, dv), lambda b, h, m: (b, 0, h, 0, 0)))
        out_shape.append(jax.ShapeDtypeStruct((n_seq, 2, HEADS, dk, dv), F32))
    grid_spec = pltpu.PrefetchScalarGridSpec(
        num_scalar_prefetch=1, grid=(n_seq, HEADS), in_specs=in_specs, out_specs=out_specs,
        scratch_shapes=[pltpu.VMEM((2, dk, dv), F32), pltpu.VMEM((2, dk, dv), BF16)]
        + ([pltpu.VMEM((n_tok, dv), F32)] if n_tok > RET_CHUNK else []))
    return pl.pallas_call(
        kern, grid_spec=grid_spec, out_shape=out_shape,
        compiler_params=_cparams(("arbitrary", "arbitrary"), 56),
        name="ret_scan",
    )(decay_flat, *args)


def _rope_tables(n_tok, dk):
    r = dk // 4
    inv = 1.0 / (ROPE_BASE ** (jnp.arange(r, dtype=F32) / r))
    sign = jnp.concatenate([-jnp.ones((r,), F32), jnp.ones((r,), F32)])
    rows = (jnp.arange(n_tok) // GRID_W).astype(F32)[:, None] * inv
    cols = (jnp.arange(ROW_TILE) % GRID_W).astype(F32)[:, None] * inv
    two = lambda a: jnp.concatenate([a, a], axis=-1)
    return (two(jnp.cos(rows)), two(jnp.sin(rows)) * sign, two(jnp.cos(cols)), two(jnp.sin(cols)) * sign)


def _make_mixer_out_kernel(dv, with_router, with_ffn):
    def kernel(*refs):
        it = iter(refs)
        hs_ref, gate_ref, hg_ref, w_ref, x_ref, g1_ref, ga_ref = (next(it) for _ in range(7))
        if with_router:
            g2_ref, fsh_ref, fsc_ref, wrh_ref, wrl_ref = (next(it) for _ in range(5))
        if with_ffn:
            g2_ref, fsh_ref, fsc_ref, wg_ref, wu_ref, wd_ref, g3_ref, fga_ref = (next(it) for _ in range(8))
        xo_ref = next(it)
        if with_router:
            h2_ref, lg_ref = next(it), next(it)
        z_ref = next(it)
        subs = [slice(r * ROW_TILE, (r + 1) * ROW_TILE) for r in range(x_ref.shape[0] // ROW_TILE)]
        for rs in subs:
            for hh in range(HEADS):
                sl = slice(hh * dv, (hh + 1) * dv)
                seg = hs_ref[rs, sl].astype(F32)
                y = seg * lax.rsqrt(jnp.mean(seg * seg, -1, keepdims=True) + EPS) * hg_ref[:, sl]
                z_ref[rs, sl] = (gate_ref[rs, sl].astype(F32) * y).astype(BF16)
        ys = [_dot(z_ref[rs, :], w_ref[...]) for rs in subs]
        xns = [x_ref[rs, :] + ga_ref[...] * _rms(y, g1_ref[...]) for rs, y in zip(subs, ys)]
        if with_ffn:
            hbs = [_adaln(xn, g2_ref[...], fsh_ref[...], fsc_ref[...]).astype(BF16) for xn in xns]
            acts = [(_silu(_dot(hb, wg_ref[...])) * _dot(hb, wu_ref[...])).astype(BF16) for hb in hbs]
            downs = [_dot(a, wd_ref[...]) for a in acts]
            xns = [xn + fga_ref[...] * _rms(dn, g3_ref[...]) for xn, dn in zip(xns, downs)]
        for rs, xn in zip(subs, xns):
            xo_ref[rs, :] = xn
        if with_router:
            for rs, xn in zip(subs, xns):
                h2 = _adaln(xn, g2_ref[...], fsh_ref[...], fsc_ref[...])
                h2_ref[rs, :] = h2
                hb, hl = _split_bf16(h2)
                lg_ref[rs, :] = _dot(hb, wrh_ref[...]) + _dot(hl, wrh_ref[...]) + _dot(hb, wrl_ref[...])
    return kernel


def _mixer_out(hs, gate, head_g, w_out, x, mods6, layer, first_row, rows_per_batch, g1, router=None, ffn=None):
    t, d = x.shape
    vdim = hs.shape[1]
    tm = MIXER_SUBTILES * ROW_TILE
    tiles_per_batch = None if rows_per_batch is None else rows_per_batch // MIXER_SUBTILES
    ms = lambda comp: _mod_spec(d, layer, comp, tiles_per_batch, first_row)
    with_router = router is not None
    with_ffn = ffn is not None
    in_specs = [pl.BlockSpec((tm, vdim), lambda i: (i, 0)),
                pl.BlockSpec((tm, vdim), lambda i: (i, 0)),
                _const_spec((1, vdim)), _const_spec(w_out.shape),
                pl.BlockSpec((tm, d), lambda i: (i, 0)), _const_spec((1, d)), ms(2)]
    args = [hs, gate, head_g, w_out, x, g1, mods6]
    out_specs = [pl.BlockSpec((tm, d), lambda i: (i, 0))]
    out_shape = [jax.ShapeDtypeStruct((t, d), F32)]
    if with_router:
        g2, wrh, wrl = router
        in_specs += [_const_spec((1, d)), ms(3), ms(4), _const_spec(wrh.shape), _const_spec(wrl.shape)]
        args += [g2, mods6, mods6, wrh, wrl]
        out_specs += [pl.BlockSpec((tm, d), lambda i: (i, 0)), pl.BlockSpec((tm, LANES), lambda i: (i, 0))]
        out_shape += [jax.ShapeDtypeStruct((t, d), F32), jax.ShapeDtypeStruct((t, LANES), F32)]
    if with_ffn:
        g2, g3, wg, wu, wd = ffn
        in_specs += [_const_spec((1, d)), ms(3), ms(4), _const_spec(wg.shape), _const_spec(wu.shape),
                     _const_spec(wd.shape), _const_spec((1, d)), ms(5)]
        args += [g2, mods6, mods6, wg, wu, wd, g3, mods6]
    return pl.pallas_call(
        _make_mixer_out_kernel(vdim // HEADS, with_router, with_ffn),
        grid=(t // tm,), in_specs=in_specs, out_specs=out_specs, out_shape=out_shape,
        scratch_shapes=[pltpu.VMEM((tm, vdim), BF16)],
        compiler_params=_cparams(("arbitrary",), 58),
        name="mixer_ffn" if with_ffn else "mixer_out",
    )(*args)


def _router_kernel(lg_ref, lrow_ref, gt_ref, tab_ref, tot_ref, carry):
    tb = lg_ref.shape[0]

    @pl.when(pl.program_id(0) == 0)
    def _():
        carry[...] = jnp.zeros_like(carry)

    lane = lax.broadcasted_iota(jnp.int32, (tb, LANES), 1)
    lg = jnp.where(lane < N_EXPERTS, lg_ref[...], -jnp.inf)
    v1 = jnp.max(lg, axis=1, keepdims=True)
    i1 = jnp.min(jnp.where(lg == v1, lane, LANES), axis=1, keepdims=True)
    lg2 = jnp.where(lane == i1, -jnp.inf, lg)
    v2 = jnp.max(lg2, axis=1, keepdims=True)
    i2 = jnp.min(jnp.where(lg2 == v2, lane, LANES), axis=1, keepdims=True)
    ex = jnp.exp(v2 - v1)
    den = 1.0 + ex
    g1 = 1.0 / den
    g2 = ex / den
    oh1 = lane == i1
    oh2 = lane == i2
    onehot = jnp.where(oh1 | oh2, 1.0, 0.0)
    ri = lax.broadcasted_iota(jnp.int32, (tb, tb), 0)
    ci = lax.broadcasted_iota(jnp.int32, (tb, tb), 1)
    before = jnp.where(ri > ci, 1.0, 0.0).astype(BF16)
    rank = _dot(before, onehot.astype(BF16))
    tiles = jnp.ceil(jnp.sum(onehot, axis=0, keepdims=True) * (1.0 / SUBLANES))
    ei = lax.broadcasted_iota(jnp.int32, (LANES, LANES), 0)
    ej = lax.broadcasted_iota(jnp.int32, (LANES, LANES), 1)
    earlier = jnp.where(ei < ej, 1.0, 0.0).astype(BF16)
    off = _dot(jnp.broadcast_to(tiles, (SUBLANES, LANES)).astype(BF16), earlier)[0:1] * SUBLANES
    pos = rank + off
    r1 = jnp.sum(jnp.where(oh1, pos, 0.0), axis=1, keepdims=True)
    r2 = jnp.sum(jnp.where(oh2, pos, 0.0), axis=1, keepdims=True)
    lrow_ref[...] = jnp.where(lane == 0, r1, jnp.where(lane == 1, r2, 0.0)).astype(jnp.int32)
    gt_ref[...] = jnp.where(lane == 0, g1, jnp.where(lane == 1, g2, 0.0))
    sub = lax.broadcasted_iota(jnp.int32, (SUBLANES, LANES), 0)
    tab = jnp.where(sub == 0, tiles, jnp.where(sub == 1, off, jnp.where(sub == 2, carry[...], 0.0)))
    tab_ref[...] = tab.astype(jnp.int32)
    carry[...] = carry[...] + tiles * SUBLANES
    tot_ref[...] = carry[...].astype(jnp.int32)


RUN_TILES = 8


def _tile_copies(nt_ref, lo_ref, gd_ref, blk, make_copy, wait):
    def go(lo, gd, rows):
        cp = make_copy(pl.multiple_of(lo, SUBLANES), pl.multiple_of(gd, SUBLANES), rows)
        if wait:
            cp.wait()
        else:
            cp.start()

    for e in range(N_EXPERTS):
        idx = blk * N_EXPERTS + e
        nt, lo, gd = nt_ref[idx], lo_ref[idx], gd_ref[idx]

        def run(j, _):
            off = j * (RUN_TILES * SUBLANES)
            go(lo + off, gd + off, RUN_TILES * SUBLANES)
            return 0

        lax.fori_loop(0, nt // RUN_TILES, run, 0)
        k = RUN_TILES // 2
        while k >= 1:
            @pl.when(nt % (2 * k) >= k)
            def _():
                off = (nt // (2 * k)) * (2 * k) * SUBLANES
                go(lo + off, gd + off, k * SUBLANES)
            k //= 2


def _make_dispatch_kernel(tb, n_blocks):
    def kernel(ps_ref, nb_ref, nt_ref, lo_ref, gd_ref, lrow_ref, h_ref, xb_ref, xs, zbuf, sem):
        i = pl.program_id(0)

        @pl.when(i == 0)
        def _():
            zbuf[...] = jnp.zeros_like(zbuf)

            def zero_block(row0):
                dst = pl.multiple_of(row0, MOE_BLOCK)
                cp = pltpu.make_async_copy(zbuf, xb_ref.at[pl.ds(dst, MOE_BLOCK), :], sem.at[2])
                cp.start()
                cp.wait()

            for e in range(N_EXPERTS):
                @pl.when(nb_ref[e] > 0)
                def _():
                    zero_block(ps_ref[e] + (nb_ref[e] - 1) * MOE_BLOCK)

            used = ps_ref[N_EXPERTS - 1] // MOE_BLOCK + nb_ref[N_EXPERTS - 1]

            def tail(j, _):
                zero_block(j * MOE_BLOCK)
                return 0

            lax.fori_loop(used, n_blocks, tail, 0)

        slot = i % 2
        lr = lrow_ref[...]
        r_iota = lax.broadcasted_iota(jnp.int32, (tb, STAGE_ROWS), 1)
        sel = jnp.where((r_iota == lr[:, 0:1]) | (r_iota == lr[:, 1:2]), 1.0, 0.0).astype(BF16)
        xs[slot] = _dot_tn(sel, h_ref[...].astype(BF16))

        def copies_from(buf):
            def make_copy(lo, gd, rows):
                return pltpu.make_async_copy(xs.at[buf, pl.ds(lo, rows), :], xb_ref.at[pl.ds(gd, rows), :],
                                             sem.at[buf])
            return make_copy

        _tile_copies(nt_ref, lo_ref, gd_ref, i, copies_from(slot), wait=False)

        @pl.when(i > 0)
        def _():
            _tile_copies(nt_ref, lo_ref, gd_ref, i - 1, copies_from(1 - slot), wait=True)

        @pl.when(i == pl.num_programs(0) - 1)
        def _():
            _tile_copies(nt_ref, lo_ref, gd_ref, i, copies_from(slot), wait=True)

    return kernel


def _dispatch(h2, lrow, pad_start, nblk, ntile, loff, gdest, n_blocks):
    t, d = h2.shape
    tb = ROW_TILE
    grid_spec = pltpu.PrefetchScalarGridSpec(
        num_scalar_prefetch=5, grid=(t // tb,),
        in_specs=[pl.BlockSpec((tb, LANES), lambda i, *_: (i, 0)),
                  pl.BlockSpec((tb, d), lambda i, *_: (i, 0))],
        out_specs=pl.BlockSpec(memory_space=pl.ANY),
        scratch_shapes=[pltpu.VMEM((2, STAGE_ROWS, d), F32), pltpu.VMEM((MOE_BLOCK, d), F32),
                        pltpu.SemaphoreType.DMA((3,))])
    return pl.pallas_call(
        _make_dispatch_kernel(tb, n_blocks), grid_spec=grid_spec,
        out_shape=jax.ShapeDtypeStruct((n_blocks * MOE_BLOCK, d), F32),
        compiler_params=_cparams(("arbitrary",), 24),
        name="moe_dispatch",
    )(pad_start, nblk, ntile, loff, gdest, lrow, h2)


def _expert_kernel(be_ref, nu_ref, xb_ref, wg_ref, wu_ref, wd_ref, yb_ref):
    i = pl.program_id(0)

    @pl.when(i < nu_ref[0])
    def _():
        xb = xb_ref[...].astype(BF16)
        a = (_silu(_dot(xb, wg_ref[...])) * _dot(xb, wu_ref[...])).astype(BF16)
        yb_ref[...] = _dot(a, wd_ref[...]).astype(BF16).astype(F32)

    @pl.when(i >= nu_ref[0])
    def _():
        yb_ref[...] = jnp.zeros_like(yb_ref)


def _experts(xb, block_e, n_used, wg, wu, wd):
    _, d, f = wg.shape
    n_blocks = xb.shape[0] // MOE_BLOCK
    blk = pl.BlockSpec((MOE_BLOCK, d), lambda i, be, nu: (i, 0))
    grid_spec = pltpu.PrefetchScalarGridSpec(
        num_scalar_prefetch=2, grid=(n_blocks,),
        in_specs=[blk,
                  pl.BlockSpec((None, d, f), lambda i, be, nu: (be[i], 0, 0)),
                  pl.BlockSpec((None, d, f), lambda i, be, nu: (be[i], 0, 0)),
                  pl.BlockSpec((None, f, d), lambda i, be, nu: (be[i], 0, 0))],
        out_specs=blk)
    return pl.pallas_call(
        _expert_kernel, grid_spec=grid_spec,
        out_shape=jax.ShapeDtypeStruct(xb.shape, F32),
        compiler_params=_cparams(("arbitrary",), 56),
        name="moe_experts",
    )(block_e, n_used, xb, wg, wu, wd)


def _make_combine_kernel(tb):
    def kernel(nt_ref, lo_ref, gd_ref, yb_ref, lrow_ref, gt_ref, x_ref, g3_ref, fga_ref, o_ref, ys, sem):
        i = pl.program_id(0)
        slot = i % 2

        def copies_into(buf):
            def make_copy(lo, gd, rows):
                return pltpu.make_async_copy(yb_ref.at[pl.ds(gd, rows), :], ys.at[buf, pl.ds(lo, rows), :],
                                             sem.at[buf])
            return make_copy

        @pl.when(i == 0)
        def _():
            ys[...] = jnp.zeros_like(ys)
            _tile_copies(nt_ref, lo_ref, gd_ref, 0, copies_into(0), wait=False)

        @pl.when(i + 1 < pl.num_programs(0))
        def _():
            _tile_copies(nt_ref, lo_ref, gd_ref, i + 1, copies_into(1 - slot), wait=False)

        _tile_copies(nt_ref, lo_ref, gd_ref, i, copies_into(slot), wait=True)

        lr = lrow_ref[...]
        gt = gt_ref[...]
        r_iota = lax.broadcasted_iota(jnp.int32, (tb, STAGE_ROWS), 1)
        q = jnp.where(r_iota == lr[:, 0:1], gt[:, 0:1], 0.0) + jnp.where(r_iota == lr[:, 1:2], gt[:, 1:2], 0.0)
        qh, ql = _split_bf16(q)
        y = ys[slot].astype(BF16)
        f = _dot(qh, y) + _dot(ql, y)
        o_ref[...] = x_ref[...] + fga_ref[...] * _rms(f, g3_ref[...])

    return kernel


def _combine(yb, lrow, ntile, loff, gdest, gates, x, mods6, layer, first_row, rows_per_batch, g3):
    t, d = x.shape
    tb = ROW_TILE
    if rows_per_batch is None:
        fga_map = lambda i, *_: (layer, first_row, 5, 0, 0)
    else:
        fga_map = lambda i, *_: (layer, first_row + i // rows_per_batch, 5, 0, 0)
    grid_spec = pltpu.PrefetchScalarGridSpec(
        num_scalar_prefetch=3, grid=(t // tb,),
        in_specs=[pl.BlockSpec(memory_space=pl.ANY),
                  pl.BlockSpec((tb, LANES), lambda i, *_: (i, 0)),
                  pl.BlockSpec((tb, LANES), lambda i, *_: (i, 0)),
                  pl.BlockSpec((tb, d), lambda i, *_: (i, 0)),
                  pl.BlockSpec((1, d), lambda i, *_: (0, 0)),
                  pl.BlockSpec((None, None, None, 1, d), fga_map)],
        out_specs=pl.BlockSpec((tb, d), lambda i, *_: (i, 0)),
        scratch_shapes=[pltpu.VMEM((2, STAGE_ROWS, d), F32), pltpu.SemaphoreType.DMA((2,))])
    return pl.pallas_call(
        _make_combine_kernel(tb), grid_spec=grid_spec,
        out_shape=jax.ShapeDtypeStruct((t, d), F32),
        compiler_params=_cparams(("arbitrary",), 32),
        name="moe_combine",
    )(ntile, loff, gdest, yb, lrow, gates, x, g3, mods6)


def _router(logits):
    t = logits.shape[0]
    tb = ROW_TILE
    blk = pl.BlockSpec((tb, LANES), lambda i: (i, 0))
    return pl.pallas_call(
        _router_kernel,
        grid=(t // tb,),
        in_specs=[blk],
        out_specs=[blk, blk, pl.BlockSpec((SUBLANES, LANES), lambda i: (i, 0)),
                   pl.BlockSpec((1, LANES), lambda i: (0, 0))],
        out_shape=[jax.ShapeDtypeStruct((t, LANES), jnp.int32), jax.ShapeDtypeStruct((t, LANES), F32),
                   jax.ShapeDtypeStruct((t // tb * SUBLANES, LANES), jnp.int32),
                   jax.ShapeDtypeStruct((1, LANES), jnp.int32)],
        scratch_shapes=[pltpu.VMEM((1, LANES), F32)],
        compiler_params=_cparams(("arbitrary",), 16),
        name="moe_router",
    )(logits)


def _moe(h2, logits, x, mods6, layer, first_row, rows_per_batch, g3, wg, wu, wd):
    t = x.shape[0]
    n_tok_blocks = t // ROW_TILE
    max_rows = t * TOP_K + n_tok_blocks * N_EXPERTS * (SUBLANES - 1)
    n_blocks = -(-max_rows // MOE_BLOCK) + N_EXPERTS
    lrow, gates, tab, tot = _router(logits)
    tab = tab.reshape(n_tok_blocks, SUBLANES, LANES)[:, :, :N_EXPERTS]
    ntile, loff, prior = tab[:, 0], tab[:, 1], tab[:, 2]
    nblk = (tot[0, :N_EXPERTS] + MOE_BLOCK - 1) // MOE_BLOCK
    blk_end = jnp.cumsum(nblk)
    pad_start = ((blk_end - nblk) * MOE_BLOCK).astype(jnp.int32)
    gdest = (pad_start[None, :] + prior).astype(jnp.int32)
    n_used = blk_end[-1:].astype(jnp.int32)
    blk = jnp.minimum(jnp.arange(n_blocks, dtype=jnp.int32), n_used[0] - 1)
    block_e = jnp.minimum(jnp.sum(blk[:, None] >= blk_end[None, :], axis=1), N_EXPERTS - 1).astype(jnp.int32)
    ntile, loff, gdest = ntile.reshape(-1), loff.reshape(-1), gdest.reshape(-1)
    xb = _dispatch(h2, lrow, pad_start, nblk.astype(jnp.int32), ntile, loff, gdest, n_blocks)
    yb = _experts(xb, block_e, n_used, wg, wu, wd)
    return _combine(yb, lrow, ntile, loff, gdest, gates, x, mods6, layer, first_row, rows_per_batch, g3)


def _pad_cols(w, n):
    return jnp.pad(w, ((0, 0), (0, n - w.shape[1])))


def kernel(x_prompt, x_sample, state_mlstm_C, state_mlstm_n, state_mlstm_m, state_ret_S, c, c_ctx, mod_w, mod_b, norm_g, mlstm_w_in, mlstm_gate_b, mlstm_conv_w, mlstm_conv_b, mlstm_head_g, mlstm_w_out, ret_w_in, ret_decay_logit, ret_head_g, ret_w_out, ffn_w_gate, ffn_w_up, ffn_w_down, moe_router, moe_w_gate, moe_w_up, moe_w_down):
    bp, n_p, d = x_prompt.shape
    bs, n_s, _ = x_sample.shape
    depth = mod_w.shape[0]
    assert depth == 2 and ROW_TILE % GRID_W == 0 and n_s % GRID_W == 0

    cond = jnp.zeros((MOD_ROWS, d), F32).at[0].set(c_ctx).at[1:1 + bs].set(c)
    mods6 = _modulation(cond, mod_w, mod_b).reshape(depth, MOD_ROWS, N_MOD, 1, d)

    groups = [dict(x=x_prompt.reshape(bp * n_p, d), first=0, rpb=None, nseq=bp, ntok=n_p, prompt=True),
              dict(x=x_sample.reshape(bs * n_s, d), first=1, rpb=n_s // ROW_TILE, nseq=bs, ntok=n_s, prompt=False)]

    j = 0
    ml_qk = (mlstm_w_in.shape[2] - 4 * HEADS) // 2
    ml_v = ml_qk // 2
    w_in = mlstm_w_in[j]
    wqk = w_in[:, :ml_qk].astype(BF16)
    wv = w_in[:, ml_qk:ml_qk + ml_v].astype(BF16)
    wo = w_in[:, ml_qk + ml_v:ml_qk + 2 * ml_v].astype(BF16)
    w_gate = w_in[:, ml_qk + 2 * ml_v:]
    wgh, wgl = _split_bf16(_pad_cols(w_gate, LANES))
    bcol = _pad_cols(mlstm_gate_b[j][None, :], LANES)
    g = norm_g[0]
    w_out0 = mlstm_w_out[j].astype(BF16)
    new_c = new_n = new_m = None
    casts = (ffn_w_gate[j], ffn_w_up[j], ffn_w_down[j], ret_w_in[j], ret_w_out[j])
    proj = {}
    for grp in reversed(groups):
        args = (mods6, 0, grp["first"], grp["rpb"])
        outs = _proj_mlstm(grp["x"], grp["ntok"], *args, g[0:1], wqk, wv, wo, wgh, wgl, bcol, w_gate.shape[1],
                           mlstm_conv_w[j], mlstm_conv_b[j][None, :], casts=() if grp["prompt"] else casts)
        proj[grp["prompt"]] = outs[:5]
        if not grp["prompt"]:
            fwg, fwu, fwd, rw_in, w_out1 = outs[5:]
    for grp in groups:
        args = (mods6, 0, grp["first"], grp["rpb"])
        q, kt, v, o, grow = proj[grp["prompt"]]
        grow3 = grow.reshape(grow.shape[0], grp["nseq"], grp["ntok"] // CHUNK, CHUNK)
        if grp["prompt"]:
            m0 = jnp.zeros((grp["nseq"] * 2 * HEADS,), F32)
            hs, new_c, new_n, new_m = _mlstm_scan(q, kt, v, grow3, m0, None, grp["nseq"], grp["ntok"], True)
        else:
            (hs,) = _mlstm_scan(q, kt, v, grow3, state_mlstm_m[:, j].reshape(-1),
                                (state_mlstm_C[:, j], state_mlstm_n[:, j]), grp["nseq"], grp["ntok"], False)
        (grp["x"],) = _mixer_out(hs, o, mlstm_head_g[j][None, :], w_out0, grp["x"], *args, g[1:2],
                                 ffn=(g[2:3], g[3:4], fwg, fwu, fwd))

    ret_qk = ret_w_in.shape[2] // 3
    g = norm_g[1]
    wrh, wrl = _split_bf16(_pad_cols(moe_router[j], LANES))
    decay_flat = ret_decay_logit[j].reshape(-1)
    new_s = None
    n_exp, _, d_ff = moe_w_gate.shape[1:]
    casts = (moe_w_gate[j].reshape(n_exp * d, d_ff), moe_w_up[j].reshape(n_exp * d, d_ff),
             moe_w_down[j].reshape(n_exp * d_ff, d))
    proj = {}
    for grp in reversed(groups):
        args = (mods6, 1, grp["first"], grp["rpb"])
        rope_tabs = None if grp["prompt"] else _rope_tables(grp["ntok"], ret_qk // (2 * HEADS))
        outs = _proj_ret(grp["x"], grp["ntok"], *args, g[0:1], rw_in, ret_qk // 2, ret_qk, rope_tabs,
                         casts=() if grp["prompt"] else casts)
        proj[grp["prompt"]] = outs[:4]
        if not grp["prompt"]:
            ewg, ewu, ewd = (w.reshape(n_exp, -1, w.shape[1]) for w in outs[4:])
    for grp in groups:
        args = (mods6, 1, grp["first"], grp["rpb"])
        q, kt, v, gate = proj[grp["prompt"]]
        if grp["prompt"]:
            hs, new_s = _ret_scan(q, kt, v, decay_flat, None, grp["nseq"], grp["ntok"], True)
        else:
            (hs,) = _ret_scan(q, kt, v, decay_flat, state_ret_S[:, j], grp["nseq"], grp["ntok"], False)
        x1, h2, logits = _mixer_out(hs, gate, ret_head_g[j][None, :], w_out1, grp["x"], *args, g[1:2],
                                    router=(g[2:3], wrh, wrl))
        grp["x"] = _moe(h2, logits, x1, *args, g[3:4], ewg, ewu, ewd)

    y_prompt = groups[0]["x"].reshape(bp, n_p, d)
    y_sample = groups[1]["x"].reshape(bs, n_s, d)
    return (y_prompt, y_sample, new_c[:, None], new_n[:, None], new_m[:, None, :, :, 0], new_s[:, None])
```

```python
import functools
import math

import jax
import jax.numpy as jnp
from jax import lax
from jax.experimental import pallas as pl
from jax.experimental.pallas import tpu as pltpu

F32 = jnp.float32
BF16 = jnp.bfloat16

EPS = 1e-6
N_MOD = 6
HEADS = 4
CHUNK = 128
RET_CHUNK = 256
GRID_W = 64
ROPE_BASE = 10000.0
N_EXPERTS = 8
TOP_K = 2
MOE_BLOCK = 256
LANES = 128
SUBLANES = 8
ROW_TILE = 256
MIXER_SUBTILES = 2
MOD_ROWS = 8
STAGE_ROWS = -(-(TOP_K * ROW_TILE + N_EXPERTS * (SUBLANES - 1)) // LANES) * LANES
MIB = 1024 * 1024


def _cparams(sem, vmem_mib):
    return pltpu.CompilerParams(dimension_semantics=sem, vmem_limit_bytes=vmem_mib * MIB)


def _dot(a, b):
    return jnp.dot(a, b, preferred_element_type=F32)


def _dot_nt(a, b):
    return lax.dot_general(a, b, (((1,), (1,)), ((), ())), preferred_element_type=F32)


def _dot_tn(a, b):
    return lax.dot_general(a, b, (((0,), (0,)), ((), ())), preferred_element_type=F32)


def _split_bf16(x):
    hi = x.astype(BF16)
    lo = (x - hi.astype(F32)).astype(BF16)
    return hi, lo


def _rms(x, g):
    return x * lax.rsqrt(jnp.mean(x * x, -1, keepdims=True) + EPS) * g


def _adaln(x, g, shift, scale):
    return _rms(x, g) * (1.0 + scale) + shift


def _silu(x):
    return x * jax.nn.sigmoid(x)


def _logsig(x):
    return jnp.minimum(x, 0.0) - jnp.log1p(jnp.exp(-jnp.abs(x)))


def _const_spec(shape):
    nd = len(shape)
    return pl.BlockSpec(shape, lambda *_: (0,) * nd, pipeline_mode=pl.Buffered(1))


def _mod_spec(d, layer, comp, rows_per_batch, first_row):
    if rows_per_batch is None:
        return pl.BlockSpec((None, None, None, 1, d), lambda i: (layer, first_row, comp, 0, 0))
    return pl.BlockSpec((None, None, None, 1, d), lambda i: (layer, first_row + i // rows_per_batch, comp, 0, 0))


def _mod_kernel(c_ref, w_ref, b_ref, o_ref):
    s = _silu(c_ref[...]).astype(BF16)
    o_ref[...] = _dot(s, w_ref[...].astype(BF16)) + b_ref[...]


def _modulation(cond, mod_w, mod_b):
    depth, d, n = mod_w.shape
    tn = n // 4
    return pl.pallas_call(
        _mod_kernel,
        grid=(depth, n // tn),
        in_specs=[pl.BlockSpec((MOD_ROWS, d), lambda l, j: (0, 0)),
                  pl.BlockSpec((None, d, tn), lambda l, j: (l, 0, j)),
                  pl.BlockSpec((None, 1, tn), lambda l, j: (l, 0, j))],
        out_specs=pl.BlockSpec((None, MOD_ROWS, tn), lambda l, j: (l, 0, j)),
        out_shape=jax.ShapeDtypeStruct((depth, MOD_ROWS, n), F32),
        compiler_params=_cparams(("arbitrary", "arbitrary"), 40),
        name="modulation",
    )(cond, mod_w, mod_b.reshape(depth, 1, n))


BF16_ROWS = 16


def _cast_plan(arrays, n_steps):
    in_specs, out_specs, out_shape, nblks = [], [], [], []
    for a in arrays:
        rows, cols = a.shape
        nblk = max(n for n in range(1, n_steps + 1) if rows % n == 0 and (rows // n) % BF16_ROWS == 0)
        spec = pl.BlockSpec((rows // nblk, cols), lambda i, nb=nblk: (jnp.minimum(i, nb - 1), 0))
        in_specs.append(spec)
        out_specs.append(spec)
        out_shape.append(jax.ShapeDtypeStruct(a.shape, BF16))
        nblks.append(nblk)
    return in_specs, out_specs, out_shape, nblks


def _cast_slabs(srcs, dsts, nblks):
    for src, dst, nb in zip(srcs, dsts, nblks):
        @pl.when(pl.program_id(0) < nb)
        def _():
            dst[...] = src[...].astype(BF16)


def _proj_mlstm_kernel(tiles_per_seq, cast_nblks, *refs):
    nc = len(cast_nblks)
    (x_ref, xp_ref, xn_ref, g_ref, sh_ref, sc_ref, wqk_ref, wv_ref, wo_ref, wgh_ref, wgl_ref, bcol_ref, cw_ref,
     cb_ref) = refs[:14]
    q_ref, kt_ref, v_ref, o_ref, grow_ref = refs[14 + nc:19 + nc]
    _cast_slabs(refs[14:14 + nc], refs[19 + nc:], cast_nblks)
    tm = x_ref.shape[0]
    nq = q_ref.shape[1]
    h = _adaln(x_ref[...], g_ref[...], sh_ref[...], sc_ref[...])
    hb, hl = _split_bf16(h)
    x_halo = jnp.concatenate([xp_ref[...], xn_ref[...]], axis=0)
    hb_halo = _adaln(x_halo, g_ref[...], sh_ref[...], sc_ref[...]).astype(BF16)
    hb_all = jnp.concatenate([hb, hb_halo], axis=0)
    pos = pl.program_id(0) % tiles_per_seq
    has_prev = jnp.where(pos > 0, 1.0, 0.0)
    has_next = jnp.where(pos < tiles_per_seq - 1, 1.0, 0.0)
    wc = nq // HEADS
    rowi = lax.broadcasted_iota(jnp.int32, (tm, wc), 0)
    for c in range(2 * HEADS):
        sl = slice(c * wc, (c + 1) * wc)
        p_all = _dot(hb_all, wqk_ref[:, sl])
        p = p_all[:tm]
        p_prev = p_all[tm + SUBLANES - 1:tm + SUBLANES, :] * has_prev
        p_next = p_all[tm + SUBLANES:tm + SUBLANES + 1, :] * has_next
        prev = jnp.where(rowi == 0, p_prev, pltpu.roll(p, 1, axis=0))
        nxt = jnp.where(rowi == tm - 1, p_next, pltpu.roll(p, tm - 1, axis=0))
        y = prev * cw_ref[0:1, sl] + p * cw_ref[1:2, sl] + nxt * cw_ref[2:3, sl] + cb_ref[:, sl]
        if c < HEADS:
            q_ref[:, sl] = (y * wc ** -0.5).astype(BF16)
        else:
            kt_ref[(c - HEADS) * wc:(c - HEADS + 1) * wc, :] = y.T.astype(BF16)
    v_ref[...] = _dot(hb, wv_ref[...]).astype(BF16)
    o_ref[...] = jax.nn.sigmoid(_dot(hb, wo_ref[...]))
    gc = _dot(hb, wgh_ref[...]) + _dot(hl, wgh_ref[...]) + _dot(hb, wgl_ref[...]) + bcol_ref[...]
    grow_ref[...] = gc.T[:grow_ref.shape[0], :]


def _proj_mlstm(x, n_tok, mods6, layer, first_row, rows_per_batch, g, wqk, wv, wo, wgh, wgl, bcol, ng, conv_w, conv_b,
                casts=()):
    t, d = x.shape
    tm = ROW_TILE
    ms = lambda comp: _mod_spec(d, layer, comp, rows_per_batch, first_row)
    nq = wqk.shape[1] // 2
    tps = tm // SUBLANES
    last = t // SUBLANES - 1
    c_in, c_out, c_shape, c_nblks = _cast_plan(casts, t // tm)
    return pl.pallas_call(
        functools.partial(_proj_mlstm_kernel, n_tok // tm, tuple(c_nblks)),
        grid=(t // tm,),
        in_specs=[pl.BlockSpec((tm, d), lambda i: (i, 0)),
                  pl.BlockSpec((SUBLANES, d), lambda i: (jnp.maximum(i * tps - 1, 0), 0)),
                  pl.BlockSpec((SUBLANES, d), lambda i: (jnp.minimum((i + 1) * tps, last), 0)),
                  _const_spec((1, d)), ms(0), ms(1),
                  _const_spec(wqk.shape), _const_spec(wv.shape), _const_spec(wo.shape),
                  _const_spec(wgh.shape), _const_spec(wgl.shape), _const_spec(bcol.shape),
                  _const_spec(conv_w.shape), _const_spec(conv_b.shape)] + c_in,
        out_specs=[pl.BlockSpec((tm, nq), lambda i: (i, 0)),
                   pl.BlockSpec((nq, tm), lambda i: (0, i)),
                   pl.BlockSpec((tm, wv.shape[1]), lambda i: (i, 0)),
                   pl.BlockSpec((tm, wo.shape[1]), lambda i: (i, 0)),
                   pl.BlockSpec((ng, tm), lambda i: (0, i))] + c_out,
        out_shape=[jax.ShapeDtypeStruct((t, nq), BF16),
                   jax.ShapeDtypeStruct((nq, t), BF16),
                   jax.ShapeDtypeStruct((t, wv.shape[1]), BF16),
                   jax.ShapeDtypeStruct((t, wo.shape[1]), F32),
                   jax.ShapeDtypeStruct((ng, t), F32)] + c_shape,
        compiler_params=_cparams(("arbitrary",), 56),
        name="proj_mlstm",
    )(x, x, x, g, mods6, mods6, wqk, wv, wo, wgh, wgl, bcol, conv_w, conv_b, *casts)


def _make_mlstm_scan_kernel(n_tok, dk, dv, has_state, emit_state):
    L = CHUNK
    nc = n_tok // L
    assert nc % 2 == 0 and L == LANES

    def kernel(*refs):
        it = iter(refs)
        m0_ref = next(it)
        q_ref, kt_ref, v_ref = next(it), next(it), next(it)
        g_ref = next(it)
        if has_state:
            c0_ref, n0_ref = next(it), next(it)
        out_ref = next(it)
        if emit_state:
            cout_ref, nout_ref, mout_ref = next(it), next(it), next(it)
        bsc, rsc, rmsc, cst, cbf, nrep, nbf = (next(it) for _ in range(7))

        b = pl.program_id(0)
        hh = pl.program_id(1)

        ri = lax.broadcasted_iota(jnp.int32, (L, L), 0)
        ci = lax.broadcasted_iota(jnp.int32, (L, L), 1)
        lower = ri >= ci
        upper = ri <= ci
        masks = ((lower, upper), (upper, lower))

        lane = lax.broadcasted_iota(jnp.int32, (nc, L), 1)

        def lane_scan(x, op, fill, reverse):
            s = 1
            while s < L:
                if reverse:
                    x = op(x, jnp.where(lane < L - s, pltpu.roll(x, L - s, axis=1), fill))
                else:
                    x = op(x, jnp.where(lane >= s, pltpu.roll(x, s, axis=1), fill))
                s *= 2
            return x

        @pl.when(hh == 0)
        def _():
            for d in range(2):
                for h in range(HEADS):
                    row_i, row_f = d * 2 * HEADS + h, d * 2 * HEADS + HEADS + h
                    b_all = lane_scan(_logsig(g_ref[row_f]), jnp.add, 0.0, reverse=(d == 1))
                    r_all = g_ref[row_i] - b_all
                    bsc[d, h] = b_all
                    rsc[d, h] = r_all
                    rmsc[d, h] = lane_scan(r_all, jnp.maximum, -jnp.inf, reverse=(d == 1))

        if has_state:
            cst[...] = c0_ref[...]
            for d in range(2):
                nrep[d] = jnp.broadcast_to(n0_ref[d], (dk, LANES))
        else:
            cst[...] = jnp.zeros_like(cst)
            nrep[...] = jnp.zeros_like(nrep)
        cbf[...] = cst[...].astype(BF16)
        nbf[...] = nrep[...].astype(BF16)
        m_init = tuple(jnp.full((1, LANES), m0_ref[(b * 2 + d) * HEADS + hh], F32) for d in range(2))

        def lanes(x, n):
            return jnp.concatenate([x] * (n // LANES), axis=1)

        def chunk(d, c, m):
            mk = masks[d][0]
            r0 = pl.multiple_of(c * L, L)
            q = q_ref[pl.ds(r0, L), :]
            kt = kt_ref[:, pl.ds(r0, L)]
            v = v_ref[pl.ds(r0, L), :]
            rr = rsc[d, hh, pl.ds(c, 1), :]
            b_rep = jnp.broadcast_to(bsc[d, hh, pl.ds(c, 1), :], (L, L)).T
            rm_rep = jnp.broadcast_to(rmsc[d, hh, pl.ds(c, 1), :], (L, L)).T
            end = 0 if d == 1 else L - 1
            b_end, rm_end = b_rep[end:end + 1, :], rm_rep[end:end + 1, :]
            mm = jnp.maximum(m, rm_rep)
            qkn = _dot(q, jnp.concatenate([kt, nbf[d]], axis=1))
            s = qkn[:, :L] * jnp.where(mk, jnp.exp(rr - mm), 0.0)
            w_inter = jnp.exp(m - mm)
            num = lanes(w_inter, dv) * _dot(q, cbf[d]) + _dot(s.astype(BF16), v)
            den = w_inter * qkn[:, L:] + jnp.sum(s, axis=1, keepdims=True)
            inv = 1.0 / jnp.maximum(jnp.abs(den), jnp.exp(-(b_rep + mm)))
            h = num * lanes(inv, dv)
            m_end = jnp.maximum(m, rm_end)
            decay = jnp.exp(m - m_end)
            kw = kt.astype(F32) * jnp.exp(rr - m_end)
            c_new = lanes(decay, dv) * cst[d] + _dot(kw.astype(BF16), v)
            n_new = decay * nrep[d] + jnp.sum(kw, axis=1, keepdims=True)
            cst[d] = c_new
            cbf[d] = c_new.astype(BF16)
            nrep[d] = n_new
            nbf[d] = n_new.astype(BF16)
            return h, b_end + m_end

        def make_body(accumulate):
            def body(i, carry):
                mf, mb = carry
                hf, mf = chunk(0, i, mf)
                rf = pl.multiple_of(i * L, L)
                cb = nc - 1 - i
                hb, mb = chunk(1, cb, mb)
                rb = pl.multiple_of(cb * L, L)
                if accumulate:
                    out_ref[pl.ds(rf, L), :] += hf
                    out_ref[pl.ds(rb, L), :] += hb
                else:
                    out_ref[pl.ds(rf, L), :] = hf
                    out_ref[pl.ds(rb, L), :] = hb
                return mf, mb
            return body

        unroll = 2 if nc % 4 == 0 else 1
        carry = lax.fori_loop(0, nc // 2, make_body(False), m_init, unroll=unroll)
        carry = lax.fori_loop(nc // 2, nc, make_body(True), carry, unroll=unroll)

        if emit_state:
            cout_ref[...] = cst[...]
            for d in range(2):
                nout_ref[d, pl.ds(hh, 1), :] = nrep[d].T[0:1, :]
                mout_ref[d, pl.ds(hh, 1), :] = carry[d]

    return kernel


def _mlstm_scan(q, kt, v, gates, m0_flat, state, n_seq, n_tok, emit_state):
    dk = q.shape[1] // HEADS
    dv = v.shape[1] // HEADS
    nc = n_tok // CHUNK
    has_state = state is not None
    kern = _make_mlstm_scan_kernel(n_tok, dk, dv, has_state, emit_state)
    in_specs = [pl.BlockSpec((n_tok, dk), lambda b, h, m: (b, h)),
                pl.BlockSpec((dk, n_tok), lambda b, h, m: (h, b)),
                pl.BlockSpec((n_tok, dv), lambda b, h, m: (b, h)),
                pl.BlockSpec((gates.shape[0], None, nc, CHUNK), lambda b, h, m: (0, b, 0, 0))]
    args = [q, kt, v, gates]
    if has_state:
        in_specs += [pl.BlockSpec((None, 2, None, dk, dv), lambda b, h, m: (b, 0, h, 0, 0)),
                     pl.BlockSpec((None, 2, None, dk, 1), lambda b, h, m: (b, 0, h, 0, 0))]
        args += [state[0], state[1][..., None]]
    out_specs = [pl.BlockSpec((n_tok, dv), lambda b, h, m: (b, h))]
    out_shape = [jax.ShapeDtypeStruct((n_seq * n_tok, HEADS * dv), F32)]
    if emit_state:
        out_specs += [pl.BlockSpec((None, 2, None, dk, dv), lambda b, h, m: (b, 0, h, 0, 0)),
                      pl.BlockSpec((None, 2, HEADS, dk), lambda b, h, m: (b, 0, 0, 0)),
                      pl.BlockSpec((None, 2, HEADS, LANES), lambda b, h, m: (b, 0, 0, 0))]
        out_shape += [jax.ShapeDtypeStruct((n_seq, 2, HEADS, dk, dv), F32),
                      jax.ShapeDtypeStruct((n_seq, 2, HEADS, dk), F32),
                      jax.ShapeDtypeStruct((n_seq, 2, HEADS, LANES), F32)]
    grid_spec = pltpu.PrefetchScalarGridSpec(
        num_scalar_prefetch=1, grid=(n_seq, HEADS), in_specs=in_specs, out_specs=out_specs,
        scratch_shapes=[pltpu.VMEM((2, HEADS, nc, CHUNK), F32),
                        pltpu.VMEM((2, HEADS, nc, CHUNK), F32),
                        pltpu.VMEM((2, HEADS, nc, CHUNK), F32),
                        pltpu.VMEM((2, dk, dv), F32), pltpu.VMEM((2, dk, dv), BF16),
                        pltpu.VMEM((2, dk, LANES), F32), pltpu.VMEM((2, dk, LANES), BF16)])
    return pl.pallas_call(
        kern, grid_spec=grid_spec, out_shape=out_shape,
        compiler_params=_cparams(("arbitrary", "arbitrary"), 56),
        name="mlstm_scan",
    )(m0_flat, *args)


def _proj_ret_kernel(rope, cast_nblks, *refs):
    nc = len(cast_nblks)
    n_in = 5 + (4 if rope else 0)
    x_ref, g_ref, sh_ref, sc_ref, w_ref = refs[:5]
    if rope:
        rcos_ref, rsin_ref, ccos_ref, csin_ref = refs[5:9]
    q_ref, kt_ref, v_ref, gate_ref = refs[n_in + nc:n_in + nc + 4]
    _cast_slabs(refs[n_in:n_in + nc], refs[n_in + nc + 4:], cast_nblks)
    nq, nv = q_ref.shape[1], v_ref.shape[1]
    wc = nq // HEADS
    hb = _adaln(x_ref[...], g_ref[...], sh_ref[...], sc_ref[...]).astype(BF16)
    if rope:
        cos = jnp.concatenate([rcos_ref[...], ccos_ref[...]], axis=1)
        sin = jnp.concatenate([rsin_ref[...], csin_ref[...]], axis=1)
    for c in range(2 * HEADS):
        p = _dot(hb, w_ref[:, c * wc:(c + 1) * wc])
        if rope:
            swapped = jnp.concatenate([pltpu.roll(p[:, j * LANES:(j + 1) * LANES], LANES // 2, axis=1)
                                       for j in range(wc // LANES)], axis=1)
            p = p * cos + swapped * sin
        if c < HEADS:
            q_ref[:, c * wc:(c + 1) * wc] = p.astype(BF16)
        else:
            kt_ref[(c - HEADS) * wc:(c - HEADS + 1) * wc, :] = (p * wc ** -0.5).T.astype(BF16)
    v_ref[...] = _dot(hb, w_ref[:, 2 * nq:2 * nq + nv]).astype(BF16)
    gate_ref[...] = _silu(_dot(hb, w_ref[:, 2 * nq + nv:])).astype(BF16)


def _proj_ret(x, n_tok, mods6, layer, first_row, rows_per_batch, g, w_in, nq, nv, rope_tabs, casts=()):
    t, d = x.shape
    tm = ROW_TILE
    ms = lambda comp: _mod_spec(d, layer, comp, rows_per_batch, first_row)
    ng = w_in.shape[1] - 2 * nq - nv
    rope = rope_tabs is not None
    in_specs = [pl.BlockSpec((tm, d), lambda i: (i, 0)), _const_spec((1, d)), ms(0), ms(1), _const_spec(w_in.shape)]
    args = [x, g, mods6, mods6, w_in]
    if rope:
        tiles_per_seq = n_tok // tm
        row_spec = pl.BlockSpec((tm, LANES), lambda i: (i % tiles_per_seq, 0))
        in_specs += [row_spec, row_spec, _const_spec((tm, LANES)), _const_spec((tm, LANES))]
        args += list(rope_tabs)
    c_in, c_out, c_shape, c_nblks = _cast_plan(casts, t // tm)
    return pl.pallas_call(
        functools.partial(_proj_ret_kernel, rope, tuple(c_nblks)),
        grid=(t // tm,),
        in_specs=in_specs + c_in,
        out_specs=[pl.BlockSpec((tm, nq), lambda i: (i, 0)),
                   pl.BlockSpec((nq, tm), lambda i: (0, i)),
                   pl.BlockSpec((tm, nv), lambda i: (i, 0)),
                   pl.BlockSpec((tm, ng), lambda i: (i, 0))] + c_out,
        out_shape=[jax.ShapeDtypeStruct((t, nq), BF16),
                   jax.ShapeDtypeStruct((nq, t), BF16),
                   jax.ShapeDtypeStruct((t, nv), BF16),
                   jax.ShapeDtypeStruct((t, ng), BF16)] + c_shape,
        compiler_params=_cparams(("arbitrary",), 56),
        name="proj_ret",
    )(*args, *casts)


def _make_ret_scan_kernel(n_tok, dk, dv, has_state, emit_state):
    L = RET_CHUNK
    nc = n_tok // L
    assert nc == 1 or nc % 2 == 0

    def kernel(*refs):
        it = iter(refs)
        dl_ref = next(it)
        q_ref, kt_ref, v_ref = next(it), next(it), next(it)
        if has_state:
            s0_ref = next(it)
        out_ref = next(it)
        if emit_state:
            sout_ref = next(it)
        sst, sbf = next(it), next(it)
        if nc > 1:
            acc = next(it)

        hh = pl.program_id(1)
        ri = lax.broadcasted_iota(jnp.int32, (L, L), 0)
        ci = lax.broadcasted_iota(jnp.int32, (L, L), 1)
        rel = (ri - ci).astype(F32)
        pos_col = lax.broadcasted_iota(jnp.int32, (L, LANES), 0).astype(F32)
        pos_row = lax.broadcasted_iota(jnp.int32, (1, L), 1).astype(F32)

        dmat, q_dec, k_dec, c_dec = [], [], [], []
        for d in range(2):
            lg = _logsig(jnp.full((1, 1), dl_ref[d * HEADS + hh], F32))
            if d == 0:
                dmat.append(jnp.where(ri >= ci, jnp.exp(lg * jnp.maximum(rel, 0.0)), 0.0))
                q_dec.append(jnp.exp(lg * (pos_col + 1.0)))
                k_dec.append(jnp.exp(lg * (L - 1.0 - pos_row)))
            else:
                dmat.append(jnp.where(ri <= ci, jnp.exp(lg * jnp.maximum(-rel, 0.0)), 0.0))
                q_dec.append(jnp.exp(lg * (L - pos_col)))
                k_dec.append(jnp.exp(lg * pos_row))
            c_dec.append(jnp.broadcast_to(jnp.exp(lg * float(L)), (1, LANES)))

        if has_state:
            sst[...] = s0_ref[...]
        else:
            sst[...] = jnp.zeros_like(sst)
        sbf[...] = sst[...].astype(BF16)

        def lanes(x, n):
            return jnp.concatenate([x] * (n // LANES), axis=1)

        def chunk(d, c):
            r0 = c * L if isinstance(c, int) else pl.multiple_of(c * L, L)
            q = q_ref[pl.ds(r0, L), :]
            kt = kt_ref[:, pl.ds(r0, L)]
            v = v_ref[pl.ds(r0, L), :]
            s = _dot(q, kt) * dmat[d]
            out = _dot(s.astype(BF16), v) + lanes(q_dec[d], dv) * _dot(q, sbf[d])
            s_new = lanes(c_dec[d], dv) * sst[d] + _dot((kt.astype(F32) * k_dec[d]).astype(BF16), v)
            sst[d] = s_new
            sbf[d] = s_new.astype(BF16)
            return out

        def body1(i, _):
            rf = pl.multiple_of(i * L, L)
            acc[pl.ds(rf, L), :] = chunk(0, i)
            cb = nc - 1 - i
            rb = pl.multiple_of(cb * L, L)
            acc[pl.ds(rb, L), :] = chunk(1, cb)
            return 0

        def body2(i, _):
            rf = pl.multiple_of(i * L, L)
            out_ref[pl.ds(rf, L), :] = (acc[pl.ds(rf, L), :] + chunk(0, i)).astype(BF16)
            cb = nc - 1 - i
            rb = pl.multiple_of(cb * L, L)
            out_ref[pl.ds(rb, L), :] = (acc[pl.ds(rb, L), :] + chunk(1, cb)).astype(BF16)
            return 0

        if nc == 1:
            out_ref[...] = (chunk(0, 0) + chunk(1, 0)).astype(BF16)
        else:
            unroll = 2 if nc % 4 == 0 else 1
            lax.fori_loop(0, nc // 2, body1, 0, unroll=unroll)
            lax.fori_loop(nc // 2, nc, body2, 0, unroll=unroll)
        if emit_state:
            sout_ref[...] = sst[...]

    return kernel


def _ret_scan(q, kt, v, decay_flat, state, n_seq, n_tok, emit_state):
    dk = q.shape[1] // HEADS
    dv = v.shape[1] // HEADS
    has_state = state is not None
    kern = _make_ret_scan_kernel(n_tok, dk, dv, has_state, emit_state)
    in_specs = [pl.BlockSpec((n_tok, dk), lambda b, h, m: (b, h)),
                pl.BlockSpec((dk, n_tok), lambda b, h, m: (h, b)),
                pl.BlockSpec((n_tok, dv), lambda b, h, m: (b, h))]
    args = [q, kt, v]
    if has_state:
        in_specs.append(pl.BlockSpec((None, 2, None, dk, dv), lambda b, h, m: (b, 0, h, 0, 0)))
        args.append(state)
    out_specs = [pl.BlockSpec((n_tok, dv), lambda b, h, m: (b, h))]
    out_shape = [jax.ShapeDtypeStruct((n_seq * n_tok, HEADS * dv), BF16)]
    if emit_state:
        out_specs.append(pl.BlockSpec((None, 2, None, dk, dv), lambda b, h, m: (b, 0, h, 0, 0)))
        out_shape.append(jax.ShapeDtypeStruct((n_seq, 2, HEADS, dk, dv), F32))
    grid_spec = pltpu.PrefetchScalarGridSpec(
        num_scalar_prefetch=1, grid=(n_seq, HEADS), in_specs=in_specs, out_specs=out_specs,
        scratch_shapes=[pltpu.VMEM((2, dk, dv), F32), pltpu.VMEM((2, dk, dv), BF16)]
        + ([pltpu.VMEM((n_tok, dv), F32)] if n_tok > RET_CHUNK else []))
    return pl.pallas_call(
        kern, grid_spec=grid_spec, out_shape=out_shape,
        compiler_params=_cparams(("arbitrary", "arbitrary"), 56),
        name="ret_scan",
    )(decay_flat, *args)


def _rope_tables(n_tok, dk):
    r = dk // 4
    inv = 1.0 / (ROPE_BASE ** (jnp.arange(r, dtype=F32) / r))
    sign = jnp.concatenate([-jnp.ones((r,), F32), jnp.ones((r,), F32)])
    rows = (jnp.arange(n_tok) // GRID_W).astype(F32)[:, None] * inv
    cols = (jnp.arange(ROW_TILE) % GRID_W).astype(F32)[:, None] * inv
    two = lambda a: jnp.concatenate([a, a], axis=-1)
    return (two(jnp.cos(rows)), two(jnp.sin(rows)) * sign, two(jnp.cos(cols)), two(jnp.sin(cols)) * sign)


def _make_mixer_out_kernel(dv, with_router, with_ffn):
    def kernel(*refs):
        it = iter(refs)
        hs_ref, gate_ref, hg_ref, w_ref, x_ref, g1_ref, ga_ref = (next(it) for _ in range(7))
        if with_router:
            g2_ref, fsh_ref, fsc_ref, wrh_ref, wrl_ref = (next(it) for _ in range(5))
        if with_ffn:
            g2_ref, fsh_ref, fsc_ref, wg_ref, wu_ref, wd_ref, g3_ref, fga_ref = (next(it) for _ in range(8))
        xo_ref = next(it)
        if with_router:
            h2_ref, lg_ref = next(it), next(it)
        z_ref = next(it)
        subs = [slice(r * ROW_TILE, (r + 1) * ROW_TILE) for r in range(x_ref.shape[0] // ROW_TILE)]
        for rs in subs:
            for hh in range(HEADS):
                sl = slice(hh * dv, (hh + 1) * dv)
                seg = hs_ref[rs, sl].astype(F32)
                y = seg * lax.rsqrt(jnp.mean(seg * seg, -1, keepdims=True) + EPS) * hg_ref[:, sl]
                z_ref[rs, sl] = (gate_ref[rs, sl].astype(F32) * y).astype(BF16)
        ys = [_dot(z_ref[rs, :], w_ref[...]) for rs in subs]
        xns = [x_ref[rs, :] + ga_ref[...] * _rms(y, g1_ref[...]) for rs, y in zip(subs, ys)]
        if with_ffn:
            hbs = [_adaln(xn, g2_ref[...], fsh_ref[...], fsc_ref[...]).astype(BF16) for xn in xns]
            acts = [(_silu(_dot(hb, wg_ref[...])) * _dot(hb, wu_ref[...])).astype(BF16) for hb in hbs]
            downs = [_dot(a, wd_ref[...]) for a in acts]
            xns = [xn + fga_ref[...] * _rms(dn, g3_ref[...]) for xn, dn in zip(xns, downs)]
        for rs, xn in zip(subs, xns):
            xo_ref[rs, :] = xn
        if with_router:
            for rs, xn in zip(subs, xns):
                h2 = _adaln(xn, g2_ref[...], fsh_ref[...], fsc_ref[...])
                h2_ref[rs, :] = h2
                hb, hl = _split_bf16(h2)
                lg_ref[rs, :] = _dot(hb, wrh_ref[...]) + _dot(hl, wrh_ref[...]) + _dot(hb, wrl_ref[...])
    return kernel


def _mixer_out(hs, gate, head_g, w_out, x, mods6, layer, first_row, rows_per_batch, g1, router=None, ffn=None):
    t, d = x.shape
    vdim = hs.shape[1]
    tm = MIXER_SUBTILES * ROW_TILE
    tiles_per_batch = None if rows_per_batch is None else rows_per_batch // MIXER_SUBTILES
    ms = lambda comp: _mod_spec(d, layer, comp, tiles_per_batch, first_row)
    with_router = router is not None
    with_ffn = ffn is not None
    in_specs = [pl.BlockSpec((tm, vdim), lambda i: (i, 0)),
                pl.BlockSpec((tm, vdim), lambda i: (i, 0)),
                _const_spec((1, vdim)), _const_spec(w_out.shape),
                pl.BlockSpec((tm, d), lambda i: (i, 0)), _const_spec((1, d)), ms(2)]
    args = [hs, gate, head_g, w_out, x, g1, mods6]
    out_specs = [pl.BlockSpec((tm, d), lambda i: (i, 0))]
    out_shape = [jax.ShapeDtypeStruct((t, d), F32)]
    if with_router:
        g2, wrh, wrl = router
        in_specs += [_const_spec((1, d)), ms(3), ms(4), _const_spec(wrh.shape), _const_spec(wrl.shape)]
        args += [g2, mods6, mods6, wrh, wrl]
        out_specs += [pl.BlockSpec((tm, d), lambda i: (i, 0)), pl.BlockSpec((tm, LANES), lambda i: (i, 0))]
        out_shape += [jax.ShapeDtypeStruct((t, d), F32), jax.ShapeDtypeStruct((t, LANES), F32)]
    if with_ffn:
        g2, g3, wg, wu, wd = ffn
        in_specs += [_const_spec((1, d)), ms(3), ms(4), _const_spec(wg.shape), _const_spec(wu.shape),
                     _const_spec(wd.shape), _const_spec((1, d)), ms(5)]
        args += [g2, mods6, mods6, wg, wu, wd, g3, mods6]
    return pl.pallas_call(
        _make_mixer_out_kernel(vdim // HEADS, with_router, with_ffn),
        grid=(t // tm,), in_specs=in_specs, out_specs=out_specs, out_shape=out_shape,
        scratch_shapes=[pltpu.VMEM((tm, vdim), BF16)],
        compiler_params=_cparams(("arbitrary",), 58),
        name="mixer_ffn" if with_ffn else "mixer_out",
    )(*args)


def _router_kernel(lg_ref, lrow_ref, gt_ref, tab_ref, tot_ref, carry):
    tb = ROW_TILE
    n_sub = lg_ref.shape[0] // tb

    @pl.when(pl.program_id(0) == 0)
    def _():
        carry[...] = jnp.zeros_like(carry)

    lane = lax.broadcasted_iota(jnp.int32, (tb, LANES), 1)
    ri = lax.broadcasted_iota(jnp.int32, (tb, tb), 0)
    ci = lax.broadcasted_iota(jnp.int32, (tb, tb), 1)
    before = jnp.where(ri > ci, 1.0, 0.0).astype(BF16)
    ei = lax.broadcasted_iota(jnp.int32, (LANES, LANES), 0)
    ej = lax.broadcasted_iota(jnp.int32, (LANES, LANES), 1)
    earlier = jnp.where(ei < ej, 1.0, 0.0).astype(BF16)
    sub = lax.broadcasted_iota(jnp.int32, (SUBLANES, LANES), 0)
    rmax = lambda x: jnp.max(x, axis=1, keepdims=True)
    rmin = lambda x: jnp.min(x, axis=1, keepdims=True)
    rsum = lambda x: jnp.sum(x, axis=1, keepdims=True)

    lgs = [jnp.where(lane < N_EXPERTS, lg_ref[k * tb:(k + 1) * tb, :], -jnp.inf) for k in range(n_sub)]
    v1s = [rmax(lg) for lg in lgs]
    i1s = [rmin(jnp.where(lg == v1, lane, LANES)) for lg, v1 in zip(lgs, v1s)]
    lg2s = [jnp.where(lane == i1, -jnp.inf, lg) for lg, i1 in zip(lgs, i1s)]
    v2s = [rmax(lg2) for lg2 in lg2s]
    i2s = [rmin(jnp.where(lg2 == v2, lane, LANES)) for lg2, v2 in zip(lg2s, v2s)]
    oh1s = [lane == i1 for i1 in i1s]
    oh2s = [lane == i2 for i2 in i2s]
    onehots = [jnp.where(a | b, 1.0, 0.0) for a, b in zip(oh1s, oh2s)]
    ranks = [_dot(before, oh.astype(BF16)) for oh in onehots]
    tiless = [jnp.ceil(jnp.sum(oh, axis=0, keepdims=True) * (1.0 / SUBLANES)) for oh in onehots]
    offs = [_dot(jnp.broadcast_to(tl, (SUBLANES, LANES)).astype(BF16), earlier)[0:1] * SUBLANES for tl in tiless]
    r1s = [rsum(jnp.where(oh, rk + off, 0.0)) for oh, rk, off in zip(oh1s, ranks, offs)]
    r2s = [rsum(jnp.where(oh, rk + off, 0.0)) for oh, rk, off in zip(oh2s, ranks, offs)]
    prior = carry[...]
    for k in range(n_sub):
        rs = slice(k * tb, (k + 1) * tb)
        ex = jnp.exp(v2s[k] - v1s[k])
        den = 1.0 + ex
        lrow_ref[rs, :] = jnp.where(lane == 0, r1s[k], jnp.where(lane == 1, r2s[k], 0.0)).astype(jnp.int32)
        gt_ref[rs, :] = jnp.where(lane == 0, 1.0 / den, jnp.where(lane == 1, ex / den, 0.0))
        tab = jnp.where(sub == 0, tiless[k], jnp.where(sub == 1, offs[k], jnp.where(sub == 2, prior, 0.0)))
        tab_ref[k * SUBLANES:(k + 1) * SUBLANES, :] = tab.astype(jnp.int32)
        prior = prior + tiless[k] * SUBLANES
    carry[...] = prior
    tot_ref[...] = prior.astype(jnp.int32)


RUN_TILES = 8


def _tile_copies(nt_ref, lo_ref, gd_ref, blk, make_copy, wait):
    def go(lo, gd, rows):
        cp = make_copy(pl.multiple_of(lo, SUBLANES), pl.multiple_of(gd, SUBLANES), rows)
        if wait:
            cp.wait()
        else:
            cp.start()

    for e in range(N_EXPERTS):
        idx = blk * N_EXPERTS + e
        nt, lo, gd = nt_ref[idx], lo_ref[idx], gd_ref[idx]

        def run(j, _):
            off = j * (RUN_TILES * SUBLANES)
            go(lo + off, gd + off, RUN_TILES * SUBLANES)
            return 0

        lax.fori_loop(0, nt // RUN_TILES, run, 0)
        k = RUN_TILES // 2
        while k >= 1:
            @pl.when(nt % (2 * k) >= k)
            def _():
                off = (nt // (2 * k)) * (2 * k) * SUBLANES
                go(lo + off, gd + off, k * SUBLANES)
            k //= 2


def _make_dispatch_kernel(tb, n_blocks):
    def kernel(ps_ref, nb_ref, nt_ref, lo_ref, gd_ref, lrow_ref, h_ref, xb_ref, xs, zbuf, sem):
        i = pl.program_id(0)

        @pl.when(i == 0)
        def _():
            zbuf[...] = jnp.zeros_like(zbuf)

            def zero_block(row0):
                dst = pl.multiple_of(row0, MOE_BLOCK)
                cp = pltpu.make_async_copy(zbuf, xb_ref.at[pl.ds(dst, MOE_BLOCK), :], sem.at[2])
                cp.start()
                cp.wait()

            for e in range(N_EXPERTS):
                @pl.when(nb_ref[e] > 0)
                def _():
                    zero_block(ps_ref[e] + (nb_ref[e] - 1) * MOE_BLOCK)

            used = ps_ref[N_EXPERTS - 1] // MOE_BLOCK + nb_ref[N_EXPERTS - 1]

            def tail(j, _):
                zero_block(j * MOE_BLOCK)
                return 0

            lax.fori_loop(used, n_blocks, tail, 0)

        slot = i % 2
        lr = lrow_ref[...]
        r_iota = lax.broadcasted_iota(jnp.int32, (tb, STAGE_ROWS), 1)
        sel = jnp.where((r_iota == lr[:, 0:1]) | (r_iota == lr[:, 1:2]), 1.0, 0.0).astype(BF16)
        xs[slot] = _dot_tn(sel, h_ref[...].astype(BF16))

        def copies_from(buf):
            def make_copy(lo, gd, rows):
                return pltpu.make_async_copy(xs.at[buf, pl.ds(lo, rows), :], xb_ref.at[pl.ds(gd, rows), :],
                                             sem.at[buf])
            return make_copy

        _tile_copies(nt_ref, lo_ref, gd_ref, i, copies_from(slot), wait=False)

        @pl.when(i > 0)
        def _():
            _tile_copies(nt_ref, lo_ref, gd_ref, i - 1, copies_from(1 - slot), wait=True)

        @pl.when(i == pl.num_programs(0) - 1)
        def _():
            _tile_copies(nt_ref, lo_ref, gd_ref, i, copies_from(slot), wait=True)

    return kernel


def _dispatch(h2, lrow, pad_start, nblk, ntile, loff, gdest, n_blocks):
    t, d = h2.shape
    tb = ROW_TILE
    grid_spec = pltpu.PrefetchScalarGridSpec(
        num_scalar_prefetch=5, grid=(t // tb,),
        in_specs=[pl.BlockSpec((tb, LANES), lambda i, *_: (i, 0)),
                  pl.BlockSpec((tb, d), lambda i, *_: (i, 0))],
        out_specs=pl.BlockSpec(memory_space=pl.ANY),
        scratch_shapes=[pltpu.VMEM((2, STAGE_ROWS, d), F32), pltpu.VMEM((MOE_BLOCK, d), F32),
                        pltpu.SemaphoreType.DMA((3,))])
    return pl.pallas_call(
        _make_dispatch_kernel(tb, n_blocks), grid_spec=grid_spec,
        out_shape=jax.ShapeDtypeStruct((n_blocks * MOE_BLOCK, d), F32),
        compiler_params=_cparams(("arbitrary",), 24),
        name="moe_dispatch",
    )(pad_start, nblk, ntile, loff, gdest, lrow, h2)


def _expert_kernel(be_ref, nu_ref, xb_ref, wg_ref, wu_ref, wd_ref, yb_ref):
    i = pl.program_id(0)

    @pl.when(i < nu_ref[0])
    def _():
        xb = xb_ref[...].astype(BF16)
        a = (_silu(_dot(xb, wg_ref[...])) * _dot(xb, wu_ref[...])).astype(BF16)
        yb_ref[...] = _dot(a, wd_ref[...]).astype(BF16).astype(F32)

    @pl.when(i >= nu_ref[0])
    def _():
        yb_ref[...] = jnp.zeros_like(yb_ref)


def _experts(xb, block_e, n_used, wg, wu, wd):
    _, d, f = wg.shape
    n_blocks = xb.shape[0] // MOE_BLOCK
    blk = pl.BlockSpec((MOE_BLOCK, d), lambda i, be, nu: (i, 0))
    grid_spec = pltpu.PrefetchScalarGridSpec(
        num_scalar_prefetch=2, grid=(n_blocks,),
        in_specs=[blk,
                  pl.BlockSpec((None, d, f), lambda i, be, nu: (be[i], 0, 0)),
                  pl.BlockSpec((None, d, f), lambda i, be, nu: (be[i], 0, 0)),
                  pl.BlockSpec((None, f, d), lambda i, be, nu: (be[i], 0, 0))],
        out_specs=blk)
    return pl.pallas_call(
        _expert_kernel, grid_spec=grid_spec,
        out_shape=jax.ShapeDtypeStruct(xb.shape, F32),
        compiler_params=_cparams(("arbitrary",), 56),
        name="moe_experts",
    )(block_e, n_used, xb, wg, wu, wd)


def _make_combine_kernel(tb):
    def kernel(nt_ref, lo_ref, gd_ref, yb_ref, lrow_ref, gt_ref, x_ref, g3_ref, fga_ref, o_ref, ys, sem):
        i = pl.program_id(0)
        slot = i % 2

        def copies_into(buf):
            def make_copy(lo, gd, rows):
                return pltpu.make_async_copy(yb_ref.at[pl.ds(gd, rows), :], ys.at[buf, pl.ds(lo, rows), :],
                                             sem.at[buf])
            return make_copy

        @pl.when(i == 0)
        def _():
            ys[...] = jnp.zeros_like(ys)
            _tile_copies(nt_ref, lo_ref, gd_ref, 0, copies_into(0), wait=False)

        @pl.when(i + 1 < pl.num_programs(0))
        def _():
            _tile_copies(nt_ref, lo_ref, gd_ref, i + 1, copies_into(1 - slot), wait=False)

        _tile_copies(nt_ref, lo_ref, gd_ref, i, copies_into(slot), wait=True)

        lr = lrow_ref[...]
        gt = gt_ref[...]
        r_iota = lax.broadcasted_iota(jnp.int32, (tb, STAGE_ROWS), 1)
        q = jnp.where(r_iota == lr[:, 0:1], gt[:, 0:1], 0.0) + jnp.where(r_iota == lr[:, 1:2], gt[:, 1:2], 0.0)
        qh, ql = _split_bf16(q)
        y = ys[slot].astype(BF16)
        f = _dot(qh, y) + _dot(ql, y)
        o_ref[...] = x_ref[...] + fga_ref[...] * _rms(f, g3_ref[...])

    return kernel


def _combine(yb, lrow, ntile, loff, gdest, gates, x, mods6, layer, first_row, rows_per_batch, g3):
    t, d = x.shape
    tb = ROW_TILE
    if rows_per_batch is None:
        fga_map = lambda i, *_: (layer, first_row, 5, 0, 0)
    else:
        fga_map = lambda i, *_: (layer, first_row + i // rows_per_batch, 5, 0, 0)
    grid_spec = pltpu.PrefetchScalarGridSpec(
        num_scalar_prefetch=3, grid=(t // tb,),
        in_specs=[pl.BlockSpec(memory_space=pl.ANY),
                  pl.BlockSpec((tb, LANES), lambda i, *_: (i, 0)),
                  pl.BlockSpec((tb, LANES), lambda i, *_: (i, 0)),
                  pl.BlockSpec((tb, d), lambda i, *_: (i, 0)),
                  pl.BlockSpec((1, d), lambda i, *_: (0, 0)),
                  pl.BlockSpec((None, None, None, 1, d), fga_map)],
        out_specs=pl.BlockSpec((tb, d), lambda i, *_: (i, 0)),
        scratch_shapes=[pltpu.VMEM((2, STAGE_ROWS, d), F32), pltpu.SemaphoreType.DMA((2,))])
    return pl.pallas_call(
        _make_combine_kernel(tb), grid_spec=grid_spec,
        out_shape=jax.ShapeDtypeStruct((t, d), F32),
        compiler_params=_cparams(("arbitrary",), 32),
        name="moe_combine",
    )(ntile, loff, gdest, yb, lrow, gates, x, g3, mods6)


ROUTER_BLOCKS = 4


def _router(logits):
    t = logits.shape[0]
    tb = ROUTER_BLOCKS * ROW_TILE
    blk = pl.BlockSpec((tb, LANES), lambda i: (i, 0))
    return pl.pallas_call(
        _router_kernel,
        grid=(t // tb,),
        in_specs=[blk],
        out_specs=[blk, blk, pl.BlockSpec((ROUTER_BLOCKS * SUBLANES, LANES), lambda i: (i, 0)),
                   pl.BlockSpec((1, LANES), lambda i: (0, 0))],
        out_shape=[jax.ShapeDtypeStruct((t, LANES), jnp.int32), jax.ShapeDtypeStruct((t, LANES), F32),
                   jax.ShapeDtypeStruct((t // ROW_TILE * SUBLANES, LANES), jnp.int32),
                   jax.ShapeDtypeStruct((1, LANES), jnp.int32)],
        scratch_shapes=[pltpu.VMEM((1, LANES), F32)],
        compiler_params=_cparams(("arbitrary",), 16),
        name="moe_router",
    )(logits)


def _moe(h2, logits, x, mods6, layer, first_row, rows_per_batch, g3, wg, wu, wd):
    t = x.shape[0]
    n_tok_blocks = t // ROW_TILE
    max_rows = t * TOP_K + n_tok_blocks * N_EXPERTS * (SUBLANES - 1)
    n_blocks = -(-max_rows // MOE_BLOCK) + N_EXPERTS
    lrow, gates, tab, tot = _router(logits)
    tab = tab.reshape(n_tok_blocks, SUBLANES, LANES)[:, :, :N_EXPERTS]
    ntile, loff, prior = tab[:, 0], tab[:, 1], tab[:, 2]
    nblk = (tot[0, :N_EXPERTS] + MOE_BLOCK - 1) // MOE_BLOCK
    blk_end = jnp.cumsum(nblk)
    pad_start = ((blk_end - nblk) * MOE_BLOCK).astype(jnp.int32)
    gdest = (pad_start[None, :] + prior).astype(jnp.int32)
    n_used = blk_end[-1:].astype(jnp.int32)
    blk = jnp.minimum(jnp.arange(n_blocks, dtype=jnp.int32), n_used[0] - 1)
    block_e = jnp.minimum(jnp.sum(blk[:, None] >= blk_end[None, :], axis=1), N_EXPERTS - 1).astype(jnp.int32)
    ntile, loff, gdest = ntile.reshape(-1), loff.reshape(-1), gdest.reshape(-1)
    xb = _dispatch(h2, lrow, pad_start, nblk.astype(jnp.int32), ntile, loff, gdest, n_blocks)
    yb = _experts(xb, block_e, n_used, wg, wu, wd)
    return _combine(yb, lrow, ntile, loff, gdest, gates, x, mods6, layer, first_row, rows_per_batch, g3)


def _pad_cols(w, n):
    return jnp.pad(w, ((0, 0), (0, n - w.shape[1])))


def kernel(x_prompt, x_sample, state_mlstm_C, state_mlstm_n, state_mlstm_m, state_ret_S, c, c_ctx, mod_w, mod_b, norm_g, mlstm_w_in, mlstm_gate_b, mlstm_conv_w, mlstm_conv_b, mlstm_head_g, mlstm_w_out, ret_w_in, ret_decay_logit, ret_head_g, ret_w_out, ffn_w_gate, ffn_w_up, ffn_w_down, moe_router, moe_w_gate, moe_w_up, moe_w_down):
    bp, n_p, d = x_prompt.shape
    bs, n_s, _ = x_sample.shape
    depth = mod_w.shape[0]
    assert depth == 2 and ROW_TILE % GRID_W == 0 and n_s % GRID_W == 0

    cond = jnp.zeros((MOD_ROWS, d), F32).at[0].set(c_ctx).at[1:1 + bs].set(c)
    mods6 = _modulation(cond, mod_w, mod_b).reshape(depth, MOD_ROWS, N_MOD, 1, d)

    groups = [dict(x=x_prompt.reshape(bp * n_p, d), first=0, rpb=None, nseq=bp, ntok=n_p, prompt=True),
              dict(x=x_sample.reshape(bs * n_s, d), first=1, rpb=n_s // ROW_TILE, nseq=bs, ntok=n_s, prompt=False)]

    j = 0
    ml_qk = (mlstm_w_in.shape[2] - 4 * HEADS) // 2
    ml_v = ml_qk // 2
    w_in = mlstm_w_in[j]
    wqk = w_in[:, :ml_qk].astype(BF16)
    wv = w_in[:, ml_qk:ml_qk + ml_v].astype(BF16)
    wo = w_in[:, ml_qk + ml_v:ml_qk + 2 * ml_v].astype(BF16)
    w_gate = w_in[:, ml_qk + 2 * ml_v:]
    wgh, wgl = _split_bf16(_pad_cols(w_gate, LANES))
    bcol = _pad_cols(mlstm_gate_b[j][None, :], LANES)
    g = norm_g[0]
    w_out0 = mlstm_w_out[j].astype(BF16)
    new_c = new_n = new_m = None
    casts = (ffn_w_gate[j], ffn_w_up[j], ffn_w_down[j], ret_w_in[j], ret_w_out[j])
    proj = {}
    for grp in reversed(groups):
        args = (mods6, 0, grp["first"], grp["rpb"])
        outs = _proj_mlstm(grp["x"], grp["ntok"], *args, g[0:1], wqk, wv, wo, wgh, wgl, bcol, w_gate.shape[1],
                           mlstm_conv_w[j], mlstm_conv_b[j][None, :], casts=() if grp["prompt"] else casts)
        proj[grp["prompt"]] = outs[:5]
        if not grp["prompt"]:
            fwg, fwu, fwd, rw_in, w_out1 = outs[5:]
    for grp in groups:
        args = (mods6, 0, grp["first"], grp["rpb"])
        q, kt, v, o, grow = proj[grp["prompt"]]
        grow3 = grow.reshape(grow.shape[0], grp["nseq"], grp["ntok"] // CHUNK, CHUNK)
        if grp["prompt"]:
            m0 = jnp.zeros((grp["nseq"] * 2 * HEADS,), F32)
            hs, new_c, new_n, new_m = _mlstm_scan(q, kt, v, grow3, m0, None, grp["nseq"], grp["ntok"], True)
        else:
            (hs,) = _mlstm_scan(q, kt, v, grow3, state_mlstm_m[:, j].reshape(-1),
                                (state_mlstm_C[:, j], state_mlstm_n[:, j]), grp["nseq"], grp["ntok"], False)
        (grp["x"],) = _mixer_out(hs, o, mlstm_head_g[j][None, :], w_out0, grp["x"], *args, g[1:2],
                                 ffn=(g[2:3], g[3:4], fwg, fwu, fwd))

    ret_qk = ret_w_in.shape[2] // 3
    g = norm_g[1]
    wrh, wrl = _split_bf16(_pad_cols(moe_router[j], LANES))
    decay_flat = ret_decay_logit[j].reshape(-1)
    new_s = None
    n_exp, _, d_ff = moe_w_gate.shape[1:]
    casts = (moe_w_gate[j].reshape(n_exp * d, d_ff), moe_w_up[j].reshape(n_exp * d, d_ff),
             moe_w_down[j].reshape(n_exp * d_ff, d))
    proj = {}
    for grp in reversed(groups):
        args = (mods6, 1, grp["first"], grp["rpb"])
        rope_tabs = None if grp["prompt"] else _rope_tables(grp["ntok"], ret_qk // (2 * HEADS))
        outs = _proj_ret(grp["x"], grp["ntok"], *args, g[0:1], rw_in, ret_qk // 2, ret_qk, rope_tabs,
                         casts=() if grp["prompt"] else casts)
        proj[grp["prompt"]] = outs[:4]
        if not grp["prompt"]:
            ewg, ewu, ewd = (w.reshape(n_exp, -1, w.shape[1]) for w in outs[4:])
    for grp in groups:
        args = (mods6, 1, grp["first"], grp["rpb"])
        q, kt, v, gate = proj[grp["prompt"]]
        if grp["prompt"]:
            hs, new_s = _ret_scan(q, kt, v, decay_flat, None, grp["nseq"], grp["ntok"], True)
        else:
            (hs,) = _ret_scan(q, kt, v, decay_flat, state_ret_S[:, j], grp["nseq"], grp["ntok"], False)
        x1, h2, logits = _mixer_out(hs, gate, ret_head_g[j][None, :], w_out1, grp["x"], *args, g[1:2],
                                    router=(g[2:3], wrh, wrl))
        grp["x"] = _moe(h2, logits, x1, *args, g[3:4], ewg, ewu, ewd)

    y_prompt = groups[0]["x"].reshape(bp, n_p, d)
    y_sample = groups[1]["x"].reshape(bs, n_s, d)
    return (y_prompt, y_sample, new_c[:, None], new_n[:, None], new_m[:, None, :, :, 0], new_s[:, None])
```

```python
import functools
import math

import jax
import jax.numpy as jnp
from jax import lax
from jax.experimental import pallas as pl
from jax.experimental.pallas import tpu as pltpu

F32 = jnp.float32
BF16 = jnp.bfloat16

EPS = 1e-6
N_MOD = 6
HEADS = 4
CHUNK = 128
RET_CHUNK = 256
GRID_W = 64
ROPE_BASE = 10000.0
N_EXPERTS = 8
TOP_K = 2
MOE_BLOCK = 256
LANES = 128
SUBLANES = 8
ROW_TILE = 256
MIXER_SUBTILES = 2
MOD_ROWS = 8
STAGE_ROWS = -(-(TOP_K * ROW_TILE + N_EXPERTS * (SUBLANES - 1)) // LANES) * LANES
MIB = 1024 * 1024


def _cparams(sem, vmem_mib):
    return pltpu.CompilerParams(dimension_semantics=sem, vmem_limit_bytes=vmem_mib * MIB)


def _dot(a, b):
    return jnp.dot(a, b, preferred_element_type=F32)


def _dot_nt(a, b):
    return lax.dot_general(a, b, (((1,), (1,)), ((), ())), preferred_element_type=F32)


def _dot_tn(a, b):
    return lax.dot_general(a, b, (((0,), (0,)), ((), ())), preferred_element_type=F32)


def _split_bf16(x):
    hi = x.astype(BF16)
    lo = (x - hi.astype(F32)).astype(BF16)
    return hi, lo


def _rms(x, g):
    return x * lax.rsqrt(jnp.mean(x * x, -1, keepdims=True) + EPS) * g


def _adaln(x, g, shift, scale):
    return _rms(x, g) * (1.0 + scale) + shift


def _silu(x):
    return x * jax.nn.sigmoid(x)


def _logsig(x):
    return jnp.minimum(x, 0.0) - jnp.log1p(jnp.exp(-jnp.abs(x)))


def _const_spec(shape):
    nd = len(shape)
    return pl.BlockSpec(shape, lambda *_: (0,) * nd, pipeline_mode=pl.Buffered(1))


def _mod_spec(d, layer, comp, rows_per_batch, first_row):
    if rows_per_batch is None:
        return pl.BlockSpec((None, None, None, 1, d), lambda i: (layer, first_row, comp, 0, 0))
    return pl.BlockSpec((None, None, None, 1, d), lambda i: (layer, first_row + i // rows_per_batch, comp, 0, 0))


def _mod_kernel(c_ref, w_ref, b_ref, o_ref):
    s = _silu(c_ref[...]).astype(BF16)
    o_ref[...] = _dot(s, w_ref[...].astype(BF16)) + b_ref[...]


def _modulation(cond, mod_w, mod_b):
    depth, d, n = mod_w.shape
    tn = n // 4
    return pl.pallas_call(
        _mod_kernel,
        grid=(depth, n // tn),
        in_specs=[pl.BlockSpec((MOD_ROWS, d), lambda l, j: (0, 0)),
                  pl.BlockSpec((None, d, tn), lambda l, j: (l, 0, j)),
                  pl.BlockSpec((None, 1, tn), lambda l, j: (l, 0, j))],
        out_specs=pl.BlockSpec((None, MOD_ROWS, tn), lambda l, j: (l, 0, j)),
        out_shape=jax.ShapeDtypeStruct((depth, MOD_ROWS, n), F32),
        compiler_params=_cparams(("arbitrary", "arbitrary"), 40),
        name="modulation",
    )(cond, mod_w, mod_b.reshape(depth, 1, n))


BF16_ROWS = 16


def _cast_plan(arrays, n_steps):
    in_specs, out_specs, out_shape, nblks = [], [], [], []
    for a in arrays:
        rows, cols = a.shape
        nblk = max(n for n in range(1, n_steps + 1) if rows % n == 0 and (rows // n) % BF16_ROWS == 0)
        spec = pl.BlockSpec((rows // nblk, cols), lambda i, nb=nblk: (jnp.minimum(i, nb - 1), 0))
        in_specs.append(spec)
        out_specs.append(spec)
        out_shape.append(jax.ShapeDtypeStruct(a.shape, BF16))
        nblks.append(nblk)
    return in_specs, out_specs, out_shape, nblks


def _cast_slabs(srcs, dsts, nblks):
    for src, dst, nb in zip(srcs, dsts, nblks):
        @pl.when(pl.program_id(0) < nb)
        def _():
            dst[...] = src[...].astype(BF16)


def _proj_mlstm_kernel(tiles_per_seq, cast_nblks, *refs):
    nc = len(cast_nblks)
    (x_ref, xp_ref, xn_ref, g_ref, sh_ref, sc_ref, wqk_ref, wv_ref, wo_ref, wgh_ref, wgl_ref, bcol_ref, cw_ref,
     cb_ref) = refs[:14]
    q_ref, kt_ref, v_ref, o_ref, grow_ref = refs[14 + nc:19 + nc]
    _cast_slabs(refs[14:14 + nc], refs[19 + nc:], cast_nblks)
    tm = x_ref.shape[0]
    nq = q_ref.shape[1]
    h = _adaln(x_ref[...], g_ref[...], sh_ref[...], sc_ref[...])
    hb, hl = _split_bf16(h)
    x_halo = jnp.concatenate([xp_ref[...], xn_ref[...]], axis=0)
    hb_halo = _adaln(x_halo, g_ref[...], sh_ref[...], sc_ref[...]).astype(BF16)
    hb_all = jnp.concatenate([hb, hb_halo], axis=0)
    pos = pl.program_id(0) % tiles_per_seq
    has_prev = jnp.where(pos > 0, 1.0, 0.0)
    has_next = jnp.where(pos < tiles_per_seq - 1, 1.0, 0.0)
    wc = nq // HEADS
    rowi = lax.broadcasted_iota(jnp.int32, (tm, wc), 0)
    for c in range(2 * HEADS):
        sl = slice(c * wc, (c + 1) * wc)
        p_all = _dot(hb_all, wqk_ref[:, sl])
        p = p_all[:tm]
        p_prev = p_all[tm + SUBLANES - 1:tm + SUBLANES, :] * has_prev
        p_next = p_all[tm + SUBLANES:tm + SUBLANES + 1, :] * has_next
        prev = jnp.where(rowi == 0, p_prev, pltpu.roll(p, 1, axis=0))
        nxt = jnp.where(rowi == tm - 1, p_next, pltpu.roll(p, tm - 1, axis=0))
        y = prev * cw_ref[0:1, sl] + p * cw_ref[1:2, sl] + nxt * cw_ref[2:3, sl] + cb_ref[:, sl]
        if c < HEADS:
            q_ref[:, sl] = (y * wc ** -0.5).astype(BF16)
        else:
            kt_ref[(c - HEADS) * wc:(c - HEADS + 1) * wc, :] = y.T.astype(BF16)
    v_ref[...] = _dot(hb, wv_ref[...]).astype(BF16)
    o_ref[...] = jax.nn.sigmoid(_dot(hb, wo_ref[...]))
    gc = _dot(hb, wgh_ref[...]) + _dot(hl, wgh_ref[...]) + _dot(hb, wgl_ref[...]) + bcol_ref[...]
    grow_ref[...] = gc.T[:grow_ref.shape[0], :]


def _proj_mlstm(x, n_tok, mods6, layer, first_row, rows_per_batch, g, wqk, wv, wo, wgh, wgl, bcol, ng, conv_w, conv_b,
                casts=()):
    t, d = x.shape
    tm = ROW_TILE
    ms = lambda comp: _mod_spec(d, layer, comp, rows_per_batch, first_row)
    nq = wqk.shape[1] // 2
    tps = tm // SUBLANES
    last = t // SUBLANES - 1
    c_in, c_out, c_shape, c_nblks = _cast_plan(casts, t // tm)
    return pl.pallas_call(
        functools.partial(_proj_mlstm_kernel, n_tok // tm, tuple(c_nblks)),
        grid=(t // tm,),
        in_specs=[pl.BlockSpec((tm, d), lambda i: (i, 0)),
                  pl.BlockSpec((SUBLANES, d), lambda i: (jnp.maximum(i * tps - 1, 0), 0)),
                  pl.BlockSpec((SUBLANES, d), lambda i: (jnp.minimum((i + 1) * tps, last), 0)),
                  _const_spec((1, d)), ms(0), ms(1),
                  _const_spec(wqk.shape), _const_spec(wv.shape), _const_spec(wo.shape),
                  _const_spec(wgh.shape), _const_spec(wgl.shape), _const_spec(bcol.shape),
                  _const_spec(conv_w.shape), _const_spec(conv_b.shape)] + c_in,
        out_specs=[pl.BlockSpec((tm, nq), lambda i: (i, 0)),
                   pl.BlockSpec((nq, tm), lambda i: (0, i)),
                   pl.BlockSpec((tm, wv.shape[1]), lambda i: (i, 0)),
                   pl.BlockSpec((tm, wo.shape[1]), lambda i: (i, 0)),
                   pl.BlockSpec((ng, tm), lambda i: (0, i))] + c_out,
        out_shape=[jax.ShapeDtypeStruct((t, nq), BF16),
                   jax.ShapeDtypeStruct((nq, t), BF16),
                   jax.ShapeDtypeStruct((t, wv.shape[1]), BF16),
                   jax.ShapeDtypeStruct((t, wo.shape[1]), F32),
                   jax.ShapeDtypeStruct((ng, t), F32)] + c_shape,
        compiler_params=_cparams(("arbitrary",), 56),
        name="proj_mlstm",
    )(x, x, x, g, mods6, mods6, wqk, wv, wo, wgh, wgl, bcol, conv_w, conv_b, *casts)


def _make_mlstm_scan_kernel(n_tok, dk, dv, has_state, emit_state):
    L = CHUNK
    nc = n_tok // L
    assert nc % 2 == 0 and L == LANES

    def kernel(*refs):
        it = iter(refs)
        m0_ref = next(it)
        q_ref, kt_ref, v_ref = next(it), next(it), next(it)
        g_ref = next(it)
        if has_state:
            c0_ref, n0_ref = next(it), next(it)
        out_ref = next(it)
        if emit_state:
            cout_ref, nout_ref, mout_ref = next(it), next(it), next(it)
        bsc, rsc, rmsc, cst, cbf, nrep, nbf = (next(it) for _ in range(7))

        b = pl.program_id(0)
        hh = pl.program_id(1)

        ri = lax.broadcasted_iota(jnp.int32, (L, L), 0)
        ci = lax.broadcasted_iota(jnp.int32, (L, L), 1)
        lower = ri >= ci
        upper = ri <= ci
        masks = ((lower, upper), (upper, lower))

        lane = lax.broadcasted_iota(jnp.int32, (nc, L), 1)

        def lane_scan(x, op, fill, reverse):
            s = 1
            while s < L:
                if reverse:
                    x = op(x, jnp.where(lane < L - s, pltpu.roll(x, L - s, axis=1), fill))
                else:
                    x = op(x, jnp.where(lane >= s, pltpu.roll(x, s, axis=1), fill))
                s *= 2
            return x

        @pl.when(hh == 0)
        def _():
            for d in range(2):
                for h in range(HEADS):
                    row_i, row_f = d * 2 * HEADS + h, d * 2 * HEADS + HEADS + h
                    b_all = lane_scan(_logsig(g_ref[row_f]), jnp.add, 0.0, reverse=(d == 1))
                    r_all = g_ref[row_i] - b_all
                    bsc[d, h] = b_all
                    rsc[d, h] = r_all
                    rmsc[d, h] = lane_scan(r_all, jnp.maximum, -jnp.inf, reverse=(d == 1))

        if has_state:
            cst[...] = c0_ref[...]
            for d in range(2):
                nrep[d] = jnp.broadcast_to(n0_ref[d], (dk, LANES))
        else:
            cst[...] = jnp.zeros_like(cst)
            nrep[...] = jnp.zeros_like(nrep)
        cbf[...] = cst[...].astype(BF16)
        nbf[...] = nrep[...].astype(BF16)
        m_init = tuple(jnp.full((1, LANES), m0_ref[(b * 2 + d) * HEADS + hh], F32) for d in range(2))

        def lanes(x, n):
            return jnp.concatenate([x] * (n // LANES), axis=1)

        def chunk(d, c, m):
            mk = masks[d][0]
            r0 = pl.multiple_of(c * L, L)
            q = q_ref[pl.ds(r0, L), :]
            kt = kt_ref[:, pl.ds(r0, L)]
            v = v_ref[pl.ds(r0, L), :]
            rr = rsc[d, hh, pl.ds(c, 1), :]
            b_rep = jnp.broadcast_to(bsc[d, hh, pl.ds(c, 1), :], (L, L)).T
            rm_rep = jnp.broadcast_to(rmsc[d, hh, pl.ds(c, 1), :], (L, L)).T
            end = 0 if d == 1 else L - 1
            b_end, rm_end = b_rep[end:end + 1, :], rm_rep[end:end + 1, :]
            mm = jnp.maximum(m, rm_rep)
            qkn = _dot(q, jnp.concatenate([kt, nbf[d]], axis=1))
            s = qkn[:, :L] * jnp.where(mk, jnp.exp(rr - mm), 0.0)
            w_inter = jnp.exp(m - mm)
            num = lanes(w_inter, dv) * _dot(q, cbf[d]) + _dot(s.astype(BF16), v)
            den = w_inter * qkn[:, L:] + jnp.sum(s, axis=1, keepdims=True)
            inv = 1.0 / jnp.maximum(jnp.abs(den), jnp.exp(-(b_rep + mm)))
            h = num * lanes(inv, dv)
            m_end = jnp.maximum(m, rm_end)
            decay = jnp.exp(m - m_end)
            kw = kt.astype(F32) * jnp.exp(rr - m_end)
            c_new = lanes(decay, dv) * cst[d] + _dot(kw.astype(BF16), v)
            n_new = decay * nrep[d] + jnp.sum(kw, axis=1, keepdims=True)
            cst[d] = c_new
            cbf[d] = c_new.astype(BF16)
            nrep[d] = n_new
            nbf[d] = n_new.astype(BF16)
            return h, b_end + m_end

        def make_body(accumulate):
            def body(i, carry):
                mf, mb = carry
                hf, mf = chunk(0, i, mf)
                rf = pl.multiple_of(i * L, L)
                cb = nc - 1 - i
                hb, mb = chunk(1, cb, mb)
                rb = pl.multiple_of(cb * L, L)
                if accumulate:
                    out_ref[pl.ds(rf, L), :] += hf
                    out_ref[pl.ds(rb, L), :] += hb
                else:
                    out_ref[pl.ds(rf, L), :] = hf
                    out_ref[pl.ds(rb, L), :] = hb
                return mf, mb
            return body

        unroll = 2 if nc % 4 == 0 else 1
        carry = lax.fori_loop(0, nc // 2, make_body(False), m_init, unroll=unroll)
        carry = lax.fori_loop(nc // 2, nc, make_body(True), carry, unroll=unroll)

        if emit_state:
            cout_ref[...] = cst[...]
            for d in range(2):
                nout_ref[d, pl.ds(hh, 1), :] = nrep[d].T[0:1, :]
                mout_ref[d, pl.ds(hh, 1), :] = carry[d]

    return kernel


def _mlstm_scan(q, kt, v, gates, m0_flat, state, n_seq, n_tok, emit_state):
    dk = q.shape[1] // HEADS
    dv = v.shape[1] // HEADS
    nc = n_tok // CHUNK
    has_state = state is not None
    kern = _make_mlstm_scan_kernel(n_tok, dk, dv, has_state, emit_state)
    in_specs = [pl.BlockSpec((n_tok, dk), lambda b, h, m: (b, h)),
                pl.BlockSpec((dk, n_tok), lambda b, h, m: (h, b)),
                pl.BlockSpec((n_tok, dv), lambda b, h, m: (b, h)),
                pl.BlockSpec((gates.shape[0], None, nc, CHUNK), lambda b, h, m: (0, b, 0, 0))]
    args = [q, kt, v, gates]
    if has_state:
        in_specs += [pl.BlockSpec((None, 2, None, dk, dv), lambda b, h, m: (b, 0, h, 0, 0)),
                     pl.BlockSpec((None, 2, None, dk, 1), lambda b, h, m: (b, 0, h, 0, 0))]
        args += [state[0], state[1][..., None]]
    out_specs = [pl.BlockSpec((n_tok, dv), lambda b, h, m: (b, h))]
    out_shape = [jax.ShapeDtypeStruct((n_seq * n_tok, HEADS * dv), F32)]
    if emit_state:
        out_specs += [pl.BlockSpec((None, 2, None, dk, dv), lambda b, h, m: (b, 0, h, 0, 0)),
                      pl.BlockSpec((None, 2, HEADS, dk), lambda b, h, m: (b, 0, 0, 0)),
                      pl.BlockSpec((None, 2, HEADS, LANES), lambda b, h, m: (b, 0, 0, 0))]
        out_shape += [jax.ShapeDtypeStruct((n_seq, 2, HEADS, dk, dv), F32),
                      jax.ShapeDtypeStruct((n_seq, 2, HEADS, dk), F32),
                      jax.ShapeDtypeStruct((n_seq, 2, HEADS, LANES), F32)]
    grid_spec = pltpu.PrefetchScalarGridSpec(
        num_scalar_prefetch=1, grid=(n_seq, HEADS), in_specs=in_specs, out_specs=out_specs,
        scratch_shapes=[pltpu.VMEM((2, HEADS, nc, CHUNK), F32),
                        pltpu.VMEM((2, HEADS, nc, CHUNK), F32),
                        pltpu.VMEM((2, HEADS, nc, CHUNK), F32),
                        pltpu.VMEM((2, dk, dv), F32), pltpu.VMEM((2, dk, dv), BF16),
                        pltpu.VMEM((2, dk, LANES), F32), pltpu.VMEM((2, dk, LANES), BF16)])
    return pl.pallas_call(
        kern, grid_spec=grid_spec, out_shape=out_shape,
        compiler_params=_cparams(("arbitrary", "arbitrary"), 56),
        name="mlstm_scan",
    )(m0_flat, *args)


def _proj_ret_kernel(rope, cast_nblks, *refs):
    nc = len(cast_nblks)
    n_in = 5 + (4 if rope else 0)
    x_ref, g_ref, sh_ref, sc_ref, w_ref = refs[:5]
    if rope:
        rcos_ref, rsin_ref, ccos_ref, csin_ref = refs[5:9]
    q_ref, kt_ref, v_ref, gate_ref = refs[n_in + nc:n_in + nc + 4]
    _cast_slabs(refs[n_in:n_in + nc], refs[n_in + nc + 4:], cast_nblks)
    nq, nv = q_ref.shape[1], v_ref.shape[1]
    wc = nq // HEADS
    hb = _adaln(x_ref[...], g_ref[...], sh_ref[...], sc_ref[...]).astype(BF16)
    if rope:
        cos = jnp.concatenate([rcos_ref[...], ccos_ref[...]], axis=1)
        sin = jnp.concatenate([rsin_ref[...], csin_ref[...]], axis=1)
    for c in range(2 * HEADS):
        p = _dot(hb, w_ref[:, c * wc:(c + 1) * wc])
        if rope:
            swapped = jnp.concatenate([pltpu.roll(p[:, j * LANES:(j + 1) * LANES], LANES // 2, axis=1)
                                       for j in range(wc // LANES)], axis=1)
            p = p * cos + swapped * sin
        if c < HEADS:
            q_ref[:, c * wc:(c + 1) * wc] = p.astype(BF16)
        else:
            kt_ref[(c - HEADS) * wc:(c - HEADS + 1) * wc, :] = (p * wc ** -0.5).T.astype(BF16)
    v_ref[...] = _dot(hb, w_ref[:, 2 * nq:2 * nq + nv]).astype(BF16)
    gate_ref[...] = _silu(_dot(hb, w_ref[:, 2 * nq + nv:])).astype(BF16)


def _proj_ret(x, n_tok, mods6, layer, first_row, rows_per_batch, g, w_in, nq, nv, rope_tabs, casts=()):
    t, d = x.shape
    tm = ROW_TILE
    ms = lambda comp: _mod_spec(d, layer, comp, rows_per_batch, first_row)
    ng = w_in.shape[1] - 2 * nq - nv
    rope = rope_tabs is not None
    in_specs = [pl.BlockSpec((tm, d), lambda i: (i, 0)), _const_spec((1, d)), ms(0), ms(1), _const_spec(w_in.shape)]
    args = [x, g, mods6, mods6, w_in]
    if rope:
        tiles_per_seq = n_tok // tm
        row_spec = pl.BlockSpec((tm, LANES), lambda i: (i % tiles_per_seq, 0))
        in_specs += [row_spec, row_spec, _const_spec((tm, LANES)), _const_spec((tm, LANES))]
        args += list(rope_tabs)
    c_in, c_out, c_shape, c_nblks = _cast_plan(casts, t // tm)
    return pl.pallas_call(
        functools.partial(_proj_ret_kernel, rope, tuple(c_nblks)),
        grid=(t // tm,),
        in_specs=in_specs + c_in,
        out_specs=[pl.BlockSpec((tm, nq), lambda i: (i, 0)),
                   pl.BlockSpec((nq, tm), lambda i: (0, i)),
                   pl.BlockSpec((tm, nv), lambda i: (i, 0)),
                   pl.BlockSpec((tm, ng), lambda i: (i, 0))] + c_out,
        out_shape=[jax.ShapeDtypeStruct((t, nq), BF16),
                   jax.ShapeDtypeStruct((nq, t), BF16),
                   jax.ShapeDtypeStruct((t, nv), BF16),
                   jax.ShapeDtypeStruct((t, ng), BF16)] + c_shape,
        compiler_params=_cparams(("arbitrary",), 56),
        name="proj_ret",
    )(*args, *casts)


def _make_ret_scan_kernel(n_tok, dk, dv, has_state, emit_state):
    L = RET_CHUNK
    nc = n_tok // L
    assert nc == 1 or nc % 2 == 0

    def kernel(*refs):
        it = iter(refs)
        dl_ref = next(it)
        q_ref, kt_ref, v_ref = next(it), next(it), next(it)
        if has_state:
            s0_ref = next(it)
        out_ref = next(it)
        if emit_state:
            sout_ref = next(it)
        sst, sbf = next(it), next(it)
        if nc > 1:
            acc = next(it)

        hh = pl.program_id(1)
        ri = lax.broadcasted_iota(jnp.int32, (L, L), 0)
        ci = lax.broadcasted_iota(jnp.int32, (L, L), 1)
        rel = (ri - ci).astype(F32)
        pos_col = lax.broadcasted_iota(jnp.int32, (L, LANES), 0).astype(F32)
        pos_row = lax.broadcasted_iota(jnp.int32, (1, L), 1).astype(F32)

        dmat, q_dec, k_dec, c_dec = [], [], [], []
        for d in range(2):
            lg = _logsig(jnp.full((1, 1), dl_ref[d * HEADS + hh], F32))
            if d == 0:
                dmat.append(jnp.where(ri >= ci, jnp.exp(lg * jnp.maximum(rel, 0.0)), 0.0))
                q_dec.append(jnp.exp(lg * (pos_col + 1.0)))
                k_dec.append(jnp.exp(lg * (L - 1.0 - pos_row)))
            else:
                dmat.append(jnp.where(ri <= ci, jnp.exp(lg * jnp.maximum(-rel, 0.0)), 0.0))
                q_dec.append(jnp.exp(lg * (L - pos_col)))
                k_dec.append(jnp.exp(lg * pos_row))
            c_dec.append(jnp.broadcast_to(jnp.exp(lg * float(L)), (1, LANES)))

        if has_state:
            sst[...] = s0_ref[...]
        else:
            sst[...] = jnp.zeros_like(sst)
        sbf[...] = sst[...].astype(BF16)

        def lanes(x, n):
            return jnp.concatenate([x] * (n // LANES), axis=1)

        def chunk(d, c):
            r0 = c * L if isinstance(c, int) else pl.multiple_of(c * L, L)
            q = q_ref[pl.ds(r0, L), :]
            kt = kt_ref[:, pl.ds(r0, L)]
            v = v_ref[pl.ds(r0, L), :]
            s = _dot(q, kt) * dmat[d]
            out = _dot(s.astype(BF16), v) + lanes(q_dec[d], dv) * _dot(q, sbf[d])
            s_new = lanes(c_dec[d], dv) * sst[d] + _dot((kt.astype(F32) * k_dec[d]).astype(BF16), v)
            sst[d] = s_new
            sbf[d] = s_new.astype(BF16)
            return out

        def body1(i, _):
            rf = pl.multiple_of(i * L, L)
            acc[pl.ds(rf, L), :] = chunk(0, i)
            cb = nc - 1 - i
            rb = pl.multiple_of(cb * L, L)
            acc[pl.ds(rb, L), :] = chunk(1, cb)
            return 0

        def body2(i, _):
            rf = pl.multiple_of(i * L, L)
            out_ref[pl.ds(rf, L), :] = (acc[pl.ds(rf, L), :] + chunk(0, i)).astype(BF16)
            cb = nc - 1 - i
            rb = pl.multiple_of(cb * L, L)
            out_ref[pl.ds(rb, L), :] = (acc[pl.ds(rb, L), :] + chunk(1, cb)).astype(BF16)
            return 0

        if nc == 1:
            out_ref[...] = (chunk(0, 0) + chunk(1, 0)).astype(BF16)
        else:
            unroll = 2 if nc % 4 == 0 else 1
            lax.fori_loop(0, nc // 2, body1, 0, unroll=unroll)
            lax.fori_loop(nc // 2, nc, body2, 0, unroll=unroll)
        if emit_state:
            sout_ref[...] = sst[...]

    return kernel


def _ret_scan(q, kt, v, decay_flat, state, n_seq, n_tok, emit_state):
    dk = q.shape[1] // HEADS
    dv = v.shape[1] // HEADS
    has_state = state is not None
    kern = _make_ret_scan_kernel(n_tok, dk, dv, has_state, emit_state)
    in_specs = [pl.BlockSpec((n_tok, dk), lambda b, h, m: (b, h)),
                pl.BlockSpec((dk, n_tok), lambda b, h, m: (h, b)),
                pl.BlockSpec((n_tok, dv), lambda b, h, m: (b, h))]
    args = [q, kt, v]
    if has_state:
        in_specs.append(pl.BlockSpec((None, 2, None, dk, dv), lambda b, h, m: (b, 0, h, 0, 0)))
        args.append(state)
    out_specs = [pl.BlockSpec((n_tok, dv), lambda b, h, m: (b, h))]
    out_shape = [jax.ShapeDtypeStruct((n_seq * n_tok, HEADS * dv), BF16)]
    if emit_state:
        out_specs.append(pl.BlockSpec((None, 2, None, dk, dv), lambda b, h, m: (b, 0, h, 0, 0)))
        out_shape.append(jax.ShapeDtypeStruct((n_seq, 2, HEADS, dk, dv), F32))
    grid_spec = pltpu.PrefetchScalarGridSpec(
        num_scalar_prefetch=1, grid=(n_seq, HEADS), in_specs=in_specs, out_specs=out_specs,
        scratch_shapes=[pltpu.VMEM((2, dk, dv), F32), pltpu.VMEM((2, dk, dv), BF16)]
        + ([pltpu.VMEM((n_tok, dv), F32)] if n_tok > RET_CHUNK else []))
    return pl.pallas_call(
        kern, grid_spec=grid_spec, out_shape=out_shape,
        compiler_params=_cparams(("arbitrary", "arbitrary"), 56),
        name="ret_scan",
    )(decay_flat, *args)


def _rope_tables(n_tok, dk):
    r = dk // 4
    inv = 1.0 / (ROPE_BASE ** (jnp.arange(r, dtype=F32) / r))
    sign = jnp.concatenate([-jnp.ones((r,), F32), jnp.ones((r,), F32)])
    rows = (jnp.arange(n_tok) // GRID_W).astype(F32)[:, None] * inv
    cols = (jnp.arange(ROW_TILE) % GRID_W).astype(F32)[:, None] * inv
    two = lambda a: jnp.concatenate([a, a], axis=-1)
    return (two(jnp.cos(rows)), two(jnp.sin(rows)) * sign, two(jnp.cos(cols)), two(jnp.sin(cols)) * sign)


def _make_mixer_out_kernel(dv, with_router, with_ffn):
    def kernel(*refs):
        it = iter(refs)
        hs_ref, gate_ref, hg_ref, w_ref, x_ref, g1_ref, ga_ref = (next(it) for _ in range(7))
        if with_router:
            g2_ref, fsh_ref, fsc_ref, wrh_ref, wrl_ref = (next(it) for _ in range(5))
        if with_ffn:
            g2_ref, fsh_ref, fsc_ref, wg_ref, wu_ref, wd_ref, g3_ref, fga_ref = (next(it) for _ in range(8))
        xo_ref = next(it)
        if with_router:
            h2_ref, lg_ref = next(it), next(it)
        z_ref = next(it)
        subs = [slice(r * ROW_TILE, (r + 1) * ROW_TILE) for r in range(x_ref.shape[0] // ROW_TILE)]
        for rs in subs:
            for hh in range(HEADS):
                sl = slice(hh * dv, (hh + 1) * dv)
                seg = hs_ref[rs, sl].astype(F32)
                y = seg * lax.rsqrt(jnp.mean(seg * seg, -1, keepdims=True) + EPS) * hg_ref[:, sl]
                z_ref[rs, sl] = (gate_ref[rs, sl].astype(F32) * y).astype(BF16)
        ys = [_dot(z_ref[rs, :], w_ref[...]) for rs in subs]
        xns = [x_ref[rs, :] + ga_ref[...] * _rms(y, g1_ref[...]) for rs, y in zip(subs, ys)]
        if with_ffn:
            hbs = [_adaln(xn, g2_ref[...], fsh_ref[...], fsc_ref[...]).astype(BF16) for xn in xns]
            acts = [(_silu(_dot(hb, wg_ref[...])) * _dot(hb, wu_ref[...])).astype(BF16) for hb in hbs]
            downs = [_dot(a, wd_ref[...]) for a in acts]
            xns = [xn + fga_ref[...] * _rms(dn, g3_ref[...]) for xn, dn in zip(xns, downs)]
        for rs, xn in zip(subs, xns):
            xo_ref[rs, :] = xn
        if with_router:
            for rs, xn in zip(subs, xns):
                h2 = _adaln(xn, g2_ref[...], fsh_ref[...], fsc_ref[...])
                h2_ref[rs, :] = h2
                hb, hl = _split_bf16(h2)
                lg_ref[rs, :] = _dot(hb, wrh_ref[...]) + _dot(hl, wrh_ref[...]) + _dot(hb, wrl_ref[...])
    return kernel


def _mixer_out(hs, gate, head_g, w_out, x, mods6, layer, first_row, rows_per_batch, g1, router=None, ffn=None):
    t, d = x.shape
    vdim = hs.shape[1]
    tm = MIXER_SUBTILES * ROW_TILE
    tiles_per_batch = None if rows_per_batch is None else rows_per_batch // MIXER_SUBTILES
    ms = lambda comp: _mod_spec(d, layer, comp, tiles_per_batch, first_row)
    with_router = router is not None
    with_ffn = ffn is not None
    in_specs = [pl.BlockSpec((tm, vdim), lambda i: (i, 0)),
                pl.BlockSpec((tm, vdim), lambda i: (i, 0)),
                _const_spec((1, vdim)), _const_spec(w_out.shape),
                pl.BlockSpec((tm, d), lambda i: (i, 0)), _const_spec((1, d)), ms(2)]
    args = [hs, gate, head_g, w_out, x, g1, mods6]
    out_specs = [pl.BlockSpec((tm, d), lambda i: (i, 0))]
    out_shape = [jax.ShapeDtypeStruct((t, d), F32)]
    if with_router:
        g2, wrh, wrl = router
        in_specs += [_const_spec((1, d)), ms(3), ms(4), _const_spec(wrh.shape), _const_spec(wrl.shape)]
        args += [g2, mods6, mods6, wrh, wrl]
        out_specs += [pl.BlockSpec((tm, d), lambda i: (i, 0)), pl.BlockSpec((tm, LANES), lambda i: (i, 0))]
        out_shape += [jax.ShapeDtypeStruct((t, d), F32), jax.ShapeDtypeStruct((t, LANES), F32)]
    if with_ffn:
        g2, g3, wg, wu, wd = ffn
        in_specs += [_const_spec((1, d)), ms(3), ms(4), _const_spec(wg.shape), _const_spec(wu.shape),
                     _const_spec(wd.shape), _const_spec((1, d)), ms(5)]
        args += [g2, mods6, mods6, wg, wu, wd, g3, mods6]
    return pl.pallas_call(
        _make_mixer_out_kernel(vdim // HEADS, with_router, with_ffn),
        grid=(t // tm,), in_specs=in_specs, out_specs=out_specs, out_shape=out_shape,
        scratch_shapes=[pltpu.VMEM((tm, vdim), BF16)],
        compiler_params=_cparams(("arbitrary",), 58),
        name="mixer_ffn" if with_ffn else "mixer_out",
    )(*args)


def _router_kernel(lg_ref, lrow_ref, gt_ref, tab_ref, tot_ref, carry):
    tb = ROW_TILE
    n_sub = lg_ref.shape[0] // tb

    @pl.when(pl.program_id(0) == 0)
    def _():
        carry[...] = jnp.zeros_like(carry)

    lane = lax.broadcasted_iota(jnp.int32, (tb, LANES), 1)
    ri = lax.broadcasted_iota(jnp.int32, (tb, tb), 0)
    ci = lax.broadcasted_iota(jnp.int32, (tb, tb), 1)
    before = jnp.where(ri > ci, 1.0, 0.0).astype(BF16)
    ei = lax.broadcasted_iota(jnp.int32, (LANES, LANES), 0)
    ej = lax.broadcasted_iota(jnp.int32, (LANES, LANES), 1)
    earlier = jnp.where(ei < ej, 1.0, 0.0).astype(BF16)
    sub = lax.broadcasted_iota(jnp.int32, (SUBLANES, LANES), 0)
    rmax = lambda x: jnp.max(x, axis=1, keepdims=True)
    rmin = lambda x: jnp.min(x, axis=1, keepdims=True)
    rsum = lambda x: jnp.sum(x, axis=1, keepdims=True)

    lgs = [jnp.where(lane < N_EXPERTS, lg_ref[k * tb:(k + 1) * tb, :], -jnp.inf) for k in range(n_sub)]
    v1s = [rmax(lg) for lg in lgs]
    i1s = [rmin(jnp.where(lg == v1, lane, LANES)) for lg, v1 in zip(lgs, v1s)]
    lg2s = [jnp.where(lane == i1, -jnp.inf, lg) for lg, i1 in zip(lgs, i1s)]
    v2s = [rmax(lg2) for lg2 in lg2s]
    i2s = [rmin(jnp.where(lg2 == v2, lane, LANES)) for lg2, v2 in zip(lg2s, v2s)]
    oh1s = [lane == i1 for i1 in i1s]
    oh2s = [lane == i2 for i2 in i2s]
    onehots = [jnp.where(a | b, 1.0, 0.0) for a, b in zip(oh1s, oh2s)]
    ranks = [_dot(before, oh.astype(BF16)) for oh in onehots]
    tiless = [jnp.ceil(jnp.sum(oh, axis=0, keepdims=True) * (1.0 / SUBLANES)) for oh in onehots]
    offs = [_dot(jnp.broadcast_to(tl, (SUBLANES, LANES)).astype(BF16), earlier)[0:1] * SUBLANES for tl in tiless]
    r1s = [rsum(jnp.where(oh, rk + off, 0.0)) for oh, rk, off in zip(oh1s, ranks, offs)]
    r2s = [rsum(jnp.where(oh, rk + off, 0.0)) for oh, rk, off in zip(oh2s, ranks, offs)]
    prior = carry[...]
    for k in range(n_sub):
        rs = slice(k * tb, (k + 1) * tb)
        ex = jnp.exp(v2s[k] - v1s[k])
        den = 1.0 + ex
        lrow_ref[rs, :] = jnp.where(lane == 0, r1s[k], jnp.where(lane == 1, r2s[k], 0.0)).astype(jnp.int32)
        gt_ref[rs, :] = jnp.where(lane == 0, 1.0 / den, jnp.where(lane == 1, ex / den, 0.0))
        tab = jnp.where(sub == 0, tiless[k], jnp.where(sub == 1, offs[k], jnp.where(sub == 2, prior, 0.0)))
        tab_ref[k * SUBLANES:(k + 1) * SUBLANES, :] = tab.astype(jnp.int32)
        prior = prior + tiless[k] * SUBLANES
    carry[...] = prior
    tot_ref[...] = prior.astype(jnp.int32)


RUN_TILES = 8
MOE_STEP_BLOCKS = 4


def _tile_copies(nt_ref, lo_ref, gd_ref, blk, make_copy, wait):
    def go(lo, gd, rows):
        cp = make_copy(pl.multiple_of(lo, SUBLANES), pl.multiple_of(gd, SUBLANES), rows)
        if wait:
            cp.wait()
        else:
            cp.start()

    for e in range(N_EXPERTS):
        idx = blk * N_EXPERTS + e
        nt, lo, gd = nt_ref[idx], lo_ref[idx], gd_ref[idx]

        def run(j, _):
            off = j * (RUN_TILES * SUBLANES)
            go(lo + off, gd + off, RUN_TILES * SUBLANES)
            return 0

        lax.fori_loop(0, nt // RUN_TILES, run, 0)
        k = RUN_TILES // 2
        while k >= 1:
            @pl.when(nt % (2 * k) >= k)
            def _():
                off = (nt // (2 * k)) * (2 * k) * SUBLANES
                go(lo + off, gd + off, k * SUBLANES)
            k //= 2


def _make_dispatch_kernel(tb, n_blocks):
    def kernel(ps_ref, nb_ref, nt_ref, lo_ref, gd_ref, lrow_ref, h_ref, xb_ref, xs, zbuf, sem):
        i = pl.program_id(0)

        @pl.when(i == 0)
        def _():
            zbuf[...] = jnp.zeros_like(zbuf)

            def zero_block(row0):
                dst = pl.multiple_of(row0, MOE_BLOCK)
                cp = pltpu.make_async_copy(zbuf, xb_ref.at[pl.ds(dst, MOE_BLOCK), :], sem.at[2])
                cp.start()
                cp.wait()

            for e in range(N_EXPERTS):
                @pl.when(nb_ref[e] > 0)
                def _():
                    zero_block(ps_ref[e] + (nb_ref[e] - 1) * MOE_BLOCK)

            used = ps_ref[N_EXPERTS - 1] // MOE_BLOCK + nb_ref[N_EXPERTS - 1]

            def tail(j, _):
                zero_block(j * MOE_BLOCK)
                return 0

            lax.fori_loop(used, n_blocks, tail, 0)

        slot = i % 2
        subs = range(MOE_STEP_BLOCKS)
        r_iota = lax.broadcasted_iota(jnp.int32, (tb, STAGE_ROWS), 1)
        lrs = [lrow_ref[sb * tb:(sb + 1) * tb, :] for sb in subs]
        sels = [jnp.where((r_iota == lr[:, 0:1]) | (r_iota == lr[:, 1:2]), 1.0, 0.0).astype(BF16) for lr in lrs]
        for sb in subs:
            xs[slot, sb] = _dot_tn(sels[sb], h_ref[sb * tb:(sb + 1) * tb, :].astype(BF16))

        def copies_from(buf, sb):
            def make_copy(lo, gd, rows):
                return pltpu.make_async_copy(xs.at[buf, sb, pl.ds(lo, rows), :], xb_ref.at[pl.ds(gd, rows), :],
                                             sem.at[buf])
            return make_copy

        for sb in subs:
            _tile_copies(nt_ref, lo_ref, gd_ref, i * MOE_STEP_BLOCKS + sb, copies_from(slot, sb), wait=False)

        @pl.when(i > 0)
        def _():
            for sb in subs:
                _tile_copies(nt_ref, lo_ref, gd_ref, (i - 1) * MOE_STEP_BLOCKS + sb, copies_from(1 - slot, sb),
                             wait=True)

        @pl.when(i == pl.num_programs(0) - 1)
        def _():
            for sb in subs:
                _tile_copies(nt_ref, lo_ref, gd_ref, i * MOE_STEP_BLOCKS + sb, copies_from(slot, sb), wait=True)

    return kernel


def _dispatch(h2, lrow, pad_start, nblk, ntile, loff, gdest, n_blocks):
    t, d = h2.shape
    tb = MOE_STEP_BLOCKS * ROW_TILE
    grid_spec = pltpu.PrefetchScalarGridSpec(
        num_scalar_prefetch=5, grid=(t // tb,),
        in_specs=[pl.BlockSpec((tb, LANES), lambda i, *_: (i, 0)),
                  pl.BlockSpec((tb, d), lambda i, *_: (i, 0))],
        out_specs=pl.BlockSpec(memory_space=pl.ANY),
        scratch_shapes=[pltpu.VMEM((2, MOE_STEP_BLOCKS, STAGE_ROWS, d), F32), pltpu.VMEM((MOE_BLOCK, d), F32),
                        pltpu.SemaphoreType.DMA((3,))])
    return pl.pallas_call(
        _make_dispatch_kernel(ROW_TILE, n_blocks), grid_spec=grid_spec,
        out_shape=jax.ShapeDtypeStruct((n_blocks * MOE_BLOCK, d), F32),
        compiler_params=_cparams(("arbitrary",), 44),
        name="moe_dispatch",
    )(pad_start, nblk, ntile, loff, gdest, lrow, h2)


def _expert_kernel(be_ref, nu_ref, xb_ref, wg_ref, wu_ref, wd_ref, yb_ref):
    i = pl.program_id(0)

    @pl.when(i < nu_ref[0])
    def _():
        xb = xb_ref[...].astype(BF16)
        a = (_silu(_dot(xb, wg_ref[...])) * _dot(xb, wu_ref[...])).astype(BF16)
        yb_ref[...] = _dot(a, wd_ref[...]).astype(BF16).astype(F32)

    @pl.when(i >= nu_ref[0])
    def _():
        yb_ref[...] = jnp.zeros_like(yb_ref)


def _experts(xb, block_e, n_used, wg, wu, wd):
    _, d, f = wg.shape
    n_blocks = xb.shape[0] // MOE_BLOCK
    blk = pl.BlockSpec((MOE_BLOCK, d), lambda i, be, nu: (i, 0))
    grid_spec = pltpu.PrefetchScalarGridSpec(
        num_scalar_prefetch=2, grid=(n_blocks,),
        in_specs=[blk,
                  pl.BlockSpec((None, d, f), lambda i, be, nu: (be[i], 0, 0)),
                  pl.BlockSpec((None, d, f), lambda i, be, nu: (be[i], 0, 0)),
                  pl.BlockSpec((None, f, d), lambda i, be, nu: (be[i], 0, 0))],
        out_specs=blk)
    return pl.pallas_call(
        _expert_kernel, grid_spec=grid_spec,
        out_shape=jax.ShapeDtypeStruct(xb.shape, F32),
        compiler_params=_cparams(("arbitrary",), 56),
        name="moe_experts",
    )(block_e, n_used, xb, wg, wu, wd)


def _make_combine_kernel(tb):
    def kernel(nt_ref, lo_ref, gd_ref, yb_ref, lrow_ref, gt_ref, x_ref, g3_ref, fga_ref, o_ref, ys, sem):
        i = pl.program_id(0)
        slot = i % 2
        subs = range(MOE_STEP_BLOCKS)

        def copies_into(buf, sb):
            def make_copy(lo, gd, rows):
                return pltpu.make_async_copy(yb_ref.at[pl.ds(gd, rows), :], ys.at[buf, sb, pl.ds(lo, rows), :],
                                             sem.at[buf])
            return make_copy

        @pl.when(i == 0)
        def _():
            ys[...] = jnp.zeros_like(ys)
            for sb in subs:
                _tile_copies(nt_ref, lo_ref, gd_ref, sb, copies_into(0, sb), wait=False)

        @pl.when(i + 1 < pl.num_programs(0))
        def _():
            for sb in subs:
                _tile_copies(nt_ref, lo_ref, gd_ref, (i + 1) * MOE_STEP_BLOCKS + sb, copies_into(1 - slot, sb),
                             wait=False)

        for sb in subs:
            _tile_copies(nt_ref, lo_ref, gd_ref, i * MOE_STEP_BLOCKS + sb, copies_into(slot, sb), wait=True)

        r_iota = lax.broadcasted_iota(jnp.int32, (tb, STAGE_ROWS), 1)
        rows = [slice(sb * tb, (sb + 1) * tb) for sb in subs]
        lrs = [lrow_ref[rs, :] for rs in rows]
        gts = [gt_ref[rs, :] for rs in rows]
        qs = [jnp.where(r_iota == lr[:, 0:1], gt[:, 0:1], 0.0) + jnp.where(r_iota == lr[:, 1:2], gt[:, 1:2], 0.0)
              for lr, gt in zip(lrs, gts)]
        splits = [_split_bf16(q) for q in qs]
        ybs = [ys[slot, sb].astype(BF16) for sb in subs]
        fs = [_dot(qh, y) + _dot(ql, y) for (qh, ql), y in zip(splits, ybs)]
        for rs, f in zip(rows, fs):
            o_ref[rs, :] = x_ref[rs, :] + fga_ref[...] * _rms(f, g3_ref[...])

    return kernel


def _combine(yb, lrow, ntile, loff, gdest, gates, x, mods6, layer, first_row, rows_per_batch, g3):
    t, d = x.shape
    tb = MOE_STEP_BLOCKS * ROW_TILE
    if rows_per_batch is None:
        fga_map = lambda i, *_: (layer, first_row, 5, 0, 0)
    else:
        steps_per_batch = rows_per_batch // MOE_STEP_BLOCKS
        fga_map = lambda i, *_: (layer, first_row + i // steps_per_batch, 5, 0, 0)
    grid_spec = pltpu.PrefetchScalarGridSpec(
        num_scalar_prefetch=3, grid=(t // tb,),
        in_specs=[pl.BlockSpec(memory_space=pl.ANY),
                  pl.BlockSpec((tb, LANES), lambda i, *_: (i, 0)),
                  pl.BlockSpec((tb, LANES), lambda i, *_: (i, 0)),
                  pl.BlockSpec((tb, d), lambda i, *_: (i, 0)),
                  pl.BlockSpec((1, d), lambda i, *_: (0, 0)),
                  pl.BlockSpec((None, None, None, 1, d), fga_map)],
        out_specs=pl.BlockSpec((tb, d), lambda i, *_: (i, 0)),
        scratch_shapes=[pltpu.VMEM((2, MOE_STEP_BLOCKS, STAGE_ROWS, d), F32), pltpu.SemaphoreType.DMA((2,))])
    return pl.pallas_call(
        _make_combine_kernel(ROW_TILE), grid_spec=grid_spec,
        out_shape=jax.ShapeDtypeStruct((t, d), F32),
        compiler_params=_cparams(("arbitrary",), 58),
        name="moe_combine",
    )(ntile, loff, gdest, yb, lrow, gates, x, g3, mods6)


ROUTER_BLOCKS = 4


def _router(logits):
    t = logits.shape[0]
    tb = ROUTER_BLOCKS * ROW_TILE
    blk = pl.BlockSpec((tb, LANES), lambda i: (i, 0))
    return pl.pallas_call(
        _router_kernel,
        grid=(t // tb,),
        in_specs=[blk],
        out_specs=[blk, blk, pl.BlockSpec((ROUTER_BLOCKS * SUBLANES, LANES), lambda i: (i, 0)),
                   pl.BlockSpec((1, LANES), lambda i: (0, 0))],
        out_shape=[jax.ShapeDtypeStruct((t, LANES), jnp.int32), jax.ShapeDtypeStruct((t, LANES), F32),
                   jax.ShapeDtypeStruct((t // ROW_TILE * SUBLANES, LANES), jnp.int32),
                   jax.ShapeDtypeStruct((1, LANES), jnp.int32)],
        scratch_shapes=[pltpu.VMEM((1, LANES), F32)],
        compiler_params=_cparams(("arbitrary",), 16),
        name="moe_router",
    )(logits)


def _moe(h2, logits, x, mods6, layer, first_row, rows_per_batch, g3, wg, wu, wd):
    t = x.shape[0]
    n_tok_blocks = t // ROW_TILE
    max_rows = t * TOP_K + n_tok_blocks * N_EXPERTS * (SUBLANES - 1)
    n_blocks = -(-max_rows // MOE_BLOCK) + N_EXPERTS
    lrow, gates, tab, tot = _router(logits)
    tab = tab.reshape(n_tok_blocks, SUBLANES, LANES)[:, :, :N_EXPERTS]
    ntile, loff, prior = tab[:, 0], tab[:, 1], tab[:, 2]
    nblk = (tot[0, :N_EXPERTS] + MOE_BLOCK - 1) // MOE_BLOCK
    blk_end = jnp.cumsum(nblk)
    pad_start = ((blk_end - nblk) * MOE_BLOCK).astype(jnp.int32)
    gdest = (pad_start[None, :] + prior).astype(jnp.int32)
    n_used = blk_end[-1:].astype(jnp.int32)
    blk = jnp.minimum(jnp.arange(n_blocks, dtype=jnp.int32), n_used[0] - 1)
    block_e = jnp.minimum(jnp.sum(blk[:, None] >= blk_end[None, :], axis=1), N_EXPERTS - 1).astype(jnp.int32)
    ntile, loff, gdest = ntile.reshape(-1), loff.reshape(-1), gdest.reshape(-1)
    xb = _dispatch(h2, lrow, pad_start, nblk.astype(jnp.int32), ntile, loff, gdest, n_blocks)
    yb = _experts(xb, block_e, n_used, wg, wu, wd)
    return _combine(yb, lrow, ntile, loff, gdest, gates, x, mods6, layer, first_row, rows_per_batch, g3)


def _pad_cols(w, n):
    return jnp.pad(w, ((0, 0), (0, n - w.shape[1])))


def kernel(x_prompt, x_sample, state_mlstm_C, state_mlstm_n, state_mlstm_m, state_ret_S, c, c_ctx, mod_w, mod_b, norm_g, mlstm_w_in, mlstm_gate_b, mlstm_conv_w, mlstm_conv_b, mlstm_head_g, mlstm_w_out, ret_w_in, ret_decay_logit, ret_head_g, ret_w_out, ffn_w_gate, ffn_w_up, ffn_w_down, moe_router, moe_w_gate, moe_w_up, moe_w_down):
    bp, n_p, d = x_prompt.shape
    bs, n_s, _ = x_sample.shape
    depth = mod_w.shape[0]
    assert depth == 2 and ROW_TILE % GRID_W == 0 and n_s % GRID_W == 0

    cond = jnp.zeros((MOD_ROWS, d), F32).at[0].set(c_ctx).at[1:1 + bs].set(c)
    mods6 = _modulation(cond, mod_w, mod_b).reshape(depth, MOD_ROWS, N_MOD, 1, d)

    groups = [dict(x=x_prompt.reshape(bp * n_p, d), first=0, rpb=None, nseq=bp, ntok=n_p, prompt=True),
              dict(x=x_sample.reshape(bs * n_s, d), first=1, rpb=n_s // ROW_TILE, nseq=bs, ntok=n_s, prompt=False)]

    j = 0
    ml_qk = (mlstm_w_in.shape[2] - 4 * HEADS) // 2
    ml_v = ml_qk // 2
    w_in = mlstm_w_in[j]
    wqk = w_in[:, :ml_qk].astype(BF16)
    wv = w_in[:, ml_qk:ml_qk + ml_v].astype(BF16)
    wo = w_in[:, ml_qk + ml_v:ml_qk + 2 * ml_v].astype(BF16)
    w_gate = w_in[:, ml_qk + 2 * ml_v:]
    wgh, wgl = _split_bf16(_pad_cols(w_gate, LANES))
    bcol = _pad_cols(mlstm_gate_b[j][None, :], LANES)
    g = norm_g[0]
    w_out0 = mlstm_w_out[j].astype(BF16)
    new_c = new_n = new_m = None
    casts = (ffn_w_gate[j], ffn_w_up[j], ffn_w_down[j], ret_w_in[j], ret_w_out[j])
    proj = {}
    for grp in reversed(groups):
        args = (mods6, 0, grp["first"], grp["rpb"])
        outs = _proj_mlstm(grp["x"], grp["ntok"], *args, g[0:1], wqk, wv, wo, wgh, wgl, bcol, w_gate.shape[1],
                           mlstm_conv_w[j], mlstm_conv_b[j][None, :], casts=() if grp["prompt"] else casts)
        proj[grp["prompt"]] = outs[:5]
        if not grp["prompt"]:
            fwg, fwu, fwd, rw_in, w_out1 = outs[5:]
    for grp in groups:
        args = (mods6, 0, grp["first"], grp["rpb"])
        q, kt, v, o, grow = proj[grp["prompt"]]
        grow3 = grow.reshape(grow.shape[0], grp["nseq"], grp["ntok"] // CHUNK, CHUNK)
        if grp["prompt"]:
            m0 = jnp.zeros((grp["nseq"] * 2 * HEADS,), F32)
            hs, new_c, new_n, new_m = _mlstm_scan(q, kt, v, grow3, m0, None, grp["nseq"], grp["ntok"], True)
        else:
            (hs,) = _mlstm_scan(q, kt, v, grow3, state_mlstm_m[:, j].reshape(-1),
                                (state_mlstm_C[:, j], state_mlstm_n[:, j]), grp["nseq"], grp["ntok"], False)
        (grp["x"],) = _mixer_out(hs, o, mlstm_head_g[j][None, :], w_out0, grp["x"], *args, g[1:2],
                                 ffn=(g[2:3], g[3:4], fwg, fwu, fwd))

    ret_qk = ret_w_in.shape[2] // 3
    g = norm_g[1]
    wrh, wrl = _split_bf16(_pad_cols(moe_router[j], LANES))
    decay_flat = ret_decay_logit[j].reshape(-1)
    new_s = None
    n_exp, _, d_ff = moe_w_gate.shape[1:]
    casts = (moe_w_gate[j].reshape(n_exp * d, d_ff), moe_w_up[j].reshape(n_exp * d, d_ff),
             moe_w_down[j].reshape(n_exp * d_ff, d))
    proj = {}
    for grp in reversed(groups):
        args = (mods6, 1, grp["first"], grp["rpb"])
        rope_tabs = None if grp["prompt"] else _rope_tables(grp["ntok"], ret_qk // (2 * HEADS))
        outs = _proj_ret(grp["x"], grp["ntok"], *args, g[0:1], rw_in, ret_qk // 2, ret_qk, rope_tabs,
                         casts=() if grp["prompt"] else casts)
        proj[grp["prompt"]] = outs[:4]
        if not grp["prompt"]:
            ewg, ewu, ewd = (w.reshape(n_exp, -1, w.shape[1]) for w in outs[4:])
    for grp in groups:
        args = (mods6, 1, grp["first"], grp["rpb"])
        q, kt, v, gate = proj[grp["prompt"]]
        if grp["prompt"]:
            hs, new_s = _ret_scan(q, kt, v, decay_flat, None, grp["nseq"], grp["ntok"], True)
        else:
            (hs,) = _ret_scan(q, kt, v, decay_flat, state_ret_S[:, j], grp["nseq"], grp["ntok"], False)
        x1, h2, logits = _mixer_out(hs, gate, ret_head_g[j][None, :], w_out1, grp["x"], *args, g[1:2],
                                    router=(g[2:3], wrh, wrl))
        grp["x"] = _moe(h2, logits, x1, *args, g[3:4], ewg, ewu, ewd)

    y_prompt = groups[0]["x"].reshape(bp, n_p, d)
    y_sample = groups[1]["x"].reshape(bs, n_s, d)
    return (y_prompt, y_sample, new_c[:, None], new_n[:, None], new_m[:, None, :, :, 0], new_s[:, None])
```

```python
import functools

import jax
import jax.numpy as jnp
from jax import lax
from jax.experimental import pallas as pl
from jax.experimental.pallas import tpu as pltpu

F32 = jnp.float32
BF16 = jnp.bfloat16

EPS = 1e-6
N_MOD = 6
HEADS = 4
CHUNK = 128
RET_CHUNK = 256
GRID_W = 64
ROPE_BASE = 10000.0
N_EXPERTS = 8
TOP_K = 2
MOE_BLOCK = 256
LANES = 128
SUBLANES = 8
ROW_TILE = 256
MIXER_SUBTILES = 2
MOD_ROWS = 8
STAGE_ROWS = -(-(TOP_K * ROW_TILE + N_EXPERTS * (SUBLANES - 1)) // LANES) * LANES
MIB = 1024 * 1024


def _cparams(sem, vmem_mib):
    return pltpu.CompilerParams(dimension_semantics=sem, vmem_limit_bytes=vmem_mib * MIB)


def _dot(a, b):
    return jnp.dot(a, b, preferred_element_type=F32)


def _dot_tn(a, b):
    return lax.dot_general(a, b, (((0,), (0,)), ((), ())), preferred_element_type=F32)


def _split_bf16(x):
    hi = x.astype(BF16)
    lo = (x - hi.astype(F32)).astype(BF16)
    return hi, lo


def _rms(x, g):
    return x * lax.rsqrt(jnp.mean(x * x, -1, keepdims=True) + EPS) * g


def _adaln(x, g, shift, scale):
    return _rms(x, g) * (1.0 + scale) + shift


def _silu(x):
    return x * jax.nn.sigmoid(x)


def _logsig(x):
    return jnp.minimum(x, 0.0) - jnp.log1p(jnp.exp(-jnp.abs(x)))


def _const_spec(shape):
    nd = len(shape)
    return pl.BlockSpec(shape, lambda *_: (0,) * nd, pipeline_mode=pl.Buffered(1))


def _mod_spec(d, layer, comp, rows_per_batch, first_row):
    if rows_per_batch is None:
        return pl.BlockSpec((None, None, None, 1, d), lambda i: (layer, first_row, comp, 0, 0))
    return pl.BlockSpec((None, None, None, 1, d), lambda i: (layer, first_row + i // rows_per_batch, comp, 0, 0))


def _mod_kernel(c_ref, w_ref, b_ref, o_ref):
    s = _silu(c_ref[...]).astype(BF16)
    o_ref[...] = _dot(s, w_ref[...].astype(BF16)) + b_ref[...]


def _modulation(cond, mod_w, mod_b):
    depth, d, n = mod_w.shape
    tn = n // 4
    return pl.pallas_call(
        _mod_kernel,
        grid=(depth, n // tn),
        in_specs=[pl.BlockSpec((MOD_ROWS, d), lambda l, j: (0, 0)),
                  pl.BlockSpec((None, d, tn), lambda l, j: (l, 0, j)),
                  pl.BlockSpec((None, 1, tn), lambda l, j: (l, 0, j))],
        out_specs=pl.BlockSpec((None, MOD_ROWS, tn), lambda l, j: (l, 0, j)),
        out_shape=jax.ShapeDtypeStruct((depth, MOD_ROWS, n), F32),
        compiler_params=_cparams(("arbitrary", "arbitrary"), 40),
        name="modulation",
    )(cond, mod_w, mod_b.reshape(depth, 1, n))


BF16_ROWS = 16


def _cast_plan(arrays, n_steps):
    in_specs, out_specs, out_shape, nblks = [], [], [], []
    for a in arrays:
        rows, cols = a.shape
        nblk = max(n for n in range(1, n_steps + 1) if rows % n == 0 and (rows // n) % BF16_ROWS == 0)
        spec = pl.BlockSpec((rows // nblk, cols), lambda i, nb=nblk: (jnp.minimum(i, nb - 1), 0))
        in_specs.append(spec)
        out_specs.append(spec)
        out_shape.append(jax.ShapeDtypeStruct(a.shape, BF16))
        nblks.append(nblk)
    return in_specs, out_specs, out_shape, nblks


def _cast_slabs(srcs, dsts, nblks):
    for src, dst, nb in zip(srcs, dsts, nblks):
        @pl.when(pl.program_id(0) < nb)
        def _():
            dst[...] = src[...].astype(BF16)


def _proj_mlstm_kernel(tiles_per_seq, cast_nblks, *refs):
    nc = len(cast_nblks)
    (x_ref, xp_ref, xn_ref, g_ref, sh_ref, sc_ref, wqk_ref, wv_ref, wo_ref, wgh_ref, wgl_ref, bcol_ref, cw_ref,
     cb_ref) = refs[:14]
    q_ref, kt_ref, v_ref, o_ref, grow_ref = refs[14 + nc:19 + nc]
    _cast_slabs(refs[14:14 + nc], refs[19 + nc:], cast_nblks)
    tm = x_ref.shape[0]
    nq = q_ref.shape[1]
    h = _adaln(x_ref[...], g_ref[...], sh_ref[...], sc_ref[...])
    hb, hl = _split_bf16(h)
    x_halo = jnp.concatenate([xp_ref[...], xn_ref[...]], axis=0)
    hb_halo = _adaln(x_halo, g_ref[...], sh_ref[...], sc_ref[...]).astype(BF16)
    hb_all = jnp.concatenate([hb, hb_halo], axis=0)
    pos = pl.program_id(0) % tiles_per_seq
    has_prev = jnp.where(pos > 0, 1.0, 0.0)
    has_next = jnp.where(pos < tiles_per_seq - 1, 1.0, 0.0)
    wc = nq // HEADS
    rowi = lax.broadcasted_iota(jnp.int32, (tm, wc), 0)
    for c in range(2 * HEADS):
        sl = slice(c * wc, (c + 1) * wc)
        p_all = _dot(hb_all, wqk_ref[:, sl])
        p = p_all[:tm]
        p_prev = p_all[tm + SUBLANES - 1:tm + SUBLANES, :] * has_prev
        p_next = p_all[tm + SUBLANES:tm + SUBLANES + 1, :] * has_next
        prev = jnp.where(rowi == 0, p_prev, pltpu.roll(p, 1, axis=0))
        nxt = jnp.where(rowi == tm - 1, p_next, pltpu.roll(p, tm - 1, axis=0))
        y = prev * cw_ref[0:1, sl] + p * cw_ref[1:2, sl] + nxt * cw_ref[2:3, sl] + cb_ref[:, sl]
        if c < HEADS:
            q_ref[:, sl] = (y * wc ** -0.5).astype(BF16)
        else:
            kt_ref[(c - HEADS) * wc:(c - HEADS + 1) * wc, :] = y.T.astype(BF16)
    v_ref[...] = _dot(hb, wv_ref[...]).astype(BF16)
    o_ref[...] = jax.nn.sigmoid(_dot(hb, wo_ref[...]))
    gc = _dot(hb, wgh_ref[...]) + _dot(hl, wgh_ref[...]) + _dot(hb, wgl_ref[...]) + bcol_ref[...]
    grow_ref[...] = gc.T[:grow_ref.shape[0], :]


def _proj_mlstm(x, n_tok, mods6, layer, first_row, rows_per_batch, g, wqk, wv, wo, wgh, wgl, bcol, ng, conv_w, conv_b,
                casts=()):
    t, d = x.shape
    tm = ROW_TILE
    ms = lambda comp: _mod_spec(d, layer, comp, rows_per_batch, first_row)
    nq = wqk.shape[1] // 2
    tps = tm // SUBLANES
    last = t // SUBLANES - 1
    c_in, c_out, c_shape, c_nblks = _cast_plan(casts, t // tm)
    return pl.pallas_call(
        functools.partial(_proj_mlstm_kernel, n_tok // tm, tuple(c_nblks)),
        grid=(t // tm,),
        in_specs=[pl.BlockSpec((tm, d), lambda i: (i, 0)),
                  pl.BlockSpec((SUBLANES, d), lambda i: (jnp.maximum(i * tps - 1, 0), 0)),
                  pl.BlockSpec((SUBLANES, d), lambda i: (jnp.minimum((i + 1) * tps, last), 0)),
                  _const_spec((1, d)), ms(0), ms(1),
                  _const_spec(wqk.shape), _const_spec(wv.shape), _const_spec(wo.shape),
                  _const_spec(wgh.shape), _const_spec(wgl.shape), _const_spec(bcol.shape),
                  _const_spec(conv_w.shape), _const_spec(conv_b.shape)] + c_in,
        out_specs=[pl.BlockSpec((tm, nq), lambda i: (i, 0)),
                   pl.BlockSpec((nq, tm), lambda i: (0, i)),
                   pl.BlockSpec((tm, wv.shape[1]), lambda i: (i, 0)),
                   pl.BlockSpec((tm, wo.shape[1]), lambda i: (i, 0)),
                   pl.BlockSpec((ng, tm), lambda i: (0, i))] + c_out,
        out_shape=[jax.ShapeDtypeStruct((t, nq), BF16),
                   jax.ShapeDtypeStruct((nq, t), BF16),
                   jax.ShapeDtypeStruct((t, wv.shape[1]), BF16),
                   jax.ShapeDtypeStruct((t, wo.shape[1]), F32),
                   jax.ShapeDtypeStruct((ng, t), F32)] + c_shape,
        compiler_params=_cparams(("arbitrary",), 56),
        name="proj_mlstm",
    )(x, x, x, g, mods6, mods6, wqk, wv, wo, wgh, wgl, bcol, conv_w, conv_b, *casts)


def _make_mlstm_scan_kernel(n_tok, dk, dv, has_state, emit_state):
    L = CHUNK
    nc = n_tok // L
    assert nc % 2 == 0 and L == LANES

    def kernel(*refs):
        it = iter(refs)
        m0_ref = next(it)
        q_ref, kt_ref, v_ref = next(it), next(it), next(it)
        g_ref = next(it)
        if has_state:
            c0_ref, n0_ref = next(it), next(it)
        out_ref = next(it)
        if emit_state:
            cout_ref, nout_ref, mout_ref = next(it), next(it), next(it)
        bsc, rsc, rmsc, cst, cbf, nrep, nbf = (next(it) for _ in range(7))

        b = pl.program_id(0)
        hh = pl.program_id(1)

        ri = lax.broadcasted_iota(jnp.int32, (L, L), 0)
        ci = lax.broadcasted_iota(jnp.int32, (L, L), 1)
        lower = ri >= ci
        upper = ri <= ci
        masks = ((lower, upper), (upper, lower))

        lane = lax.broadcasted_iota(jnp.int32, (nc, L), 1)

        def lane_scan(x, op, fill, reverse):
            s = 1
            while s < L:
                if reverse:
                    x = op(x, jnp.where(lane < L - s, pltpu.roll(x, L - s, axis=1), fill))
                else:
                    x = op(x, jnp.where(lane >= s, pltpu.roll(x, s, axis=1), fill))
                s *= 2
            return x

        @pl.when(hh == 0)
        def _():
            for d in range(2):
                for h in range(HEADS):
                    row_i, row_f = d * 2 * HEADS + h, d * 2 * HEADS + HEADS + h
                    b_all = lane_scan(_logsig(g_ref[row_f]), jnp.add, 0.0, reverse=(d == 1))
                    r_all = g_ref[row_i] - b_all
                    bsc[d, h] = b_all
                    rsc[d, h] = r_all
                    rmsc[d, h] = lane_scan(r_all, jnp.maximum, -jnp.inf, reverse=(d == 1))

        if has_state:
            cst[...] = c0_ref[...]
            for d in range(2):
                nrep[d] = jnp.broadcast_to(n0_ref[d], (dk, LANES))
        else:
            cst[...] = jnp.zeros_like(cst)
            nrep[...] = jnp.zeros_like(nrep)
        cbf[...] = cst[...].astype(BF16)
        nbf[...] = nrep[...].astype(BF16)
        m_init = tuple(jnp.full((1, LANES), m0_ref[(b * 2 + d) * HEADS + hh], F32) for d in range(2))

        def lanes(x, n):
            return jnp.concatenate([x] * (n // LANES), axis=1)

        def chunk(d, c, m):
            mk = masks[d][0]
            r0 = pl.multiple_of(c * L, L)
            q = q_ref[pl.ds(r0, L), :]
            kt = kt_ref[:, pl.ds(r0, L)]
            v = v_ref[pl.ds(r0, L), :]
            rr = rsc[d, hh, pl.ds(c, 1), :]
            b_rep = jnp.broadcast_to(bsc[d, hh, pl.ds(c, 1), :], (L, L)).T
            rm_rep = jnp.broadcast_to(rmsc[d, hh, pl.ds(c, 1), :], (L, L)).T
            end = 0 if d == 1 else L - 1
            b_end, rm_end = b_rep[end:end + 1, :], rm_rep[end:end + 1, :]
            mm = jnp.maximum(m, rm_rep)
            qkn = _dot(q, jnp.concatenate([kt, nbf[d]], axis=1))
            s = qkn[:, :L] * jnp.where(mk, jnp.exp(rr - mm), 0.0)
            w_inter = jnp.exp(m - mm)
            num = lanes(w_inter, dv) * _dot(q, cbf[d]) + _dot(s.astype(BF16), v)
            den = w_inter * qkn[:, L:] + jnp.sum(s, axis=1, keepdims=True)
            inv = 1.0 / jnp.maximum(jnp.abs(den), jnp.exp(-(b_rep + mm)))
            h = num * lanes(inv, dv)
            m_end = jnp.maximum(m, rm_end)
            decay = jnp.exp(m - m_end)
            kw = kt.astype(F32) * jnp.exp(rr - m_end)
            c_new = lanes(decay, dv) * cst[d] + _dot(kw.astype(BF16), v)
            n_new = decay * nrep[d] + jnp.sum(kw, axis=1, keepdims=True)
            cst[d] = c_new
            cbf[d] = c_new.astype(BF16)
            nrep[d] = n_new
            nbf[d] = n_new.astype(BF16)
            return h, b_end + m_end

        def make_body(accumulate):
            def body(i, carry):
                mf, mb = carry
                hf, mf = chunk(0, i, mf)
                rf = pl.multiple_of(i * L, L)
                cb = nc - 1 - i
                hb, mb = chunk(1, cb, mb)
                rb = pl.multiple_of(cb * L, L)
                if accumulate:
                    out_ref[pl.ds(rf, L), :] += hf
                    out_ref[pl.ds(rb, L), :] += hb
                else:
                    out_ref[pl.ds(rf, L), :] = hf
                    out_ref[pl.ds(rb, L), :] = hb
                return mf, mb
            return body

        unroll = 2 if nc % 4 == 0 else 1
        carry = lax.fori_loop(0, nc // 2, make_body(False), m_init, unroll=unroll)
        carry = lax.fori_loop(nc // 2, nc, make_body(True), carry, unroll=unroll)

        if emit_state:
            cout_ref[...] = cst[...]
            for d in range(2):
                nout_ref[d, pl.ds(hh, 1), :] = nrep[d].T[0:1, :]
                mout_ref[d, pl.ds(hh, 1), :] = carry[d]

    return kernel


def _mlstm_scan(q, kt, v, gates, m0_flat, state, n_seq, n_tok, emit_state):
    dk = q.shape[1] // HEADS
    dv = v.shape[1] // HEADS
    nc = n_tok // CHUNK
    has_state = state is not None
    kern = _make_mlstm_scan_kernel(n_tok, dk, dv, has_state, emit_state)
    in_specs = [pl.BlockSpec((n_tok, dk), lambda b, h, m: (b, h)),
                pl.BlockSpec((dk, n_tok), lambda b, h, m: (h, b)),
                pl.BlockSpec((n_tok, dv), lambda b, h, m: (b, h)),
                pl.BlockSpec((gates.shape[0], None, nc, CHUNK), lambda b, h, m: (0, b, 0, 0))]
    args = [q, kt, v, gates]
    if has_state:
        in_specs += [pl.BlockSpec((None, 2, None, dk, dv), lambda b, h, m: (b, 0, h, 0, 0)),
                     pl.BlockSpec((None, 2, None, dk, 1), lambda b, h, m: (b, 0, h, 0, 0))]
        args += [state[0], state[1][..., None]]
    out_specs = [pl.BlockSpec((n_tok, dv), lambda b, h, m: (b, h))]
    out_shape = [jax.ShapeDtypeStruct((n_seq * n_tok, HEADS * dv), F32)]
    if emit_state:
        out_specs += [pl.BlockSpec((None, 2, None, dk, dv), lambda b, h, m: (b, 0, h, 0, 0)),
                      pl.BlockSpec((None, 2, HEADS, dk), lambda b, h, m: (b, 0, 0, 0)),
                      pl.BlockSpec((None, 2, HEADS, LANES), lambda b, h, m: (b, 0, 0, 0))]
        out_shape += [jax.ShapeDtypeStruct((n_seq, 2, HEADS, dk, dv), F32),
                      jax.ShapeDtypeStruct((n_seq, 2, HEADS, dk), F32),
                      jax.ShapeDtypeStruct((n_seq, 2, HEADS, LANES), F32)]
    grid_spec = pltpu.PrefetchScalarGridSpec(
        num_scalar_prefetch=1, grid=(n_seq, HEADS), in_specs=in_specs, out_specs=out_specs,
        scratch_shapes=[pltpu.VMEM((2, HEADS, nc, CHUNK), F32),
                        pltpu.VMEM((2, HEADS, nc, CHUNK), F32),
                        pltpu.VMEM((2, HEADS, nc, CHUNK), F32),
                        pltpu.VMEM((2, dk, dv), F32), pltpu.VMEM((2, dk, dv), BF16),
                        pltpu.VMEM((2, dk, LANES), F32), pltpu.VMEM((2, dk, LANES), BF16)])
    return pl.pallas_call(
        kern, grid_spec=grid_spec, out_shape=out_shape,
        compiler_params=_cparams(("arbitrary", "arbitrary"), 56),
        name="mlstm_scan",
    )(m0_flat, *args)


def _proj_ret_kernel(rope, cast_nblks, *refs):
    nc = len(cast_nblks)
    n_in = 5 + (4 if rope else 0)
    x_ref, g_ref, sh_ref, sc_ref, w_ref = refs[:5]
    if rope:
        rcos_ref, rsin_ref, ccos_ref, csin_ref = refs[5:9]
    q_ref, kt_ref, v_ref, gate_ref = refs[n_in + nc:n_in + nc + 4]
    _cast_slabs(refs[n_in:n_in + nc], refs[n_in + nc + 4:], cast_nblks)
    nq, nv = q_ref.shape[1], v_ref.shape[1]
    wc = nq // HEADS
    hb = _adaln(x_ref[...], g_ref[...], sh_ref[...], sc_ref[...]).astype(BF16)
    if rope:
        cos = jnp.concatenate([rcos_ref[...], ccos_ref[...]], axis=1)
        sin = jnp.concatenate([rsin_ref[...], csin_ref[...]], axis=1)
    for c in range(2 * HEADS):
        p = _dot(hb, w_ref[:, c * wc:(c + 1) * wc])
        if rope:
            swapped = jnp.concatenate([pltpu.roll(p[:, j * LANES:(j + 1) * LANES], LANES // 2, axis=1)
                                       for j in range(wc // LANES)], axis=1)
            p = p * cos + swapped * sin
        if c < HEADS:
            q_ref[:, c * wc:(c + 1) * wc] = p.astype(BF16)
        else:
            kt_ref[(c - HEADS) * wc:(c - HEADS + 1) * wc, :] = (p * wc ** -0.5).T.astype(BF16)
    v_ref[...] = _dot(hb, w_ref[:, 2 * nq:2 * nq + nv]).astype(BF16)
    gate_ref[...] = _silu(_dot(hb, w_ref[:, 2 * nq + nv:])).astype(BF16)


def _proj_ret(x, n_tok, mods6, layer, first_row, rows_per_batch, g, w_in, nq, nv, rope_tabs, casts=()):
    t, d = x.shape
    tm = ROW_TILE
    ms = lambda comp: _mod_spec(d, layer, comp, rows_per_batch, first_row)
    ng = w_in.shape[1] - 2 * nq - nv
    rope = rope_tabs is not None
    in_specs = [pl.BlockSpec((tm, d), lambda i: (i, 0)), _const_spec((1, d)), ms(0), ms(1), _const_spec(w_in.shape)]
    args = [x, g, mods6, mods6, w_in]
    if rope:
        tiles_per_seq = n_tok // tm
        row_spec = pl.BlockSpec((tm, LANES), lambda i: (i % tiles_per_seq, 0))
        in_specs += [row_spec, row_spec, _const_spec((tm, LANES)), _const_spec((tm, LANES))]
        args += list(rope_tabs)
    c_in, c_out, c_shape, c_nblks = _cast_plan(casts, t // tm)
    return pl.pallas_call(
        functools.partial(_proj_ret_kernel, rope, tuple(c_nblks)),
        grid=(t // tm,),
        in_specs=in_specs + c_in,
        out_specs=[pl.BlockSpec((tm, nq), lambda i: (i, 0)),
                   pl.BlockSpec((nq, tm), lambda i: (0, i)),
                   pl.BlockSpec((tm, nv), lambda i: (i, 0)),
                   pl.BlockSpec((tm, ng), lambda i: (i, 0))] + c_out,
        out_shape=[jax.ShapeDtypeStruct((t, nq), BF16),
                   jax.ShapeDtypeStruct((nq, t), BF16),
                   jax.ShapeDtypeStruct((t, nv), BF16),
                   jax.ShapeDtypeStruct((t, ng), BF16)] + c_shape,
        compiler_params=_cparams(("arbitrary",), 56),
        name="proj_ret",
    )(*args, *casts)


def _make_ret_scan_kernel(n_tok, dk, dv, has_state, emit_state):
    L = RET_CHUNK
    nc = n_tok // L
    assert nc == 1 or nc % 2 == 0

    def kernel(*refs):
        it = iter(refs)
        dl_ref = next(it)
        q_ref, kt_ref, v_ref = next(it), next(it), next(it)
        if has_state:
            s0_ref = next(it)
        out_ref = next(it)
        if emit_state:
            sout_ref = next(it)
        sst, sbf = next(it), next(it)
        if nc > 1:
            acc = next(it)

        hh = pl.program_id(1)
        ri = lax.broadcasted_iota(jnp.int32, (L, L), 0)
        ci = lax.broadcasted_iota(jnp.int32, (L, L), 1)
        rel = (ri - ci).astype(F32)
        pos_col = lax.broadcasted_iota(jnp.int32, (L, LANES), 0).astype(F32)
        pos_row = lax.broadcasted_iota(jnp.int32, (1, L), 1).astype(F32)

        dmat, q_dec, k_dec, c_dec = [], [], [], []
        for d in range(2):
            lg = _logsig(jnp.full((1, 1), dl_ref[d * HEADS + hh], F32))
            if d == 0:
                dmat.append(jnp.where(ri >= ci, jnp.exp(lg * jnp.maximum(rel, 0.0)), 0.0))
                q_dec.append(jnp.exp(lg * (pos_col + 1.0)))
                k_dec.append(jnp.exp(lg * (L - 1.0 - pos_row)))
            else:
                dmat.append(jnp.where(ri <= ci, jnp.exp(lg * jnp.maximum(-rel, 0.0)), 0.0))
                q_dec.append(jnp.exp(lg * (L - pos_col)))
                k_dec.append(jnp.exp(lg * pos_row))
            c_dec.append(jnp.broadcast_to(jnp.exp(lg * float(L)), (1, LANES)))

        if has_state:
            sst[...] = s0_ref[...]
        else:
            sst[...] = jnp.zeros_like(sst)
        sbf[...] = sst[...].astype(BF16)

        def lanes(x, n):
            return jnp.concatenate([x] * (n // LANES), axis=1)

        def chunk(d, c):
            r0 = c * L if isinstance(c, int) else pl.multiple_of(c * L, L)
            q = q_ref[pl.ds(r0, L), :]
            kt = kt_ref[:, pl.ds(r0, L)]
            v = v_ref[pl.ds(r0, L), :]
            s = _dot(q, kt) * dmat[d]
            out = _dot(s.astype(BF16), v) + lanes(q_dec[d], dv) * _dot(q, sbf[d])
            s_new = lanes(c_dec[d], dv) * sst[d] + _dot((kt.astype(F32) * k_dec[d]).astype(BF16), v)
            sst[d] = s_new
            sbf[d] = s_new.astype(BF16)
            return out

        def body1(i, _):
            rf = pl.multiple_of(i * L, L)
            acc[pl.ds(rf, L), :] = chunk(0, i)
            cb = nc - 1 - i
            rb = pl.multiple_of(cb * L, L)
            acc[pl.ds(rb, L), :] = chunk(1, cb)
            return 0

        def body2(i, _):
            rf = pl.multiple_of(i * L, L)
            out_ref[pl.ds(rf, L), :] = (acc[pl.ds(rf, L), :] + chunk(0, i)).astype(BF16)
            cb = nc - 1 - i
            rb = pl.multiple_of(cb * L, L)
            out_ref[pl.ds(rb, L), :] = (acc[pl.ds(rb, L), :] + chunk(1, cb)).astype(BF16)
            return 0

        if nc == 1:
            out_ref[...] = (chunk(0, 0) + chunk(1, 0)).astype(BF16)
        else:
            unroll = 2 if nc % 4 == 0 else 1
            lax.fori_loop(0, nc // 2, body1, 0, unroll=unroll)
            lax.fori_loop(nc // 2, nc, body2, 0, unroll=unroll)
        if emit_state:
            sout_ref[...] = sst[...]

    return kernel


def _ret_scan(q, kt, v, decay_flat, state, n_seq, n_tok, emit_state):
    dk = q.shape[1] // HEADS
    dv = v.shape[1] // HEADS
    has_state = state is not None
    kern = _make_ret_scan_kernel(n_tok, dk, dv, has_state, emit_state)
    in_specs = [pl.BlockSpec((n_tok, dk), lambda b, h, m: (b, h)),
                pl.BlockSpec((dk, n_tok), lambda b, h, m: (h, b)),
                pl.BlockSpec((n_tok, dv), lambda b, h, m: (b, h))]
    args = [q, kt, v]
    if has_state:
        in_specs.append(pl.BlockSpec((None, 2, None, dk, dv), lambda b, h, m: (b, 0, h, 0, 0)))
        args.append(state)
    out_specs = [pl.BlockSpec((n_tok, dv), lambda b, h, m: (b, h))]
    out_shape = [jax.ShapeDtypeStruct((n_seq * n_tok, HEADS * dv), BF16)]
    if emit_state:
        out_specs.append(pl.BlockSpec((None, 2, None, dk, dv), lambda b, h, m: (b, 0, h, 0, 0)))
        out_shape.append(jax.ShapeDtypeStruct((n_seq, 2, HEADS, dk, dv), F32))
    grid_spec = pltpu.PrefetchScalarGridSpec(
        num_scalar_prefetch=1, grid=(n_seq, HEADS), in_specs=in_specs, out_specs=out_specs,
        scratch_shapes=[pltpu.VMEM((2, dk, dv), F32), pltpu.VMEM((2, dk, dv), BF16)]
        + ([pltpu.VMEM((n_tok, dv), F32)] if n_tok > RET_CHUNK else []))
    return pl.pallas_call(
        kern, grid_spec=grid_spec, out_shape=out_shape,
        compiler_params=_cparams(("arbitrary", "arbitrary"), 56),
        name="ret_scan",
    )(decay_flat, *args)


def _rope_tables(n_tok, dk):
    r = dk // 4
    inv = 1.0 / (ROPE_BASE ** (jnp.arange(r, dtype=F32) / r))
    sign = jnp.concatenate([-jnp.ones((r,), F32), jnp.ones((r,), F32)])
    rows = (jnp.arange(n_tok) // GRID_W).astype(F32)[:, None] * inv
    cols = (jnp.arange(ROW_TILE) % GRID_W).astype(F32)[:, None] * inv
    two = lambda a: jnp.concatenate([a, a], axis=-1)
    return (two(jnp.cos(rows)), two(jnp.sin(rows)) * sign, two(jnp.cos(cols)), two(jnp.sin(cols)) * sign)


def _make_mixer_out_kernel(dv, with_router, with_ffn):
    def kernel(*refs):
        it = iter(refs)
        hs_ref, gate_ref, hg_ref, w_ref, x_ref, g1_ref, ga_ref = (next(it) for _ in range(7))
        if with_router:
            g2_ref, fsh_ref, fsc_ref, wrh_ref, wrl_ref = (next(it) for _ in range(5))
        if with_ffn:
            g2_ref, fsh_ref, fsc_ref, wg_ref, wu_ref, wd_ref, g3_ref, fga_ref = (next(it) for _ in range(8))
        xo_ref = next(it)
        if with_router:
            h2_ref, lg_ref = next(it), next(it)
        z_ref = next(it)
        subs = [slice(r * ROW_TILE, (r + 1) * ROW_TILE) for r in range(x_ref.shape[0] // ROW_TILE)]
        for rs in subs:
            for hh in range(HEADS):
                sl = slice(hh * dv, (hh + 1) * dv)
                seg = hs_ref[rs, sl].astype(F32)
                y = seg * lax.rsqrt(jnp.mean(seg * seg, -1, keepdims=True) + EPS) * hg_ref[:, sl]
                z_ref[rs, sl] = (gate_ref[rs, sl].astype(F32) * y).astype(BF16)
        ys = [_dot(z_ref[rs, :], w_ref[...]) for rs in subs]
        xns = [x_ref[rs, :] + ga_ref[...] * _rms(y, g1_ref[...]) for rs, y in zip(subs, ys)]
        if with_ffn:
            hbs = [_adaln(xn, g2_ref[...], fsh_ref[...], fsc_ref[...]).astype(BF16) for xn in xns]
            acts = [(_silu(_dot(hb, wg_ref[...])) * _dot(hb, wu_ref[...])).astype(BF16) for hb in hbs]
            downs = [_dot(a, wd_ref[...]) for a in acts]
            xns = [xn + fga_ref[...] * _rms(dn, g3_ref[...]) for xn, dn in zip(xns, downs)]
        for rs, xn in zip(subs, xns):
            xo_ref[rs, :] = xn
        if with_router:
            for rs, xn in zip(subs, xns):
                h2 = _adaln(xn, g2_ref[...], fsh_ref[...], fsc_ref[...])
                h2_ref[rs, :] = h2
                hb, hl = _split_bf16(h2)
                lg_ref[rs, :] = _dot(hb, wrh_ref[...]) + _dot(hl, wrh_ref[...]) + _dot(hb, wrl_ref[...])
    return kernel


def _mixer_out(hs, gate, head_g, w_out, x, mods6, layer, first_row, rows_per_batch, g1, router=None, ffn=None):
    t, d = x.shape
    vdim = hs.shape[1]
    tm = MIXER_SUBTILES * ROW_TILE
    tiles_per_batch = None if rows_per_batch is None else rows_per_batch // MIXER_SUBTILES
    ms = lambda comp: _mod_spec(d, layer, comp, tiles_per_batch, first_row)
    with_router = router is not None
    with_ffn = ffn is not None
    in_specs = [pl.BlockSpec((tm, vdim), lambda i: (i, 0)),
                pl.BlockSpec((tm, vdim), lambda i: (i, 0)),
                _const_spec((1, vdim)), _const_spec(w_out.shape),
                pl.BlockSpec((tm, d), lambda i: (i, 0)), _const_spec((1, d)), ms(2)]
    args = [hs, gate, head_g, w_out, x, g1, mods6]
    out_specs = [pl.BlockSpec((tm, d), lambda i: (i, 0))]
    out_shape = [jax.ShapeDtypeStruct((t, d), F32)]
    if with_router:
        g2, wrh, wrl = router
        in_specs += [_const_spec((1, d)), ms(3), ms(4), _const_spec(wrh.shape), _const_spec(wrl.shape)]
        args += [g2, mods6, mods6, wrh, wrl]
        out_specs += [pl.BlockSpec((tm, d), lambda i: (i, 0)), pl.BlockSpec((tm, LANES), lambda i: (i, 0))]
        out_shape += [jax.ShapeDtypeStruct((t, d), F32), jax.ShapeDtypeStruct((t, LANES), F32)]
    if with_ffn:
        g2, g3, wg, wu, wd = ffn
        in_specs += [_const_spec((1, d)), ms(3), ms(4), _const_spec(wg.shape), _const_spec(wu.shape),
                     _const_spec(wd.shape), _const_spec((1, d)), ms(5)]
        args += [g2, mods6, mods6, wg, wu, wd, g3, mods6]
    return pl.pallas_call(
        _make_mixer_out_kernel(vdim // HEADS, with_router, with_ffn),
        grid=(t // tm,), in_specs=in_specs, out_specs=out_specs, out_shape=out_shape,
        scratch_shapes=[pltpu.VMEM((tm, vdim), BF16)],
        compiler_params=_cparams(("arbitrary",), 58),
        name="mixer_ffn" if with_ffn else "mixer_out",
    )(*args)


def _router_kernel(lg_ref, lrow_ref, gt_ref, tab_ref, tot_ref, carry):
    tb = ROW_TILE
    n_sub = lg_ref.shape[0] // tb

    @pl.when(pl.program_id(0) == 0)
    def _():
        carry[...] = jnp.zeros_like(carry)

    lane = lax.broadcasted_iota(jnp.int32, (tb, LANES), 1)
    ri = lax.broadcasted_iota(jnp.int32, (tb, tb), 0)
    ci = lax.broadcasted_iota(jnp.int32, (tb, tb), 1)
    before = jnp.where(ri > ci, 1.0, 0.0).astype(BF16)
    ei = lax.broadcasted_iota(jnp.int32, (LANES, LANES), 0)
    ej = lax.broadcasted_iota(jnp.int32, (LANES, LANES), 1)
    earlier = jnp.where(ei < ej, 1.0, 0.0).astype(BF16)
    sub = lax.broadcasted_iota(jnp.int32, (SUBLANES, LANES), 0)
    rmax = lambda x: jnp.max(x, axis=1, keepdims=True)
    rmin = lambda x: jnp.min(x, axis=1, keepdims=True)
    rsum = lambda x: jnp.sum(x, axis=1, keepdims=True)

    lgs = [jnp.where(lane < N_EXPERTS, lg_ref[k * tb:(k + 1) * tb, :], -jnp.inf) for k in range(n_sub)]
    v1s = [rmax(lg) for lg in lgs]
    i1s = [rmin(jnp.where(lg == v1, lane, LANES)) for lg, v1 in zip(lgs, v1s)]
    lg2s = [jnp.where(lane == i1, -jnp.inf, lg) for lg, i1 in zip(lgs, i1s)]
    v2s = [rmax(lg2) for lg2 in lg2s]
    i2s = [rmin(jnp.where(lg2 == v2, lane, LANES)) for lg2, v2 in zip(lg2s, v2s)]
    oh1s = [lane == i1 for i1 in i1s]
    oh2s = [lane == i2 for i2 in i2s]
    onehots = [jnp.where(a | b, 1.0, 0.0) for a, b in zip(oh1s, oh2s)]
    ranks = [_dot(before, oh.astype(BF16)) for oh in onehots]
    tiless = [jnp.ceil(jnp.sum(oh, axis=0, keepdims=True) * (1.0 / SUBLANES)) for oh in onehots]
    offs = [_dot(jnp.broadcast_to(tl, (SUBLANES, LANES)).astype(BF16), earlier)[0:1] * SUBLANES for tl in tiless]
    r1s = [rsum(jnp.where(oh, rk + off, 0.0)) for oh, rk, off in zip(oh1s, ranks, offs)]
    r2s = [rsum(jnp.where(oh, rk + off, 0.0)) for oh, rk, off in zip(oh2s, ranks, offs)]
    prior = carry[...]
    for k in range(n_sub):
        rs = slice(k * tb, (k + 1) * tb)
        ex = jnp.exp(v2s[k] - v1s[k])
        den = 1.0 + ex
        lrow_ref[rs, :] = jnp.where(lane == 0, r1s[k], jnp.where(lane == 1, r2s[k], 0.0)).astype(jnp.int32)
        gt_ref[rs, :] = jnp.where(lane == 0, 1.0 / den, jnp.where(lane == 1, ex / den, 0.0))
        tab = jnp.where(sub == 0, tiless[k], jnp.where(sub == 1, offs[k], jnp.where(sub == 2, prior, 0.0)))
        tab_ref[k * SUBLANES:(k + 1) * SUBLANES, :] = tab.astype(jnp.int32)
        prior = prior + tiless[k] * SUBLANES
    carry[...] = prior
    tot_ref[...] = prior.astype(jnp.int32)


RUN_TILES = 8
MOE_STEP_BLOCKS = 2


def _tile_copies(nt_ref, lo_ref, gd_ref, blk, make_copy, wait):
    def go(lo, gd, rows):
        cp = make_copy(pl.multiple_of(lo, SUBLANES), pl.multiple_of(gd, SUBLANES), rows)
        if wait:
            cp.wait()
        else:
            cp.start()

    for e in range(N_EXPERTS):
        idx = blk * N_EXPERTS + e
        nt, lo, gd = nt_ref[idx], lo_ref[idx], gd_ref[idx]

        def run(j, _):
            off = j * (RUN_TILES * SUBLANES)
            go(lo + off, gd + off, RUN_TILES * SUBLANES)
            return 0

        lax.fori_loop(0, nt // RUN_TILES, run, 0)
        k = RUN_TILES // 2
        while k >= 1:
            @pl.when(nt % (2 * k) >= k)
            def _():
                off = (nt // (2 * k)) * (2 * k) * SUBLANES
                go(lo + off, gd + off, k * SUBLANES)
            k //= 2


def _make_dispatch_kernel(tb, n_blocks):
    def kernel(ps_ref, nb_ref, nt_ref, lo_ref, gd_ref, lrow_ref, h_ref, xb_ref, xs, zbuf, sem):
        i = pl.program_id(0)

        @pl.when(i == 0)
        def _():
            zbuf[...] = jnp.zeros_like(zbuf)

            def zero_block(row0):
                dst = pl.multiple_of(row0, MOE_BLOCK)
                cp = pltpu.make_async_copy(zbuf, xb_ref.at[pl.ds(dst, MOE_BLOCK), :], sem.at[2])
                cp.start()
                cp.wait()

            for e in range(N_EXPERTS):
                @pl.when(nb_ref[e] > 0)
                def _():
                    zero_block(ps_ref[e] + (nb_ref[e] - 1) * MOE_BLOCK)

            used = ps_ref[N_EXPERTS - 1] // MOE_BLOCK + nb_ref[N_EXPERTS - 1]

            def tail(j, _):
                zero_block(j * MOE_BLOCK)
                return 0

            lax.fori_loop(used, n_blocks, tail, 0)

        slot = i % 2
        subs = range(MOE_STEP_BLOCKS)
        r_iota = lax.broadcasted_iota(jnp.int32, (tb, STAGE_ROWS), 1)
        lrs = [lrow_ref[sb * tb:(sb + 1) * tb, :] for sb in subs]
        sels = [jnp.where((r_iota == lr[:, 0:1]) | (r_iota == lr[:, 1:2]), 1.0, 0.0).astype(BF16) for lr in lrs]
        for sb in subs:
            xs[slot, sb] = _dot_tn(sels[sb], h_ref[sb * tb:(sb + 1) * tb, :].astype(BF16))

        def copies_from(buf, sb):
            def make_copy(lo, gd, rows):
                return pltpu.make_async_copy(xs.at[buf, sb, pl.ds(lo, rows), :], xb_ref.at[pl.ds(gd, rows), :],
                                             sem.at[buf])
            return make_copy

        for sb in subs:
            _tile_copies(nt_ref, lo_ref, gd_ref, i * MOE_STEP_BLOCKS + sb, copies_from(slot, sb), wait=False)

        @pl.when(i > 0)
        def _():
            for sb in subs:
                _tile_copies(nt_ref, lo_ref, gd_ref, (i - 1) * MOE_STEP_BLOCKS + sb, copies_from(1 - slot, sb),
                             wait=True)

        @pl.when(i == pl.num_programs(0) - 1)
        def _():
            for sb in subs:
                _tile_copies(nt_ref, lo_ref, gd_ref, i * MOE_STEP_BLOCKS + sb, copies_from(slot, sb), wait=True)

    return kernel


def _dispatch(h2, lrow, pad_start, nblk, ntile, loff, gdest, n_blocks):
    t, d = h2.shape
    tb = MOE_STEP_BLOCKS * ROW_TILE
    grid_spec = pltpu.PrefetchScalarGridSpec(
        num_scalar_prefetch=5, grid=(t // tb,),
        in_specs=[pl.BlockSpec((tb, LANES), lambda i, *_: (i, 0)),
                  pl.BlockSpec((tb, d), lambda i, *_: (i, 0))],
        out_specs=pl.BlockSpec(memory_space=pl.ANY),
        scratch_shapes=[pltpu.VMEM((2, MOE_STEP_BLOCKS, STAGE_ROWS, d), F32), pltpu.VMEM((MOE_BLOCK, d), F32),
                        pltpu.SemaphoreType.DMA((3,))])
    return pl.pallas_call(
        _make_dispatch_kernel(ROW_TILE, n_blocks), grid_spec=grid_spec,
        out_shape=jax.ShapeDtypeStruct((n_blocks * MOE_BLOCK, d), F32),
        compiler_params=_cparams(("arbitrary",), 32),
        name="moe_dispatch",
    )(pad_start, nblk, ntile, loff, gdest, lrow, h2)


def _expert_kernel(be_ref, nu_ref, xb_ref, wg_ref, wu_ref, wd_ref, yb_ref):
    i = pl.program_id(0)

    @pl.when(i < nu_ref[0])
    def _():
        xb = xb_ref[...].astype(BF16)
        a = (_silu(_dot(xb, wg_ref[...])) * _dot(xb, wu_ref[...])).astype(BF16)
        yb_ref[...] = _dot(a, wd_ref[...]).astype(BF16).astype(F32)

    @pl.when(i >= nu_ref[0])
    def _():
        yb_ref[...] = jnp.zeros_like(yb_ref)


def _experts(xb, block_e, n_used, wg, wu, wd):
    _, d, f = wg.shape
    n_blocks = xb.shape[0] // MOE_BLOCK
    blk = pl.BlockSpec((MOE_BLOCK, d), lambda i, be, nu: (i, 0))
    grid_spec = pltpu.PrefetchScalarGridSpec(
        num_scalar_prefetch=2, grid=(n_blocks,),
        in_specs=[blk,
                  pl.BlockSpec((None, d, f), lambda i, be, nu: (be[i], 0, 0)),
                  pl.BlockSpec((None, d, f), lambda i, be, nu: (be[i], 0, 0)),
                  pl.BlockSpec((None, f, d), lambda i, be, nu: (be[i], 0, 0))],
        out_specs=blk)
    return pl.pallas_call(
        _expert_kernel, grid_spec=grid_spec,
        out_shape=jax.ShapeDtypeStruct(xb.shape, F32),
        compiler_params=_cparams(("arbitrary",), 56),
        name="moe_experts",
    )(block_e, n_used, xb, wg, wu, wd)


def _make_combine_kernel(tb):
    def kernel(nt_ref, lo_ref, gd_ref, yb_ref, lrow_ref, gt_ref, x_ref, g3_ref, fga_ref, o_ref, ys, sem):
        i = pl.program_id(0)
        slot = i % 2
        subs = range(MOE_STEP_BLOCKS)

        def copies_into(buf, sb):
            def make_copy(lo, gd, rows):
                return pltpu.make_async_copy(yb_ref.at[pl.ds(gd, rows), :], ys.at[buf, sb, pl.ds(lo, rows), :],
                                             sem.at[buf])
            return make_copy

        @pl.when(i == 0)
        def _():
            ys[...] = jnp.zeros_like(ys)
            for sb in subs:
                _tile_copies(nt_ref, lo_ref, gd_ref, sb, copies_into(0, sb), wait=False)

        @pl.when(i + 1 < pl.num_programs(0))
        def _():
            for sb in subs:
                _tile_copies(nt_ref, lo_ref, gd_ref, (i + 1) * MOE_STEP_BLOCKS + sb, copies_into(1 - slot, sb),
                             wait=False)

        for sb in subs:
            _tile_copies(nt_ref, lo_ref, gd_ref, i * MOE_STEP_BLOCKS + sb, copies_into(slot, sb), wait=True)

        r_iota = lax.broadcasted_iota(jnp.int32, (tb, STAGE_ROWS), 1)
        rows = [slice(sb * tb, (sb + 1) * tb) for sb in subs]
        lrs = [lrow_ref[rs, :] for rs in rows]
        gts = [gt_ref[rs, :] for rs in rows]
        qs = [jnp.where(r_iota == lr[:, 0:1], gt[:, 0:1], 0.0) + jnp.where(r_iota == lr[:, 1:2], gt[:, 1:2], 0.0)
              for lr, gt in zip(lrs, gts)]
        splits = [_split_bf16(q) for q in qs]
        ybs = [ys[slot, sb].astype(BF16) for sb in subs]
        fs = [_dot(qh, y) + _dot(ql, y) for (qh, ql), y in zip(splits, ybs)]
        for rs, f in zip(rows, fs):
            o_ref[rs, :] = x_ref[rs, :] + fga_ref[...] * _rms(f, g3_ref[...])

    return kernel


def _combine(yb, lrow, ntile, loff, gdest, gates, x, mods6, layer, first_row, rows_per_batch, g3):
    t, d = x.shape
    tb = MOE_STEP_BLOCKS * ROW_TILE
    if rows_per_batch is None:
        fga_map = lambda i, *_: (layer, first_row, 5, 0, 0)
    else:
        steps_per_batch = rows_per_batch // MOE_STEP_BLOCKS
        fga_map = lambda i, *_: (layer, first_row + i // steps_per_batch, 5, 0, 0)
    grid_spec = pltpu.PrefetchScalarGridSpec(
        num_scalar_prefetch=3, grid=(t // tb,),
        in_specs=[pl.BlockSpec(memory_space=pl.ANY),
                  pl.BlockSpec((tb, LANES), lambda i, *_: (i, 0)),
                  pl.BlockSpec((tb, LANES), lambda i, *_: (i, 0)),
                  pl.BlockSpec((tb, d), lambda i, *_: (i, 0)),
                  pl.BlockSpec((1, d), lambda i, *_: (0, 0)),
                  pl.BlockSpec((None, None, None, 1, d), fga_map)],
        out_specs=pl.BlockSpec((tb, d), lambda i, *_: (i, 0)),
        scratch_shapes=[pltpu.VMEM((2, MOE_STEP_BLOCKS, STAGE_ROWS, d), F32), pltpu.SemaphoreType.DMA((2,))])
    return pl.pallas_call(
        _make_combine_kernel(ROW_TILE), grid_spec=grid_spec,
        out_shape=jax.ShapeDtypeStruct((t, d), F32),
        compiler_params=_cparams(("arbitrary",), 48),
        name="moe_combine",
    )(ntile, loff, gdest, yb, lrow, gates, x, g3, mods6)


ROUTER_BLOCKS = 4


def _router(logits):
    t = logits.shape[0]
    tb = ROUTER_BLOCKS * ROW_TILE
    blk = pl.BlockSpec((tb, LANES), lambda i: (i, 0))
    return pl.pallas_call(
        _router_kernel,
        grid=(t // tb,),
        in_specs=[blk],
        out_specs=[blk, blk, pl.BlockSpec((ROUTER_BLOCKS * SUBLANES, LANES), lambda i: (i, 0)),
                   pl.BlockSpec((1, LANES), lambda i: (0, 0))],
        out_shape=[jax.ShapeDtypeStruct((t, LANES), jnp.int32), jax.ShapeDtypeStruct((t, LANES), F32),
                   jax.ShapeDtypeStruct((t // ROW_TILE * SUBLANES, LANES), jnp.int32),
                   jax.ShapeDtypeStruct((1, LANES), jnp.int32)],
        scratch_shapes=[pltpu.VMEM((1, LANES), F32)],
        compiler_params=_cparams(("arbitrary",), 16),
        name="moe_router",
    )(logits)


def _moe(h2, logits, x, mods6, layer, first_row, rows_per_batch, g3, wg, wu, wd):
    t = x.shape[0]
    n_tok_blocks = t // ROW_TILE
    max_rows = t * TOP_K + n_tok_blocks * N_EXPERTS * (SUBLANES - 1)
    n_blocks = -(-max_rows // MOE_BLOCK) + N_EXPERTS
    lrow, gates, tab, tot = _router(logits)
    tab = tab.reshape(n_tok_blocks, SUBLANES, LANES)[:, :, :N_EXPERTS]
    ntile, loff, prior = tab[:, 0], tab[:, 1], tab[:, 2]
    nblk = (tot[0, :N_EXPERTS] + MOE_BLOCK - 1) // MOE_BLOCK
    blk_end = jnp.cumsum(nblk)
    pad_start = ((blk_end - nblk) * MOE_BLOCK).astype(jnp.int32)
    gdest = (pad_start[None, :] + prior).astype(jnp.int32)
    n_used = blk_end[-1:].astype(jnp.int32)
    blk = jnp.minimum(jnp.arange(n_blocks, dtype=jnp.int32), n_used[0] - 1)
    block_e = jnp.minimum(jnp.sum(blk[:, None] >= blk_end[None, :], axis=1), N_EXPERTS - 1).astype(jnp.int32)
    ntile, loff, gdest = ntile.reshape(-1), loff.reshape(-1), gdest.reshape(-1)
    xb = _dispatch(h2, lrow, pad_start, nblk.astype(jnp.int32), ntile, loff, gdest, n_blocks)
    yb = _experts(xb, block_e, n_used, wg, wu, wd)
    return _combine(yb, lrow, ntile, loff, gdest, gates, x, mods6, layer, first_row, rows_per_batch, g3)


def _pad_cols(w, n):
    return jnp.pad(w, ((0, 0), (0, n - w.shape[1])))


def kernel(x_prompt, x_sample, state_mlstm_C, state_mlstm_n, state_mlstm_m, state_ret_S, c, c_ctx, mod_w, mod_b, norm_g, mlstm_w_in, mlstm_gate_b, mlstm_conv_w, mlstm_conv_b, mlstm_head_g, mlstm_w_out, ret_w_in, ret_decay_logit, ret_head_g, ret_w_out, ffn_w_gate, ffn_w_up, ffn_w_down, moe_router, moe_w_gate, moe_w_up, moe_w_down):
    bp, n_p, d = x_prompt.shape
    bs, n_s, _ = x_sample.shape
    depth = mod_w.shape[0]
    assert depth == 2 and ROW_TILE % GRID_W == 0 and n_s % GRID_W == 0

    cond = jnp.zeros((MOD_ROWS, d), F32).at[0].set(c_ctx).at[1:1 + bs].set(c)
    mods6 = _modulation(cond, mod_w, mod_b).reshape(depth, MOD_ROWS, N_MOD, 1, d)

    groups = [dict(x=x_prompt.reshape(bp * n_p, d), first=0, rpb=None, nseq=bp, ntok=n_p, prompt=True),
              dict(x=x_sample.reshape(bs * n_s, d), first=1, rpb=n_s // ROW_TILE, nseq=bs, ntok=n_s, prompt=False)]

    j = 0
    ml_qk = (mlstm_w_in.shape[2] - 4 * HEADS) // 2
    ml_v = ml_qk // 2
    w_in = mlstm_w_in[j]
    wqk = w_in[:, :ml_qk].astype(BF16)
    wv = w_in[:, ml_qk:ml_qk + ml_v].astype(BF16)
    wo = w_in[:, ml_qk + ml_v:ml_qk + 2 * ml_v].astype(BF16)
    w_gate = w_in[:, ml_qk + 2 * ml_v:]
    wgh, wgl = _split_bf16(_pad_cols(w_gate, LANES))
    bcol = _pad_cols(mlstm_gate_b[j][None, :], LANES)
    g = norm_g[0]
    w_out0 = mlstm_w_out[j].astype(BF16)
    new_c = new_n = new_m = None
    casts = (ffn_w_gate[j], ffn_w_up[j], ffn_w_down[j], ret_w_in[j], ret_w_out[j])
    proj = {}
    for grp in reversed(groups):
        args = (mods6, 0, grp["first"], grp["rpb"])
        outs = _proj_mlstm(grp["x"], grp["ntok"], *args, g[0:1], wqk, wv, wo, wgh, wgl, bcol, w_gate.shape[1],
                           mlstm_conv_w[j], mlstm_conv_b[j][None, :], casts=() if grp["prompt"] else casts)
        proj[grp["prompt"]] = outs[:5]
        if not grp["prompt"]:
            fwg, fwu, fwd, rw_in, w_out1 = outs[5:]
    for grp in groups:
        args = (mods6, 0, grp["first"], grp["rpb"])
        q, kt, v, o, grow = proj[grp["prompt"]]
        grow3 = grow.reshape(grow.shape[0], grp["nseq"], grp["ntok"] // CHUNK, CHUNK)
        if grp["prompt"]:
            m0 = jnp.zeros((grp["nseq"] * 2 * HEADS,), F32)
            hs, new_c, new_n, new_m = _mlstm_scan(q, kt, v, grow3, m0, None, grp["nseq"], grp["ntok"], True)
        else:
            (hs,) = _mlstm_scan(q, kt, v, grow3, state_mlstm_m[:, j].reshape(-1),
                                (state_mlstm_C[:, j], state_mlstm_n[:, j]), grp["nseq"], grp["ntok"], False)
        (grp["x"],) = _mixer_out(hs, o, mlstm_head_g[j][None, :], w_out0, grp["x"], *args, g[1:2],
                                 ffn=(g[2:3], g[3:4], fwg, fwu, fwd))

    ret_qk = ret_w_in.shape[2] // 3
    g = norm_g[1]
    wrh, wrl = _split_bf16(_pad_cols(moe_router[j], LANES))
    decay_flat = ret_decay_logit[j].reshape(-1)
    new_s = None
    n_exp, _, d_ff = moe_w_gate.shape[1:]
    casts = (moe_w_gate[j].reshape(n_exp * d, d_ff), moe_w_up[j].reshape(n_exp * d, d_ff),
             moe_w_down[j].reshape(n_exp * d_ff, d))
    proj = {}
    for grp in reversed(groups):
        args = (mods6, 1, grp["first"], grp["rpb"])
        rope_tabs = None if grp["prompt"] else _rope_tables(grp["ntok"], ret_qk // (2 * HEADS))
        outs = _proj_ret(grp["x"], grp["ntok"], *args, g[0:1], rw_in, ret_qk // 2, ret_qk, rope_tabs,
                         casts=() if grp["prompt"] else casts)
        proj[grp["prompt"]] = outs[:4]
        if not grp["prompt"]:
            ewg, ewu, ewd = (w.reshape(n_exp, -1, w.shape[1]) for w in outs[4:])
    for grp in groups:
        args = (mods6, 1, grp["first"], grp["rpb"])
        q, kt, v, gate = proj[grp["prompt"]]
        if grp["prompt"]:
            hs, new_s = _ret_scan(q, kt, v, decay_flat, None, grp["nseq"], grp["ntok"], True)
        else:
            (hs,) = _ret_scan(q, kt, v, decay_flat, state_ret_S[:, j], grp["nseq"], grp["ntok"], False)
        x1, h2, logits = _mixer_out(hs, gate, ret_head_g[j][None, :], w_out1, grp["x"], *args, g[1:2],
                                    router=(g[2:3], wrh, wrl))
        grp["x"] = _moe(h2, logits, x1, *args, g[3:4], ewg, ewu, ewd)

    y_prompt = groups[0]["x"].reshape(bp, n_p, d)
    y_sample = groups[1]["x"].reshape(bs, n_s, d)
    return (y_prompt, y_sample, new_c[:, None], new_n[:, None], new_m[:, None, :, :, 0], new_s[:, None])
```
